```python
import math
import jax, jax.numpy as jnp
from jax import lax
import numpy as np

D_MODEL = 2048
BATCH = 4
SEQ = 4096
DEPTH = 2

GRID_W = 64
CTX_LEN = 256
N_MIXERS = 2
N_DIRS = 2
RWKV_HEAD = 64
RWKV_HEADS = D_MODEL // RWKV_HEAD
DECAY_LORA = max(32, int(round(1.8 * D_MODEL ** 0.5 / 32)) * 32)
AAA_LORA = max(32, int(round(1.8 * D_MODEL ** 0.5 / 32)) * 32)
GATE_LORA = max(32, int(round(0.6 * D_MODEL ** 0.8 / 32)) * 32)
GN_EPS = 64e-5
L2_EPS = 1e-24
DIFF_HEAD = 64
DIFF_HEADS = D_MODEL // (2 * DIFF_HEAD)
ROPE_BASE = 10000.0
Q_BLOCK = 128
QK_EPS = 1e-6
SUBLN_EPS = 1e-5
D_FF = -(-8 * D_MODEL // (3 * 256)) * 256
NORM_EPS = 1e-6

kernel_name = "hybrid_rwkv7_diffattn_prefix_dit"


def rms_norm(x, g, eps=NORM_EPS):
    xf = x.astype(jnp.float32)
    y = xf * lax.rsqrt(jnp.mean(xf * xf, axis=-1, keepdims=True) + eps)
    return (y * g.astype(jnp.float32)).astype(x.dtype)


def adaln(cvec, w_mod, b_mod):
    m = jax.nn.silu(cvec) @ w_mod + b_mod
    return jnp.split(m[..., None, :], 6, axis=-1)


def modulate(h, shift, scale):
    return h * (1.0 + scale) + shift


def swiglu(h, wg, wu, wd):
    return (jax.nn.silu(h @ wg) * (h @ wu)) @ wd


def centred_shift_delta(x):
    xp = jnp.pad(x, ((0, 0), (1, 1), (0, 0)))
    return 0.5 * (xp[:, :-2] + xp[:, 2:]) - x


def rwkv_inputs(h, mu, wr, wk, wv, w0, w1, w2, a0, a1, a2, g1, g2, kk_scale, ka):
    B, T, _ = h.shape
    f32 = jnp.float32
    heads = lambda t: t.astype(f32).reshape(B, T, RWKV_HEADS, RWKV_HEAD)
    xx = centred_shift_delta(h)
    xr, xw, xk, xv, xa, xg = (h + xx * mu[n] for n in range(6))
    r = heads(xr @ wr)
    k = (xk @ wk).astype(f32)
    v = heads(xv @ wv)
    g = jax.nn.sigmoid(xg @ g1) @ g2
    kk = heads(k * kk_scale)
    kk = kk * lax.rsqrt(jnp.maximum(jnp.sum(kk * kk, axis=-1, keepdims=True), L2_EPS))
    dirs = []
    for d in range(N_DIRS):
        w_log = -jax.nn.softplus(-(w0[d] + jnp.tanh(xw @ w1[d]) @ w2[d]).astype(f32)) - 0.5
        a = jax.nn.sigmoid((a0[d] + (xa @ a1[d]) @ a2[d]).astype(f32))
        k_d = k * (1.0 + (a - 1.0) * ka)
        dirs.append((heads(jnp.exp(-jnp.exp(w_log))), heads(k_d), heads(a)))
    return r, v, g, kk, dirs


def wkv7_scan(s0, r, dec, k, v, a, b, reverse):
    xs = tuple(jnp.moveaxis(t, 1, 0) for t in (r, dec, k, v, a, b))

    def step(S, inp):
        r_t, d_t, k_t, v_t, a_t, b_t = inp
        sa = jnp.einsum('bhvk,bhk->bhv', S, a_t)
        S = S * d_t[:, :, None, :] + sa[..., None] * b_t[:, :, None, :] + v_t[..., None] * k_t[:, :, None, :]
        return S, jnp.einsum('bhvk,bhk->bhv', S, r_t)

    s_final, y = lax.scan(step, s0, xs, reverse=reverse)
    return s_final, jnp.moveaxis(y, 0, 1)


def rwkv_readout(y, r, v, g, k_dirs, r_k, ln_w, ln_b, w_o, dtype):
    B, T, H, N = y.shape
    mean = jnp.mean(y, axis=-1, keepdims=True)
    var = jnp.mean(jnp.square(y - mean), axis=-1, keepdims=True)
    yn = ((y - mean) * lax.rsqrt(var + GN_EPS)).reshape(B, T, H * N) * ln_w + ln_b
    bonus = jnp.sum(r * (k_dirs[0] + k_dirs[1]) * r_k.astype(jnp.float32), axis=-1, keepdims=True) * v
    return ((yn + bonus.reshape(B, T, H * N)) * g).astype(dtype) @ w_o


def rwkv_mixer(h_ctx, h_lat, branch_params, readout_params, need_ctx):
    r_c, v_c, g_c, kk_c, dirs_c = rwkv_inputs(h_ctx, *branch_params)
    r_l, v_l, g_l, kk_l, dirs_l = rwkv_inputs(h_lat, *branch_params)
    B = h_lat.shape[0]
    s0 = jnp.zeros((B, RWKV_HEADS, RWKV_HEAD, RWKV_HEAD), jnp.float32)
    ys_c, ys_l = [], []
    for (dec_c, k_c, a_c), (dec_l, k_l, a_l), rev in zip(dirs_c, dirs_l, (False, True)):
        s_ctx, y_c = wkv7_scan(s0, r_c, dec_c, k_c, v_c, -kk_c, kk_c * a_c, rev)
        _, y_l = wkv7_scan(s_ctx, r_l, dec_l, k_l, v_l, -kk_l, kk_l * a_l, rev)
        ys_c.append(y_c)
        ys_l.append(y_l)
    out_l = rwkv_readout(ys_l[0] + ys_l[1], r_l, v_l, g_l, [d[1] for d in dirs_l], *readout_params, dtype=h_lat.dtype)
    out_c = None
    if need_ctx:
        out_c = rwkv_readout(ys_c[0] + ys_c[1], r_c, v_c, g_c, [d[1] for d in dirs_c], *readout_params, dtype=h_ctx.dtype)
    return out_l, out_c


def axial_angles(n_tokens):
    rows = n_tokens // GRID_W
    row, col = jnp.meshgrid(jnp.arange(rows), jnp.arange(GRID_W), indexing='ij')
    n_freq = DIFF_HEAD // 4
    freqs = ROPE_BASE ** (-jnp.arange(n_freq, dtype=jnp.float32) / n_freq)
    ang_row = row.reshape(-1).astype(jnp.float32)[:, None] * freqs
    ang_col = col.reshape(-1).astype(jnp.float32)[:, None] * freqs
    return ang_row, ang_col


def rotate(u, ang):
    cos = jnp.cos(ang)[:, None, None, :]
    sin = jnp.sin(ang)[:, None, None, :]
    u1, u2 = jnp.split(u, 2, axis=-1)
    return jnp.concatenate([u1 * cos - u2 * sin, u1 * sin + u2 * cos], axis=-1)


def axial_rope(t, ang_row, ang_col):
    half = DIFF_HEAD // 2
    tf = t.astype(jnp.float32)
    out = jnp.concatenate([rotate(tf[..., :half], ang_row), rotate(tf[..., half:], ang_col)], axis=-1)
    return out.astype(t.dtype)


def diff_qkv(h, w_qkv, q_gain, k_gain):
    B, T, _ = h.shape
    q, k, v = jnp.split(h @ w_qkv, 3, axis=-1)
    q = rms_norm(q.reshape(B, T, DIFF_HEADS, 2, DIFF_HEAD), q_gain, QK_EPS)
    k = rms_norm(k.reshape(B, T, DIFF_HEADS, 2, DIFF_HEAD), k_gain, QK_EPS)
    v = v.reshape(B, T, DIFF_HEADS, 2 * DIFF_HEAD)
    return q, k, v


def diff_attend(q, k, v, lam):
    s = jnp.einsum('bqhmd,bkhmd->bhmqk', q, k).astype(jnp.float32) * (DIFF_HEAD ** -0.5)
    p = jax.nn.softmax(s, axis=-1)
    attn = p[:, :, 0] - lam * p[:, :, 1]
    return jnp.einsum('bhqk,bkhe->bqhe', attn.astype(v.dtype), v)


def diff_attn_mixer(h_ctx, h_lat, w_qkv, q_gain, k_gain, lq1, lk1, lq2, lk2, sub_gain, w_o,
                    lambda_init, ang_row, ang_col, need_ctx):
    B, T, _ = h_lat.shape
    q_c, k_c, v_c = diff_qkv(h_ctx, w_qkv, q_gain, k_gain)
    q_l, k_l, v_l = diff_qkv(h_lat, w_qkv, q_gain, k_gain)
    q_l = axial_rope(q_l, ang_row, ang_col)
    k_l = axial_rope(k_l, ang_row, ang_col)
    f32 = jnp.float32
    lam = (jnp.exp(jnp.sum(lq1.astype(f32) * lk1.astype(f32)))
           - jnp.exp(jnp.sum(lq2.astype(f32) * lk2.astype(f32))) + lambda_init)
    k_all = jnp.concatenate([k_c, k_l], axis=1)
    v_all = jnp.concatenate([v_c, v_l], axis=1)
    nb = T // Q_BLOCK
    q_blocks = jnp.moveaxis(q_l.reshape(B, nb, Q_BLOCK, DIFF_HEADS, 2, DIFF_HEAD), 1, 0)
    o_l = lax.map(lambda qb: diff_attend(qb, k_all, v_all, lam), q_blocks)
    o_l = jnp.moveaxis(o_l, 0, 1).reshape(B, T, DIFF_HEADS, 2 * DIFF_HEAD)

    def readout(o):
        o = rms_norm(o, sub_gain, SUBLN_EPS) * (1.0 - lambda_init)
        return o.reshape(o.shape[0], o.shape[1], -1) @ w_o

    out_l = readout(o_l)
    out_c = readout(diff_attend(q_c, k_c, v_c, lam)) if need_ctx else None
    return out_l, out_c


def setup_inputs(seed: int = 0) -> dict:
    key = jax.random.key(seed)
    ks = iter(jax.random.split(key, 48))
    f32 = jnp.float32
    D = D_MODEL
    n_r = (DEPTH + 1) // 2
    n_d = DEPTH // 2
    nrm = lambda shape, scale: jax.random.normal(next(ks), shape, f32) * scale
    ramp = jnp.linspace(0.0, 1.0, D, dtype=f32) ** 0.9
    inp = {}
    inp['x'] = nrm((BATCH, SEQ, D), 1.0)
    inp['c'] = nrm((BATCH, D), 1.0)
    inp['ctx'] = nrm((BATCH, CTX_LEN, D), 1.0)
    inp['c_ctx'] = nrm((D,), 1.0)
    inp['mod_w'] = nrm((DEPTH, D, 6 * D), 0.5 * D ** -0.5)
    inp['mod_b'] = nrm((DEPTH, 6 * D), 0.01)
    inp['norm1_g'] = 1.0 + nrm((DEPTH, D), 0.1)
    inp['norm2_g'] = 1.0 + nrm((DEPTH, D), 0.1)
    inp['rwkv_mu'] = jax.random.uniform(next(ks), (n_r, 6, D), f32)
    inp['rwkv_wr'] = nrm((n_r, D, D), D ** -0.5)
    inp['rwkv_wk'] = nrm((n_r, D, D), D ** -0.5)
    inp['rwkv_wv'] = nrm((n_r, D, D), D ** -0.5)
    inp['rwkv_wo'] = nrm((n_r, D, D), D ** -0.5)
    inp['rwkv_w0'] = (-6.0 + 5.0 * ramp) + nrm((n_r, N_DIRS, D), 0.1)
    inp['rwkv_w1'] = nrm((n_r, N_DIRS, D, DECAY_LORA), D ** -0.5)
    inp['rwkv_w2'] = nrm((n_r, N_DIRS, DECAY_LORA, D), 0.1 * DECAY_LORA ** -0.5)
    inp['rwkv_a0'] = nrm((n_r, N_DIRS, D), 0.1)
    inp['rwkv_a1'] = nrm((n_r, N_DIRS, D, AAA_LORA), D ** -0.5)
    inp['rwkv_a2'] = nrm((n_r, N_DIRS, AAA_LORA, D), 0.1 * AAA_LORA ** -0.5)
    inp['rwkv_g1'] = nrm((n_r, D, GATE_LORA), D ** -0.5)
    inp['rwkv_g2'] = nrm((n_r, GATE_LORA, D), GATE_LORA ** -0.5)
    inp['rwkv_kk'] = 0.85 + nrm((n_r, D), 0.05)
    inp['rwkv_ka'] = 1.0 + nrm((n_r, D), 0.05)
    inp['rwkv_rk'] = nrm((n_r, RWKV_HEADS, RWKV_HEAD), 0.05)
    inp['rwkv_lnw'] = 1.0 + nrm((n_r, D), 0.1)
    inp['rwkv_lnb'] = nrm((n_r, D), 0.01)
    inp['diff_wqkv'] = nrm((n_d, D, 3 * D), D ** -0.5)
    inp['diff_qn'] = 1.0 + nrm((n_d, DIFF_HEAD), 0.1)
    inp['diff_kn'] = 1.0 + nrm((n_d, DIFF_HEAD), 0.1)
    inp['diff_lq1'] = nrm((n_d, DIFF_HEAD), 0.1)
    inp['diff_lk1'] = nrm((n_d, DIFF_HEAD), 0.1)
    inp['diff_lq2'] = nrm((n_d, DIFF_HEAD), 0.1)
    inp['diff_lk2'] = nrm((n_d, DIFF_HEAD), 0.1)
    inp['diff_subln'] = 1.0 + nrm((n_d, 2 * DIFF_HEAD), 0.1)
    inp['diff_wo'] = nrm((n_d, D, D), D ** -0.5)
    inp['ffn_wg'] = nrm((DEPTH, D, D_FF), D ** -0.5)
    inp['ffn_wu'] = nrm((DEPTH, D, D_FF), D ** -0.5)
    inp['ffn_wd'] = nrm((DEPTH, D_FF, D), D_FF ** -0.5)
    return inp


def reference(x, c, ctx, c_ctx, mod_w, mod_b, norm1_g, norm2_g,
              rwkv_mu, rwkv_wr, rwkv_wk, rwkv_wv, rwkv_wo, rwkv_w0, rwkv_w1, rwkv_w2,
              rwkv_a0, rwkv_a1, rwkv_a2, rwkv_g1, rwkv_g2, rwkv_kk, rwkv_ka, rwkv_rk, rwkv_lnw, rwkv_lnb,
              diff_wqkv, diff_qn, diff_kn, diff_lq1, diff_lk1, diff_lq2, diff_lk2, diff_subln, diff_wo,
              ffn_wg, ffn_wu, ffn_wd):
    n_lat = x.shape[1]
    ang_row, ang_col = axial_angles(n_lat)
    xl, xc = x, ctx
    for i in range(DEPTH):
        last = i == DEPTH - 1
        sh1, sc1, gt1, sh2, sc2, gt2 = adaln(c, mod_w[i], mod_b[i])
        csh1, csc1, cgt1, csh2, csc2, cgt2 = adaln(c_ctx, mod_w[i], mod_b[i])
        hl = modulate(rms_norm(xl, norm1_g[i]), sh1, sc1)
        hc = modulate(rms_norm(xc, norm1_g[i]), csh1, csc1)
        j = i // N_MIXERS
        if i % N_MIXERS == 0:
            branch = (rwkv_mu[j], rwkv_wr[j], rwkv_wk[j], rwkv_wv[j], rwkv_w0[j], rwkv_w1[j], rwkv_w2[j],
                      rwkv_a0[j], rwkv_a1[j], rwkv_a2[j], rwkv_g1[j], rwkv_g2[j], rwkv_kk[j], rwkv_ka[j])
            readout = (rwkv_rk[j], rwkv_lnw[j], rwkv_lnb[j], rwkv_wo[j])
            ol, oc = rwkv_mixer(hc, hl, branch, readout, not last)
        else:
            lambda_init = 0.8 - 0.6 * math.exp(-0.3 * i)
            ol, oc = diff_attn_mixer(hc, hl, diff_wqkv[j], diff_qn[j], diff_kn[j], diff_lq1[j], diff_lk1[j],
                                     diff_lq2[j], diff_lk2[j], diff_subln[j], diff_wo[j], lambda_init,
                                     ang_row, ang_col, not last)
        xl = xl + gt1 * ol
        hl2 = modulate(rms_norm(xl, norm2_g[i]), sh2, sc2)
        xl = xl + gt2 * swiglu(hl2, ffn_wg[i], ffn_wu[i], ffn_wd[i])
        if not last:
            xc = xc + cgt1 * oc
            hc2 = modulate(rms_norm(xc, norm2_g[i]), csh2, csc2)
            xc = xc + cgt2 * swiglu(hc2, ffn_wg[i], ffn_wu[i], ffn_wd[i])
    return xl
```

```python
import functools
import math

import jax
import jax.numpy as jnp
from jax import lax
from jax.experimental import pallas as pl
from jax.experimental.pallas import tpu as pltpu

F32 = jnp.float32
BF16 = jnp.bfloat16

HEAD = 64
GROUP = 256
CHUNK = 64
SUBLANES = 8
V7X_VMEM_CAP = 56 * 1024 * 1024

NORM_EPS = 1e-6
GN_EPS = 64e-5
L2_EPS = 1e-24
QK_EPS = 1e-6
SUBLN_EPS = 1e-5
ROPE_BASE = 10000.0
GRID_W = 64
DECAY_SCALE = math.exp(-0.5)


def _cparams(semantics, vmem_bytes):
    return pltpu.CompilerParams(dimension_semantics=semantics,
                                vmem_limit_bytes=int(min(max(vmem_bytes, 16 << 20), V7X_VMEM_CAP)))


def _dot(a, b):
    return jnp.dot(a, b, preferred_element_type=F32)


def _dot_nt(a, b):
    return lax.dot_general(a, b, (((1,), (1,)), ((), ())), preferred_element_type=F32)


def _dot_tn(a, b):
    return lax.dot_general(a, b, (((0,), (0,)), ((), ())), preferred_element_type=F32)


def _split_dot(x, w_bf16):
    hi = x.astype(BF16)
    lo = (x - hi.astype(F32)).astype(BF16)
    return _dot(hi, w_bf16) + _dot(lo, w_bf16)


def _block_mask(n):
    r = lax.broadcasted_iota(jnp.int32, (n, n), 0) // HEAD
    c = lax.broadcasted_iota(jnp.int32, (n, n), 1) // HEAD
    return r == c


def _adaln_kernel(c_ref, w_ref, b_ref, o_ref):
    c = c_ref[...]
    s = (c * jax.nn.sigmoid(c)).astype(BF16)
    o_ref[...] = _dot(s, w_ref[...].astype(BF16)) + b_ref[...]


def _adaln(cvecs, mod_w, mod_b):
    depth, d, n = mod_w.shape
    rows = cvecs.shape[0]
    tn = 1024
    return pl.pallas_call(
        _adaln_kernel,
        grid=(depth, n // tn),
        in_specs=[pl.BlockSpec((rows, d), lambda l, j: (0, 0)),
                  pl.BlockSpec((None, d, tn), lambda l, j: (l, 0, j)),
                  pl.BlockSpec((None, 1, tn), lambda l, j: (l, 0, j))],
        out_specs=pl.BlockSpec((None, rows, tn), lambda l, j: (l, 0, j)),
        out_shape=jax.ShapeDtypeStruct((depth, rows, n), F32),
        compiler_params=_cparams(("parallel", "parallel"), 3 * d * tn * 4 + (8 << 20)),
        name="adaln",
    )(cvecs, mod_w, mod_b.reshape(depth, 1, n))


def _norm_mod_value(x, g, sh, sc):
    y = x * lax.rsqrt(jnp.mean(x * x, axis=-1, keepdims=True) + NORM_EPS)
    return (y * g) * (1.0 + sc) + sh


def _norm_mod_kernel(x_ref, g_ref, sh_ref, sc_ref, o_ref):
    o_ref[...] = _norm_mod_value(x_ref[...], g_ref[...], sh_ref[...], sc_ref[...]).astype(o_ref.dtype)


def _row_spec(arr, d):
    if arr.shape[0] == 1:
        return pl.BlockSpec((None, 1, d), lambda b, *_: (0, 0, 0))
    return pl.BlockSpec((None, 1, d), lambda b, *_: (b, 0, 0))


def _norm_mod(x, g, sh, sc, out_dtype):
    b, t, d = x.shape
    tr = min(t, 512)
    return pl.pallas_call(
        _norm_mod_kernel,
        grid=(b, t // tr),
        in_specs=[pl.BlockSpec((None, tr, d), lambda i, j: (i, j, 0)),
                  pl.BlockSpec((1, d), lambda i, j: (0, 0)),
                  _row_spec(sh, d), _row_spec(sc, d)],
        out_specs=pl.BlockSpec((None, tr, d), lambda i, j: (i, j, 0)),
        out_shape=jax.ShapeDtypeStruct((b, t, d), out_dtype),
        compiler_params=_cparams(("parallel", "parallel"), 6 * tr * d * 4),
        name="norm_mod",
    )(x, g.reshape(1, d), sh, sc)


def _shift_delta(h, prev_ref, next_ref, t, nt):
    tm = h.shape[0]
    row = lax.broadcasted_iota(jnp.int32, h.shape, 0)
    prev_row = jnp.where(t > 0, prev_ref[SUBLANES - 1:SUBLANES, :], 0.0)
    next_row = jnp.where(t < nt - 1, next_ref[0:1, :], 0.0)
    hp = jnp.where(row == 0, prev_row, pltpu.roll(h, 1, 0))
    hn = jnp.where(row == tm - 1, next_row, pltpu.roll(h, tm - 1, 0))
    return 0.5 * (hp + hn) - h


def _halo_specs(t, tm, d):
    per = tm // SUBLANES
    last = t // SUBLANES - 1
    return [pl.BlockSpec((None, tm, d), lambda b, i, *_: (b, i, 0)),
            pl.BlockSpec((None, SUBLANES, d), lambda b, i, *_: (b, jnp.maximum(i * per - 1, 0), 0)),
            pl.BlockSpec((None, SUBLANES, d), lambda b, i, *_: (b, jnp.minimum((i + 1) * per, last), 0))]


def _rkv_kernel(h_ref, hp_ref, hn_ref, mu_ref, wr_ref, wk_ref, wv_ref, r_ref, k_ref, v_ref, xs_ref, *, nt):
    t = pl.program_id(1)

    @pl.when(pl.program_id(2) == 0)
    def _():
        h = h_ref[...]
        xx = _shift_delta(h, hp_ref, hn_ref, t, nt)
        xs_ref[0] = (h + xx * mu_ref[0:1, :]).astype(BF16)
        xs_ref[1] = (h + xx * mu_ref[2:3, :]).astype(BF16)
        xs_ref[2] = (h + xx * mu_ref[3:4, :]).astype(BF16)

    r_ref[...] = _dot(xs_ref[0], wr_ref[...])
    k_ref[...] = _dot(xs_ref[1], wk_ref[...])
    v_ref[...] = _dot(xs_ref[2], wv_ref[...])


def _rwkv_rkv(h, mu, wr, wk, wv):
    b, t, d = h.shape
    tm = min(t, 512)
    tn = 512
    nt = t // tm
    out = jax.ShapeDtypeStruct((b, t, d), F32)
    wspec = pl.BlockSpec((d, tn), lambda i, j, n: (0, n))
    ospec = pl.BlockSpec((None, tm, tn), lambda i, j, n: (i, j, n))
    vmem = 2 * tm * d * 4 + 3 * tm * d * 2 + 6 * d * tn * 2 + 6 * tm * tn * 4 + 3 * tm * d * 4 + (4 << 20)
    return pl.pallas_call(
        functools.partial(_rkv_kernel, nt=nt),
        grid=(b, nt, d // tn),
        in_specs=_halo_specs(t, tm, d) + [pl.BlockSpec((6, d), lambda i, j, n: (0, 0)), wspec, wspec, wspec],
        out_specs=[ospec, ospec, ospec],
        out_shape=[out, out, out],
        scratch_shapes=[pltpu.VMEM((3, tm, d), BF16)],
        compiler_params=_cparams(("parallel", "parallel", "arbitrary"), vmem),
        name="rwkv_rkv",
    )(h, h, h, mu, wr, wk, wv)


def _lora_kernel(h_ref, hp_ref, hn_ref, mu_ref, w0_ref, w1_ref, w2_ref, a0_ref, a1_ref, a2_ref, g1_ref, g2_ref,
                 g_ref, lw0_ref, lw1_ref, as0_ref, as1_ref, *, nt):
    t = pl.program_id(1)
    h = h_ref[...]
    xx = _shift_delta(h, hp_ref, hn_ref, t, nt)
    xw = (h + xx * mu_ref[1:2, :]).astype(BF16)
    xa = (h + xx * mu_ref[4:5, :]).astype(BF16)
    xg = (h + xx * mu_ref[5:6, :]).astype(BF16)
    g_ref[...] = _dot(jax.nn.sigmoid(_dot(xg, g1_ref[...])).astype(BF16), g2_ref[...])
    for d, (lw_ref, as_ref) in enumerate(((lw0_ref, as0_ref), (lw1_ref, as1_ref))):
        w_pre = w0_ref[d:d + 1, :] + _dot(jnp.tanh(_dot(xw, w1_ref[d])).astype(BF16), w2_ref[d])
        lw_ref[...] = -DECAY_SCALE * jax.nn.sigmoid(w_pre)
        a_pre = a0_ref[d:d + 1, :] + _dot(_dot(xa, a1_ref[d]).astype(BF16), a2_ref[d])
        as_ref[...] = jax.nn.sigmoid(a_pre)


def _pad_lora(w_in, w_out):
    rank = w_in.shape[-1]
    pad = (-rank) % 128
    w_in = jnp.pad(w_in, [(0, 0)] * (w_in.ndim - 1) + [(0, pad)])
    w_out = jnp.pad(w_out, [(0, 0)] * (w_out.ndim - 2) + [(0, pad), (0, 0)])
    return w_in.astype(BF16), w_out.astype(BF16)


def _rwkv_lora(h, mu, w0, w1, w2, a0, a1, a2, g1, g2):
    b, t, d = h.shape
    tm = min(t, 256)
    nt = t // tm
    out = jax.ShapeDtypeStruct((b, t, d), F32)
    ospec = pl.BlockSpec((None, tm, d), lambda i, j: (i, j, 0))

    def full(a):
        return pl.BlockSpec(a.shape, lambda i, j, _n=a.ndim: (0,) * _n)

    params = (mu, w0, w1, w2, a0, a1, a2, g1, g2)
    vmem = 2 * 6 * tm * d * 4 + 10 * tm * d * 4 + 2 * sum(p.size * p.dtype.itemsize for p in params)
    return pl.pallas_call(
        functools.partial(_lora_kernel, nt=nt),
        grid=(b, nt),
        in_specs=_halo_specs(t, tm, d) + [full(p) for p in params],
        out_specs=[ospec] * 5,
        out_shape=[out] * 5,
        compiler_params=_cparams(("parallel", "parallel"), vmem),
        name="rwkv_lora",
    )(h, h, h, *params)


def _scan_group(d, refs, kks, ka, state_ref, y_ref, sl, g, consts):
    r_ref, k_ref, v_ref, lw_ref, as_ref = refs
    bmask, ones_bd, tri, m_strict, m_incl = consts
    c = CHUNK

    def stack(x):
        return jnp.where(bmask, jnp.concatenate([x] * (GROUP // c), axis=0), 0.0).astype(BF16)

    lw = lw_ref[:, sl]
    a_sig = as_ref[:, sl]
    k = k_ref[:, sl]
    r = r_ref[:, sl]
    v = v_ref[:, sl]

    kk = k * kks
    kk = kk * lax.rsqrt(jnp.maximum(_split_dot(kk * kk, ones_bd), L2_EPS))
    b_vec = kk * a_sig
    k_d = k * (1.0 + (a_sig - 1.0) * ka)

    cum = _split_dot_left(tri[d], lw)
    tot = cum[c - 1:c, :] if d == 0 else cum[0:1, :]
    a_t = -kk * jnp.exp(cum - lw)
    r_t = r * jnp.exp(cum)
    p_inv = jnp.exp(-cum)
    p_end = jnp.exp(tot - cum)
    b_t = b_vec * p_inv
    k_t = k_d * p_inv

    lhs = jnp.concatenate([a_t, r_t], axis=0).astype(BF16)
    rhs = jnp.concatenate([stack(b_t), stack(k_t)], axis=0)
    aa = _dot_nt(lhs, rhs)
    n1 = jnp.where(m_strict[d], aa[:c, :GROUP], 0.0)
    a_ak = jnp.where(m_strict[d], aa[:c, GROUP:], 0.0)
    a_rb = jnp.where(m_incl[d], aa[c:, :GROUP], 0.0)
    a_rk = jnp.where(m_incl[d], aa[c:, GROUP:], 0.0)

    q = n1
    npow = _dot(n1.astype(BF16), stack(n1))
    span = 2
    while 2 * span < c:
        z = _dot(jnp.concatenate([npow, q], axis=0).astype(BF16), stack(npow))
        q = q + npow + z[c:]
        npow = z[:c]
        span *= 2
    q = q + npow + _dot(q.astype(BF16), stack(npow))
    q16 = q.astype(BF16)

    v_st = stack(v)
    z1 = _dot(jnp.concatenate([a_ak, a_rk], axis=0).astype(BF16), v_st)
    v_ak = z1[:c]
    y0 = z1[c:]
    z2 = _dot(q16, jnp.concatenate([stack(a_t), stack(v_ak)], axis=1))
    w = a_t + z2[:, :GROUP]
    u0 = v_ak + z2[:, GROUP:]

    s = state_ref[d, g]
    z3 = _dot_nt(jnp.concatenate([w, r_t], axis=0).astype(BF16), s.astype(BF16))
    u = z3[:c] + u0
    y = z3[c:] + _dot(a_rb.astype(BF16), stack(u)) + y0
    upd = _dot_tn(jnp.concatenate([u, v], axis=0).astype(BF16),
                  jnp.concatenate([b_vec * p_end, k_d * p_end], axis=0).astype(BF16))
    state_ref[d, g] = s * jnp.exp(tot) + jnp.where(bmask, upd, 0.0)
    y_ref[:, sl] = y


def _split_dot_left(w_bf16, x):
    hi = x.astype(BF16)
    lo = (x - hi.astype(F32)).astype(BF16)
    return _dot(w_bf16, hi) + _dot(w_bf16, lo)


def _scan_kernel(r0, k0, v0, lw0, as0, r1, k1, v1, lw1, as1, kks_ref, ka_ref, sin_ref,
                 y0_ref, y1_ref, sout_ref, state_ref, *, groups):
    i = pl.program_id(1)
    c = CHUNK

    @pl.when(i == 0)
    def _():
        state_ref[...] = sin_ref[...]

    bmask = _block_mask(GROUP)
    ones_bd = jnp.where(bmask, 1.0, 0.0).astype(BF16)
    tt = lax.broadcasted_iota(jnp.int32, (c, c), 0)
    ss = lax.broadcasted_iota(jnp.int32, (c, c), 1)
    tri = (jnp.where(ss <= tt, 1.0, 0.0).astype(BF16), jnp.where(ss >= tt, 1.0, 0.0).astype(BF16))
    tp = lax.broadcasted_iota(jnp.int32, (c, GROUP), 0)
    sp = lax.broadcasted_iota(jnp.int32, (c, GROUP), 1) % c
    m_strict = (sp < tp, sp > tp)
    m_incl = (sp <= tp, sp >= tp)
    consts = (bmask, ones_bd, tri, m_strict, m_incl)

    def body(g, carry):
        sl = pl.ds(pl.multiple_of(g * GROUP, GROUP), GROUP)
        kks = kks_ref[:, sl]
        ka = ka_ref[:, sl]
        _scan_group(0, (r0, k0, v0, lw0, as0), kks, ka, state_ref, y0_ref, sl, g, consts)
        _scan_group(1, (r1, k1, v1, lw1, as1), kks, ka, state_ref, y1_ref, sl, g, consts)
        return carry

    lax.fori_loop(0, groups, body, 0)

    @pl.when(i == pl.num_programs(1) - 1)
    def _():
        sout_ref[...] = state_ref[...]


def _rwkv_scan(r, k, v, lw0, as0, lw1, as1, kk_scale, ka, s_init):
    b, t, d = r.shape
    assert CHUNK * (GROUP // HEAD) == GROUP and t % CHUNK == 0 and d % GROUP == 0
    groups = d // GROUP
    nc = t // CHUNK
    fwd = pl.BlockSpec((None, CHUNK, d), lambda i, j: (i, j, 0))
    bwd = pl.BlockSpec((None, CHUNK, d), lambda i, j: (i, nc - 1 - j, 0))
    vec = pl.BlockSpec((1, d), lambda i, j: (0, 0))
    sspec = pl.BlockSpec((None, 2, groups, GROUP, GROUP), lambda i, j: (i, 0, 0, 0, 0))
    y = jax.ShapeDtypeStruct((b, t, d), F32)
    state_bytes = 2 * groups * GROUP * GROUP * 4
    vmem = 2 * 12 * CHUNK * d * 4 + 5 * state_bytes + (8 << 20)
    return pl.pallas_call(
        functools.partial(_scan_kernel, groups=groups),
        grid=(b, nc),
        in_specs=[fwd] * 5 + [bwd] * 5 + [vec, vec, sspec],
        out_specs=[fwd, bwd, sspec],
        out_shape=[y, y, jax.ShapeDtypeStruct(s_init.shape, F32)],
        scratch_shapes=[pltpu.VMEM((2, groups, GROUP, GROUP), F32)],
        compiler_params=_cparams(("parallel", "arbitrary"), vmem),
        name="rwkv_scan",
    )(r, k, v, lw0, as0, r, k, v, lw1, as1, kk_scale.reshape(1, d), ka.reshape(1, d), s_init)


def _readout_kernel(y0_ref, y1_ref, r_ref, k_ref, v_ref, as0_ref, as1_ref, g_ref,
                    ka_ref, rk_ref, lnw_ref, lnb_ref, o_ref):
    d = o_ref.shape[-1]
    ones_bd = jnp.where(_block_mask(GROUP), 1.0, 0.0).astype(BF16)
    for j in range(d // GROUP):
        sl = slice(j * GROUP, (j + 1) * GROUP)
        y = y0_ref[:, sl] + y1_ref[:, sl]
        mean = _split_dot(y, ones_bd) * (1.0 / HEAD)
        yc = y - mean
        var = _split_dot(yc * yc, ones_bd) * (1.0 / HEAD)
        yn = (yc * lax.rsqrt(var + GN_EPS)) * lnw_ref[:, sl] + lnb_ref[:, sl]
        k = k_ref[:, sl]
        ka = ka_ref[:, sl]
        k_sum = k * (1.0 + (as0_ref[:, sl] - 1.0) * ka) + k * (1.0 + (as1_ref[:, sl] - 1.0) * ka)
        bonus = _split_dot(r_ref[:, sl] * k_sum * rk_ref[:, sl], ones_bd) * v_ref[:, sl]
        o_ref[:, sl] = ((yn + bonus) * g_ref[:, sl]).astype(o_ref.dtype)


def _rwkv_readout(y0, y1, r, k, v, as0, as1, g, ka, rk, lnw, lnb):
    b, t, d = r.shape
    tm = min(t, 256)
    spec = pl.BlockSpec((None, tm, d), lambda i, j: (i, j, 0))
    vec = pl.BlockSpec((1, d), lambda i, j: (0, 0))
    return pl.pallas_call(
        _readout_kernel,
        grid=(b, t // tm),
        in_specs=[spec] * 8 + [vec] * 4,
        out_specs=spec,
        out_shape=jax.ShapeDtypeStruct((b, t, d), BF16),
        compiler_params=_cparams(("parallel", "parallel"), 2 * 9 * tm * d * 4 + (8 << 20)),
        name="rwkv_readout",
    )(y0, y1, r, k, v, as0, as1, g, ka.reshape(1, d), rk.reshape(1, d), lnw.reshape(1, d), lnb.reshape(1, d))


def _mm_kernel(z_ref, w_ref, o_ref):
    o_ref[...] = _dot(z_ref[...], w_ref[...]).astype(o_ref.dtype)


def _mm_res_kernel(z_ref, w_ref, res_ref, gate_ref, o_ref):
    o_ref[...] = res_ref[...] + gate_ref[...] * _dot(z_ref[...], w_ref[...])


def _matmul(z, w, res=None, gate=None, out_dtype=F32):
    b, t, kdim = z.shape
    n = w.shape[1]
    tm = min(t, 1024)
    tn = 512
    zspec = pl.BlockSpec((None, tm, kdim), lambda i, j, l: (i, j, 0))
    wspec = pl.BlockSpec((kdim, tn), lambda i, j, l: (0, l))
    ospec = pl.BlockSpec((None, tm, tn), lambda i, j, l: (i, j, l))
    vmem = 2 * (tm * kdim * 2 + kdim * tn * 2 + 2 * tm * tn * 4) + tm * tn * 4 + (4 << 20)
    common = dict(grid=(b, t // tm, n // tn), out_specs=ospec,
                  compiler_params=_cparams(("parallel", "parallel", "parallel"), vmem))
    if res is None:
        return pl.pallas_call(_mm_kernel, in_specs=[zspec, wspec],
                              out_shape=jax.ShapeDtypeStruct((b, t, n), out_dtype), name="matmul", **common)(z, w)
    gspec = (pl.BlockSpec((None, 1, tn), lambda i, j, l: (0, 0, l)) if gate.shape[0] == 1
             else pl.BlockSpec((None, 1, tn), lambda i, j, l: (i, 0, l)))
    return pl.pallas_call(_mm_res_kernel, in_specs=[zspec, wspec, ospec, gspec],
                          out_shape=jax.ShapeDtypeStruct((b, t, n), F32), name="matmul_res", **common)(z, w, res, gate)


def _ffn_kernel(x_ref, g_ref, sh_ref, sc_ref, gate_ref, wg_ref, wu_ref, wd_ref, o_ref, h_ref):
    f = pl.program_id(2)

    @pl.when(f == 0)
    def _():
        h_ref[...] = _norm_mod_value(x_ref[...], g_ref[...], sh_ref[...], sc_ref[...]).astype(BF16)

    h = h_ref[...]
    gp = _dot(h, wg_ref[...])
    act = ((gp * jax.nn.sigmoid(gp)) * _dot(h, wu_ref[...])).astype(BF16)
    part = _dot(act, wd_ref[...])

    @pl.when(f == 0)
    def _():
        o_ref[...] = part

    @pl.when(f > 0)
    def _():
        o_ref[...] += part

    @pl.when(f == pl.num_programs(2) - 1)
    def _():
        o_ref[...] = x_ref[...] + gate_ref[...] * o_ref[...]


def _ffn(x, g, sh, sc, gate, wg, wu, wd):
    b, t, d = x.shape
    ff = wg.shape[1]
    tm = min(t, 512)
    tf = 512
    xspec = pl.BlockSpec((None, tm, d), lambda i, j, f: (i, j, 0))
    vmem = 4 * tm * d * 4 + tm * d * 2 + 2 * 3 * d * tf * 2 + 6 * tm * tf * 4 + tm * d * 4 + (4 << 20)
    return pl.pallas_call(
        _ffn_kernel,
        grid=(b, t // tm, ff // tf),
        in_specs=[xspec, pl.BlockSpec((1, d), lambda i, j, f: (0, 0)),
                  _row_spec(sh, d), _row_spec(sc, d), _row_spec(gate, d),
                  pl.BlockSpec((d, tf), lambda i, j, f: (0, f)),
                  pl.BlockSpec((d, tf), lambda i, j, f: (0, f)),
                  pl.BlockSpec((tf, d), lambda i, j, f: (f, 0))],
        out_specs=xspec,
        out_shape=jax.ShapeDtypeStruct((b, t, d), F32),
        scratch_shapes=[pltpu.VMEM((tm, d), BF16)],
        compiler_params=_cparams(("parallel", "parallel", "arbitrary"), vmem),
        name="ffn",
    )(x, g.reshape(1, d), sh, sc, gate, wg, wu, wd)


def _qk_norm(x, gain, ones_bd):
    ms = _split_dot(x * x, ones_bd) * (1.0 / HEAD)
    return x * lax.rsqrt(ms + QK_EPS) * gain


def _rope(u, cos, sin):
    n = u.shape[-1]
    lane = lax.broadcasted_iota(jnp.int32, u.shape, 1)
    partner = jnp.where(lane % 32 < 16, pltpu.roll(u, n - 16, 1), pltpu.roll(u, 16, 1))
    return u * cos + partner * sin


def _qkv_prep_kernel(*refs, rope, with_q):
    refs = list(refs)
    q_ref = refs.pop(0) if with_q else None
    k_ref, v_ref = refs.pop(0), refs.pop(0)
    qn_ref = refs.pop(0) if with_q else None
    kn_ref = refs.pop(0)
    if rope:
        cos_ref, sin_ref = refs.pop(0), refs.pop(0)
    qo_ref = refs.pop(0) if with_q else None
    ko_ref, vo_ref = refs
    d = ko_ref.shape[-1]
    ones_bd = jnp.where(_block_mask(GROUP), 1.0, 0.0).astype(BF16)
    if rope:
        cos, sin = cos_ref[...], sin_ref[...]
    for j in range(d // GROUP):
        sl = slice(j * GROUP, (j + 1) * GROUP)
        kh = _qk_norm(k_ref[:, sl], kn_ref[:, sl], ones_bd)
        if rope:
            kh = _rope(kh, cos, sin)
        ko_ref[:, sl] = kh.astype(BF16)
        if with_q:
            qh = _qk_norm(q_ref[:, sl], qn_ref[:, sl], ones_bd)
            if rope:
                qh = _rope(qh, cos, sin)
            qo_ref[:, sl] = (qh * (HEAD ** -0.5)).astype(BF16)
    vo_ref[...] = v_ref[...].astype(BF16)


def _qkv_prep(qkv, qn, kn, cos, sin, *, with_q):
    b, t, width = qkv.shape
    parts = 3 if with_q else 2
    d = width // parts
    rope = cos is not None
    tm = min(t, 256)
    col = lambda c: pl.BlockSpec((None, tm, d), lambda i, j, _c=c: (i, j, _c))
    vec = pl.BlockSpec((1, d), lambda i, j: (0, 0))
    tab = pl.BlockSpec((tm, GROUP), lambda i, j: (j, 0))
    in_specs = [col(c) for c in range(parts)] + [vec] * (2 if with_q else 1) + ([tab, tab] if rope else [])
    args = [qkv] * parts + ([qn] if with_q else []) + [kn] + ([cos, sin] if rope else [])
    out = jax.ShapeDtypeStruct((b, t, d), BF16)
    ospec = pl.BlockSpec((None, tm, d), lambda i, j: (i, j, 0))
    return pl.pallas_call(
        functools.partial(_qkv_prep_kernel, rope=rope, with_q=with_q),
        grid=(b, t // tm),
        in_specs=in_specs,
        out_specs=[ospec] * parts,
        out_shape=[out] * parts,
        compiler_params=_cparams(("parallel", "parallel"), 2 * parts * tm * d * 6 + (16 << 20)),
        name="qkv_prep",
    )(*args)


def _attn_kernel(lam_ref, q_ref, k_ref, v_ref, sub_ref, o_ref, *, tk, lambda_init):
    tq = q_ref.shape[0]
    tkeys = k_ref.shape[0]
    first = tkeys % tk or tk
    lp = lam_ref[...]
    lam = (jnp.exp(jnp.sum(lp[0:1] * lp[1:2], axis=-1, keepdims=True))
           - jnp.exp(jnp.sum(lp[2:3] * lp[3:4], axis=-1, keepdims=True)) + lambda_init)
    q = q_ref[...]
    lane = lax.broadcasted_iota(jnp.int32, q.shape, 1)
    zero = jnp.zeros_like(q)
    q2 = jnp.concatenate([jnp.where(lane < HEAD, q, zero), jnp.where(lane >= HEAD, q, zero)], axis=0)

    def step(ks, carry):
        m, l, acc = carry
        s = _dot_nt(q2, k_ref[ks, :])
        m_new = jnp.maximum(m, jnp.max(s, axis=-1, keepdims=True))
        alpha = jnp.exp(m - m_new)
        p = jnp.exp(s - m_new)
        l = alpha * l + jnp.sum(p, axis=-1, keepdims=True)
        acc = alpha * acc + _dot(p.astype(BF16), v_ref[ks, :])
        return m_new, l, acc

    m0 = jnp.full((2 * tq, 1), -jnp.inf, F32)
    l0 = jnp.zeros((2 * tq, 1), F32)
    acc0 = jnp.zeros((2 * tq, 2 * HEAD), F32)
    carry = step(slice(0, first), (m0, l0, acc0))
    m, l, acc = lax.fori_loop(
        0, (tkeys - first) // tk,
        lambda j, cr: step(pl.ds(pl.multiple_of(first + j * tk, first), tk), cr), carry)
    o = acc[:tq] / l[:tq] - lam * (acc[tq:] / l[tq:])
    o = o * lax.rsqrt(jnp.mean(o * o, axis=-1, keepdims=True) + SUBLN_EPS) * sub_ref[...]
    o_ref[...] = (o * (1.0 - lambda_init)).astype(o_ref.dtype)


def _diff_attention(q, k_all, v_all, lam_params, sub_gain, lambda_init):
    b, t, d = q.shape
    tkeys = k_all.shape[1]
    hw = 2 * HEAD
    tq = 256
    tk = 512
    assert t % tq == 0 and (tkeys % tk) % 128 == 0
    qspec = pl.BlockSpec((None, tq, hw), lambda i, h, j: (i, j, h))
    kspec = pl.BlockSpec((None, tkeys, hw), lambda i, h, j: (i, 0, h))
    vmem = 4 * tkeys * hw * 2 + 4 * tq * hw * 2 + 12 * 2 * tq * tk * 4 + (8 << 20)
    return pl.pallas_call(
        functools.partial(_attn_kernel, tk=tk, lambda_init=lambda_init),
        grid=(b, d // hw, t // tq),
        in_specs=[pl.BlockSpec((4, HEAD), lambda i, h, j: (0, 0)), qspec, kspec, kspec,
                  pl.BlockSpec((1, hw), lambda i, h, j: (0, 0))],
        out_specs=qspec,
        out_shape=jax.ShapeDtypeStruct((b, t, d), BF16),
        compiler_params=_cparams(("parallel", "parallel", "parallel"), vmem),
        name="diff_attention",
    )(lam_params, q, k_all, v_all, sub_gain.reshape(1, hw))


def _rope_tables(n_tokens):
    n_freq = HEAD // 4
    pos = jnp.arange(n_tokens)
    freqs = ROPE_BASE ** (-jnp.arange(n_freq, dtype=F32) / n_freq)
    ang_row = (pos // GRID_W).astype(F32)[:, None] * freqs
    ang_col = (pos % GRID_W).astype(F32)[:, None] * freqs
    cos = jnp.concatenate([jnp.cos(ang_row)] * 2 + [jnp.cos(ang_col)] * 2, axis=-1)
    sin = jnp.concatenate([-jnp.sin(ang_row), jnp.sin(ang_row), -jnp.sin(ang_col), jnp.sin(ang_col)], axis=-1)
    reps = GROUP // HEAD
    return jnp.tile(cos, (1, reps)), jnp.tile(sin, (1, reps))


def _mod_rows(mod, layer, j, d, batch):
    m = mod[layer, :, j * d:(j + 1) * d]
    return m[:batch, None, :], m[batch:batch + 1, None, :]


def _rwkv_layer(xs, mods, norm_g, p):
    rkv, lora, outs = [], [], []
    for x, (sh, sc, _) in zip(xs, mods):
        h = _norm_mod(x, norm_g, sh, sc, F32)
        rkv.append(_rwkv_rkv(h, p["mu"], p["wr"], p["wk"], p["wv"]))
        lora.append(_rwkv_lora(h, p["mu"], p["w0"], p["w1"], p["w2"], p["a0"], p["a1"], p["a2"], p["g1"], p["g2"]))
    b, _, d = xs[0].shape
    state = jnp.zeros((b, 2, d // GROUP, GROUP, GROUP), F32)
    for x, (_, _, gate), (r, k, v), (g, lw0, lw1, as0, as1) in zip(xs, mods, rkv, lora):
        y0, y1, state = _rwkv_scan(r, k, v, lw0, as0, lw1, as1, p["kk"], p["ka"], state)
        z = _rwkv_readout(y0, y1, r, k, v, as0, as1, g, p["ka"], p["rk"], p["lnw"], p["lnb"])
        outs.append(_matmul(z, p["wo"], res=x, gate=gate))
    return outs


def kernel(x, c, ctx, c_ctx, mod_w, mod_b, norm1_g, norm2_g, rwkv_mu, rwkv_wr, rwkv_wk, rwkv_wv, rwkv_wo, rwkv_w0, rwkv_w1, rwkv_w2, rwkv_a0, rwkv_a1, rwkv_a2, rwkv_g1, rwkv_g2, rwkv_kk, rwkv_ka, rwkv_rk, rwkv_lnw, rwkv_lnb, diff_wqkv, diff_qn, diff_kn, diff_lq1, diff_lk1, diff_lq2, diff_lk2, diff_subln, diff_wo, ffn_wg, ffn_wu, ffn_wd):
    batch, n_lat, d = x.shape
    depth = mod_w.shape[0]
    assert depth == 2, "layer 0 is the RWKV-7 mixer, layer 1 differential attention"

    cvecs = jnp.zeros((SUBLANES, d), F32).at[:batch].set(c).at[batch].set(c_ctx)
    mod = _adaln(cvecs, mod_w, mod_b)

    def mods(layer, first):
        lat, cx = zip(*[_mod_rows(mod, layer, first + j, d, batch) for j in range(3)])
        return cx, lat

    w1, w2 = _pad_lora(rwkv_w1[0], rwkv_w2[0])
    a1, a2 = _pad_lora(rwkv_a1[0], rwkv_a2[0])
    g1, g2 = _pad_lora(rwkv_g1[0], rwkv_g2[0])
    p = dict(mu=rwkv_mu[0], wr=rwkv_wr[0].astype(BF16), wk=rwkv_wk[0].astype(BF16), wv=rwkv_wv[0].astype(BF16),
             wo=rwkv_wo[0].astype(BF16), w0=rwkv_w0[0], w1=w1, w2=w2, a0=rwkv_a0[0], a1=a1, a2=a2,
             g1=g1, g2=g2, kk=rwkv_kk[0], ka=rwkv_ka[0],
             rk=rwkv_rk[0], lnw=rwkv_lnw[0], lnb=rwkv_lnb[0])
    xc, xl = _rwkv_layer((ctx, x), mods(0, 0), norm1_g[0], p)
    wg, wu, wd = ffn_wg[0].astype(BF16), ffn_wu[0].astype(BF16), ffn_wd[0].astype(BF16)
    (csh, csc, cgt), (lsh, lsc, lgt) = mods(0, 3)
    xc = _ffn(xc, norm2_g[0], csh, csc, cgt, wg, wu, wd)
    xl = _ffn(xl, norm2_g[0], lsh, lsc, lgt, wg, wu, wd)

    (csh, csc, _), (lsh, lsc, lgt) = mods(1, 0)
    lambda_init = 0.8 - 0.6 * math.exp(-0.3 * 1)
    wqkv = diff_wqkv[0].astype(BF16)
    hl = _norm_mod(xl, norm1_g[1], lsh, lsc, BF16)
    hc = _norm_mod(xc, norm1_g[1], csh, csc, BF16)
    qn = jnp.tile(diff_qn[0], d // HEAD).reshape(1, d)
    kn = jnp.tile(diff_kn[0], d // HEAD).reshape(1, d)
    cos, sin = _rope_tables(n_lat)
    q_l, k_l, v_l = _qkv_prep(_matmul(hl, wqkv), qn, kn, cos, sin, with_q=True)
    k_c, v_c = _qkv_prep(_matmul(hc, wqkv[:, d:]), None, kn, None, None, with_q=False)
    k_all = jnp.concatenate([k_c, k_l], axis=1)
    v_all = jnp.concatenate([v_c, v_l], axis=1)
    lam_params = jnp.stack([diff_lq1[0], diff_lk1[0], diff_lq2[0], diff_lk2[0]])
    o = _diff_attention(q_l, k_all, v_all, lam_params, diff_subln[0], lambda_init)
    xl = _matmul(o, diff_wo[0].astype(BF16), res=xl, gate=lgt)
    (_, _, _), (lsh, lsc, lgt) = mods(1, 3)
    return _ffn(xl, norm2_g[1], lsh, lsc, lgt,
                ffn_wg[1].astype(BF16), ffn_wu[1].astype(BF16), ffn_wd[1].astype(BF16))
```

```python
import functools
import math

import jax
import jax.numpy as jnp
from jax import lax
from jax.experimental import pallas as pl
from jax.experimental.pallas import tpu as pltpu

F32 = jnp.float32
BF16 = jnp.bfloat16

HEAD = 64
GROUP = 256
CHUNK = 64
SCAN_UNROLL = 8
SUBLANES = 8
V7X_VMEM_CAP = 56 * 1024 * 1024

NORM_EPS = 1e-6
GN_EPS = 64e-5
L2_EPS = 1e-24
QK_EPS = 1e-6
SUBLN_EPS = 1e-5
ROPE_BASE = 10000.0
GRID_W = 64
DECAY_SCALE = math.exp(-0.5)


def _cparams(semantics, vmem_bytes):
    return pltpu.CompilerParams(dimension_semantics=semantics,
                                vmem_limit_bytes=int(min(max(vmem_bytes, 16 << 20), V7X_VMEM_CAP)))


def _dot(a, b):
    return jnp.dot(a, b, preferred_element_type=F32)


def _dot_nt(a, b):
    return lax.dot_general(a, b, (((1,), (1,)), ((), ())), preferred_element_type=F32)


def _dot_tn(a, b):
    return lax.dot_general(a, b, (((0,), (0,)), ((), ())), preferred_element_type=F32)


def _split_dot(x, w_bf16):
    hi = x.astype(BF16)
    lo = (x - hi.astype(F32)).astype(BF16)
    return _dot(hi, w_bf16) + _dot(lo, w_bf16)


def _block_mask(n):
    r = lax.broadcasted_iota(jnp.int32, (n, n), 0) // HEAD
    c = lax.broadcasted_iota(jnp.int32, (n, n), 1) // HEAD
    return r == c


def _adaln_kernel(c_ref, w_ref, b_ref, o_ref):
    c = c_ref[...]
    s = (c * jax.nn.sigmoid(c)).astype(BF16)
    o_ref[...] = _dot(s, w_ref[...].astype(BF16)) + b_ref[...]


def _adaln(cvecs, mod_w, mod_b):
    depth, d, n = mod_w.shape
    rows = cvecs.shape[0]
    tn = 1024
    return pl.pallas_call(
        _adaln_kernel,
        grid=(depth, n // tn),
        in_specs=[pl.BlockSpec((rows, d), lambda l, j: (0, 0)),
                  pl.BlockSpec((None, d, tn), lambda l, j: (l, 0, j)),
                  pl.BlockSpec((None, 1, tn), lambda l, j: (l, 0, j))],
        out_specs=pl.BlockSpec((None, rows, tn), lambda l, j: (l, 0, j)),
        out_shape=jax.ShapeDtypeStruct((depth, rows, n), F32),
        compiler_params=_cparams(("parallel", "parallel"), 3 * d * tn * 4 + (8 << 20)),
        name="adaln",
    )(cvecs, mod_w, mod_b.reshape(depth, 1, n))


def _norm_mod_value(x, g, sh, sc):
    y = x * lax.rsqrt(jnp.mean(x * x, axis=-1, keepdims=True) + NORM_EPS)
    return (y * g) * (1.0 + sc) + sh


def _norm_mod_kernel(x_ref, g_ref, sh_ref, sc_ref, o_ref):
    o_ref[...] = _norm_mod_value(x_ref[...], g_ref[...], sh_ref[...], sc_ref[...]).astype(o_ref.dtype)


def _row_spec(arr, d):
    if arr.shape[0] == 1:
        return pl.BlockSpec((None, 1, d), lambda b, *_: (0, 0, 0))
    return pl.BlockSpec((None, 1, d), lambda b, *_: (b, 0, 0))


def _norm_mod(x, g, sh, sc, out_dtype):
    b, t, d = x.shape
    tr = min(t, 512)
    return pl.pallas_call(
        _norm_mod_kernel,
        grid=(b, t // tr),
        in_specs=[pl.BlockSpec((None, tr, d), lambda i, j: (i, j, 0)),
                  pl.BlockSpec((1, d), lambda i, j: (0, 0)),
                  _row_spec(sh, d), _row_spec(sc, d)],
        out_specs=pl.BlockSpec((None, tr, d), lambda i, j: (i, j, 0)),
        out_shape=jax.ShapeDtypeStruct((b, t, d), out_dtype),
        compiler_params=_cparams(("parallel", "parallel"), 6 * tr * d * 4),
        name="norm_mod",
    )(x, g.reshape(1, d), sh, sc)


def _shift_delta(h, prev_ref, next_ref, t, nt):
    tm = h.shape[0]
    row = lax.broadcasted_iota(jnp.int32, h.shape, 0)
    prev_row = jnp.where(t > 0, prev_ref[SUBLANES - 1:SUBLANES, :], 0.0)
    next_row = jnp.where(t < nt - 1, next_ref[0:1, :], 0.0)
    hp = jnp.where(row == 0, prev_row, pltpu.roll(h, 1, 0))
    hn = jnp.where(row == tm - 1, next_row, pltpu.roll(h, tm - 1, 0))
    return 0.5 * (hp + hn) - h


def _halo_specs(t, tm, d):
    per = tm // SUBLANES
    last = t // SUBLANES - 1
    return [pl.BlockSpec((None, tm, d), lambda b, i, *_: (b, i, 0)),
            pl.BlockSpec((None, SUBLANES, d), lambda b, i, *_: (b, jnp.maximum(i * per - 1, 0), 0)),
            pl.BlockSpec((None, SUBLANES, d), lambda b, i, *_: (b, jnp.minimum((i + 1) * per, last), 0))]


def _rkv_kernel(h_ref, hp_ref, hn_ref, mu_ref, wr_ref, wk_ref, wv_ref, r_ref, k_ref, v_ref, xs_ref, *, nt):
    t = pl.program_id(1)

    @pl.when(pl.program_id(2) == 0)
    def _():
        h = h_ref[...]
        xx = _shift_delta(h, hp_ref, hn_ref, t, nt)
        xs_ref[0] = (h + xx * mu_ref[0:1, :]).astype(BF16)
        xs_ref[1] = (h + xx * mu_ref[2:3, :]).astype(BF16)
        xs_ref[2] = (h + xx * mu_ref[3:4, :]).astype(BF16)

    r_ref[...] = _dot(xs_ref[0], wr_ref[...])
    k_ref[...] = _dot(xs_ref[1], wk_ref[...])
    v_ref[...] = _dot(xs_ref[2], wv_ref[...])


def _rwkv_rkv(h, mu, wr, wk, wv):
    b, t, d = h.shape
    tm = min(t, 512)
    tn = 512
    nt = t // tm
    out = jax.ShapeDtypeStruct((b, t, d), F32)
    wspec = pl.BlockSpec((d, tn), lambda i, j, n: (0, n))
    ospec = pl.BlockSpec((None, tm, tn), lambda i, j, n: (i, j, n))
    vmem = 2 * tm * d * 4 + 3 * tm * d * 2 + 6 * d * tn * 2 + 6 * tm * tn * 4 + 3 * tm * d * 4 + (4 << 20)
    return pl.pallas_call(
        functools.partial(_rkv_kernel, nt=nt),
        grid=(b, nt, d // tn),
        in_specs=_halo_specs(t, tm, d) + [pl.BlockSpec((6, d), lambda i, j, n: (0, 0)), wspec, wspec, wspec],
        out_specs=[ospec, ospec, ospec],
        out_shape=[out, out, out],
        scratch_shapes=[pltpu.VMEM((3, tm, d), BF16)],
        compiler_params=_cparams(("parallel", "parallel", "arbitrary"), vmem),
        name="rwkv_rkv",
    )(h, h, h, mu, wr, wk, wv)


def _lora_kernel(h_ref, hp_ref, hn_ref, mu_ref, w0_ref, w1_ref, w2_ref, a0_ref, a1_ref, a2_ref, g1_ref, g2_ref,
                 g_ref, lw0_ref, lw1_ref, as0_ref, as1_ref, *, nt):
    t = pl.program_id(1)
    h = h_ref[...]
    xx = _shift_delta(h, hp_ref, hn_ref, t, nt)
    xw = (h + xx * mu_ref[1:2, :]).astype(BF16)
    xa = (h + xx * mu_ref[4:5, :]).astype(BF16)
    xg = (h + xx * mu_ref[5:6, :]).astype(BF16)
    g_ref[...] = _dot(jax.nn.sigmoid(_dot(xg, g1_ref[...])).astype(BF16), g2_ref[...])
    for d, (lw_ref, as_ref) in enumerate(((lw0_ref, as0_ref), (lw1_ref, as1_ref))):
        w_pre = w0_ref[d:d + 1, :] + _dot(jnp.tanh(_dot(xw, w1_ref[d])).astype(BF16), w2_ref[d])
        lw_ref[...] = -DECAY_SCALE * jax.nn.sigmoid(w_pre)
        a_pre = a0_ref[d:d + 1, :] + _dot(_dot(xa, a1_ref[d]).astype(BF16), a2_ref[d])
        as_ref[...] = jax.nn.sigmoid(a_pre)


def _pad_lora(w_in, w_out):
    rank = w_in.shape[-1]
    pad = (-rank) % 128
    w_in = jnp.pad(w_in, [(0, 0)] * (w_in.ndim - 1) + [(0, pad)])
    w_out = jnp.pad(w_out, [(0, 0)] * (w_out.ndim - 2) + [(0, pad), (0, 0)])
    return w_in.astype(BF16), w_out.astype(BF16)


def _rwkv_lora(h, mu, w0, w1, w2, a0, a1, a2, g1, g2):
    b, t, d = h.shape
    tm = min(t, 256)
    nt = t // tm
    out = jax.ShapeDtypeStruct((b, t, d), F32)
    ospec = pl.BlockSpec((None, tm, d), lambda i, j: (i, j, 0))

    def full(a):
        return pl.BlockSpec(a.shape, lambda i, j, _n=a.ndim: (0,) * _n)

    params = (mu, w0, w1, w2, a0, a1, a2, g1, g2)
    vmem = 2 * 6 * tm * d * 4 + 10 * tm * d * 4 + 2 * sum(p.size * p.dtype.itemsize for p in params)
    return pl.pallas_call(
        functools.partial(_lora_kernel, nt=nt),
        grid=(b, nt),
        in_specs=_halo_specs(t, tm, d) + [full(p) for p in params],
        out_specs=[ospec] * 5,
        out_shape=[out] * 5,
        compiler_params=_cparams(("parallel", "parallel"), vmem),
        name="rwkv_lora",
    )(h, h, h, *params)


def _split_dot_left(w_bf16, x):
    hi = x.astype(BF16)
    lo = (x - hi.astype(F32)).astype(BF16)
    return _dot(w_bf16, hi) + _dot(w_bf16, lo)


def _scan_chain(d, refs, kks, ka, state_ref, y_ref, sl, g, consts):
    r_ref, k_ref, v_ref, lw_ref, as_ref = refs
    bmask, ones_bd, tri, m_strict, m_incl = consts
    c = CHUNK

    def stack(x):
        return jnp.where(bmask, jnp.concatenate([x] * (GROUP // c), axis=0), 0.0).astype(BF16)

    lw = lw_ref[:, sl]
    a_sig = as_ref[:, sl]
    k = k_ref[:, sl]
    kk = k * kks
    ssq = _split_dot(kk * kk, ones_bd)
    cum = _split_dot_left(tri[d], lw)
    yield

    kk = kk * lax.rsqrt(jnp.maximum(ssq, L2_EPS))
    b_vec = kk * a_sig
    k_d = k * (1.0 + (a_sig - 1.0) * ka)
    tot = cum[c - 1:c, :] if d == 0 else cum[0:1, :]
    a_t = -kk * jnp.exp(cum - lw)
    r_t = r_ref[:, sl] * jnp.exp(cum)
    p_inv = jnp.exp(-cum)
    lhs = jnp.concatenate([a_t, r_t], axis=0).astype(BF16)
    rhs = jnp.concatenate([stack(b_vec * p_inv), stack(k_d * p_inv)], axis=0)
    aa = _dot_nt(lhs, rhs)
    yield

    n1 = jnp.where(m_strict[d], aa[:c, :GROUP], 0.0)
    a_ak = jnp.where(m_strict[d], aa[:c, GROUP:], 0.0)
    a_rb = jnp.where(m_incl[d], aa[c:, :GROUP], 0.0).astype(BF16)
    a_rk = jnp.where(m_incl[d], aa[c:, GROUP:], 0.0)
    v = v_ref[:, sl]
    z1 = _dot(jnp.concatenate([a_ak, a_rk], axis=0).astype(BF16), stack(v))
    q = n1
    npow = _dot(n1.astype(BF16), stack(n1))
    yield
    span = 2
    while 2 * span < c:
        z = _dot(jnp.concatenate([npow, q], axis=0).astype(BF16), stack(npow))
        yield
        q = q + npow + z[c:]
        npow = z[:c]
        span *= 2
    zq = _dot(q.astype(BF16), stack(npow))
    yield
    q = q + npow + zq

    v_ak = z1[:c]
    y0 = z1[c:]
    z2 = _dot(q.astype(BF16), jnp.concatenate([stack(a_t), stack(v_ak)], axis=1))
    yield
    w = a_t + z2[:, :GROUP]
    u0 = v_ak + z2[:, GROUP:]

    s = state_ref[d, g]
    z3 = _dot_nt(jnp.concatenate([w, r_t], axis=0).astype(BF16), s.astype(BF16))
    yield
    u = z3[:c] + u0
    p_end = jnp.exp(tot - cum)
    yu = _dot(a_rb, stack(u))
    upd = _dot_tn(jnp.concatenate([u, v], axis=0).astype(BF16),
                  jnp.concatenate([b_vec * p_end, k_d * p_end], axis=0).astype(BF16))
    yield
    y_ref[:, sl] = z3[c:] + yu + y0
    state_ref[d, g] = s * jnp.exp(tot) + jnp.where(bmask, upd, 0.0)


def _interleave(chains):
    chains = list(chains)
    while chains:
        alive = []
        for ch in chains:
            try:
                next(ch)
                alive.append(ch)
            except StopIteration:
                pass
        chains = alive


def _scan_kernel(r0, k0, v0, lw0, as0, r1, k1, v1, lw1, as1, kks_ref, ka_ref, sin_ref,
                 y0_ref, y1_ref, state_ref, *, groups):
    i = pl.program_id(1)
    c = CHUNK
    unroll = math.gcd(groups, SCAN_UNROLL)

    @pl.when(i == 0)
    def _():
        state_ref[...] = sin_ref[...]

    bmask = _block_mask(GROUP)
    ones_bd = jnp.where(bmask, 1.0, 0.0).astype(BF16)
    tt = lax.broadcasted_iota(jnp.int32, (c, c), 0)
    ss = lax.broadcasted_iota(jnp.int32, (c, c), 1)
    tri = (jnp.where(ss <= tt, 1.0, 0.0).astype(BF16), jnp.where(ss >= tt, 1.0, 0.0).astype(BF16))
    tp = lax.broadcasted_iota(jnp.int32, (c, GROUP), 0)
    sp = lax.broadcasted_iota(jnp.int32, (c, GROUP), 1) % c
    m_strict = (sp < tp, sp > tp)
    m_incl = (sp <= tp, sp >= tp)
    consts = (bmask, ones_bd, tri, m_strict, m_incl)

    def body(it, carry):
        chains = []
        for j in range(unroll):
            g = it * unroll + j
            sl = pl.ds(pl.multiple_of(g * GROUP, GROUP), GROUP)
            kks = kks_ref[:, sl]
            ka = ka_ref[:, sl]
            chains.append(_scan_chain(0, (r0, k0, v0, lw0, as0), kks, ka, state_ref, y0_ref, sl, g, consts))
            chains.append(_scan_chain(1, (r1, k1, v1, lw1, as1), kks, ka, state_ref, y1_ref, sl, g, consts))
        _interleave(chains)
        return carry

    lax.fori_loop(0, groups // unroll, body, 0)


def _rwkv_scan(r, k, v, lw0, as0, lw1, as1, kk_scale, ka, s_init):
    b, t, d = r.shape
    assert CHUNK * (GROUP // HEAD) == GROUP and t % CHUNK == 0 and d % GROUP == 0
    groups = d // GROUP
    nc = t // CHUNK
    fwd = pl.BlockSpec((None, CHUNK, d), lambda i, j: (i, j, 0))
    bwd = pl.BlockSpec((None, CHUNK, d), lambda i, j: (i, nc - 1 - j, 0))
    vec = pl.BlockSpec((1, d), lambda i, j: (0, 0))
    sspec = pl.BlockSpec((None, 2, groups, GROUP, GROUP), lambda i, j: (i, 0, 0, 0, 0))
    y = jax.ShapeDtypeStruct((b, t, d), F32)
    state_bytes = 2 * groups * GROUP * GROUP * 4
    vmem = 2 * 12 * CHUNK * d * 4 + 4 * state_bytes + 2 * SCAN_UNROLL * (3 << 19) + (4 << 20)
    return pl.pallas_call(
        functools.partial(_scan_kernel, groups=groups),
        grid=(b, nc),
        in_specs=[fwd] * 5 + [bwd] * 5 + [vec, vec, sspec],
        out_specs=[fwd, bwd, sspec],
        out_shape=[y, y, jax.ShapeDtypeStruct(s_init.shape, F32)],
        compiler_params=_cparams(("parallel", "arbitrary"), vmem),
        name="rwkv_scan",
    )(r, k, v, lw0, as0, r, k, v, lw1, as1, kk_scale.reshape(1, d), ka.reshape(1, d), s_init)


def _readout_kernel(y0_ref, y1_ref, r_ref, k_ref, v_ref, as0_ref, as1_ref, g_ref,
                    ka_ref, rk_ref, lnw_ref, lnb_ref, o_ref):
    d = o_ref.shape[-1]
    ones_bd = jnp.where(_block_mask(GROUP), 1.0, 0.0).astype(BF16)
    for j in range(d // GROUP):
        sl = slice(j * GROUP, (j + 1) * GROUP)
        y = y0_ref[:, sl] + y1_ref[:, sl]
        mean = _split_dot(y, ones_bd) * (1.0 / HEAD)
        yc = y - mean
        var = _split_dot(yc * yc, ones_bd) * (1.0 / HEAD)
        yn = (yc * lax.rsqrt(var + GN_EPS)) * lnw_ref[:, sl] + lnb_ref[:, sl]
        k = k_ref[:, sl]
        ka = ka_ref[:, sl]
        k_sum = k * (1.0 + (as0_ref[:, sl] - 1.0) * ka) + k * (1.0 + (as1_ref[:, sl] - 1.0) * ka)
        bonus = _split_dot(r_ref[:, sl] * k_sum * rk_ref[:, sl], ones_bd) * v_ref[:, sl]
        o_ref[:, sl] = ((yn + bonus) * g_ref[:, sl]).astype(o_ref.dtype)


def _rwkv_readout(y0, y1, r, k, v, as0, as1, g, ka, rk, lnw, lnb):
    b, t, d = r.shape
    tm = min(t, 256)
    spec = pl.BlockSpec((None, tm, d), lambda i, j: (i, j, 0))
    vec = pl.BlockSpec((1, d), lambda i, j: (0, 0))
    return pl.pallas_call(
        _readout_kernel,
        grid=(b, t // tm),
        in_specs=[spec] * 8 + [vec] * 4,
        out_specs=spec,
        out_shape=jax.ShapeDtypeStruct((b, t, d), BF16),
        compiler_params=_cparams(("parallel", "parallel"), 2 * 9 * tm * d * 4 + (8 << 20)),
        name="rwkv_readout",
    )(y0, y1, r, k, v, as0, as1, g, ka.reshape(1, d), rk.reshape(1, d), lnw.reshape(1, d), lnb.reshape(1, d))


def _mm_kernel(z_ref, w_ref, o_ref):
    o_ref[...] = _dot(z_ref[...], w_ref[...]).astype(o_ref.dtype)


def _mm_res_kernel(z_ref, w_ref, res_ref, gate_ref, o_ref):
    o_ref[...] = res_ref[...] + gate_ref[...] * _dot(z_ref[...], w_ref[...])


def _matmul(z, w, res=None, gate=None, out_dtype=F32):
    b, t, kdim = z.shape
    n = w.shape[1]
    tm = min(t, 1024)
    tn = 512
    zspec = pl.BlockSpec((None, tm, kdim), lambda i, j, l: (i, j, 0))
    wspec = pl.BlockSpec((kdim, tn), lambda i, j, l: (0, l))
    ospec = pl.BlockSpec((None, tm, tn), lambda i, j, l: (i, j, l))
    vmem = 2 * (tm * kdim * 2 + kdim * tn * 2 + 2 * tm * tn * 4) + tm * tn * 4 + (4 << 20)
    common = dict(grid=(b, t // tm, n // tn), out_specs=ospec,
                  compiler_params=_cparams(("parallel", "parallel", "parallel"), vmem))
    if res is None:
        return pl.pallas_call(_mm_kernel, in_specs=[zspec, wspec],
                              out_shape=jax.ShapeDtypeStruct((b, t, n), out_dtype), name="matmul", **common)(z, w)
    gspec = (pl.BlockSpec((None, 1, tn), lambda i, j, l: (0, 0, l)) if gate.shape[0] == 1
             else pl.BlockSpec((None, 1, tn), lambda i, j, l: (i, 0, l)))
    return pl.pallas_call(_mm_res_kernel, in_specs=[zspec, wspec, ospec, gspec],
                          out_shape=jax.ShapeDtypeStruct((b, t, n), F32), name="matmul_res", **common)(z, w, res, gate)


def _ffn_kernel(x_ref, g_ref, sh_ref, sc_ref, gate_ref, wg_ref, wu_ref, wd_ref, o_ref, h_ref):
    f = pl.program_id(2)

    @pl.when(f == 0)
    def _():
        h_ref[...] = _norm_mod_value(x_ref[...], g_ref[...], sh_ref[...], sc_ref[...]).astype(BF16)

    h = h_ref[...]
    gp = _dot(h, wg_ref[...])
    act = ((gp * jax.nn.sigmoid(gp)) * _dot(h, wu_ref[...])).astype(BF16)
    part = _dot(act, wd_ref[...])

    @pl.when(f == 0)
    def _():
        o_ref[...] = part

    @pl.when(f > 0)
    def _():
        o_ref[...] += part

    @pl.when(f == pl.num_programs(2) - 1)
    def _():
        o_ref[...] = x_ref[...] + gate_ref[...] * o_ref[...]


def _ffn(x, g, sh, sc, gate, wg, wu, wd):
    b, t, d = x.shape
    ff = wg.shape[1]
    tm = min(t, 512)
    tf = 512
    xspec = pl.BlockSpec((None, tm, d), lambda i, j, f: (i, j, 0))
    vmem = 4 * tm * d * 4 + tm * d * 2 + 2 * 3 * d * tf * 2 + 6 * tm * tf * 4 + tm * d * 4 + (4 << 20)
    return pl.pallas_call(
        _ffn_kernel,
        grid=(b, t // tm, ff // tf),
        in_specs=[xspec, pl.BlockSpec((1, d), lambda i, j, f: (0, 0)),
                  _row_spec(sh, d), _row_spec(sc, d), _row_spec(gate, d),
                  pl.BlockSpec((d, tf), lambda i, j, f: (0, f)),
                  pl.BlockSpec((d, tf), lambda i, j, f: (0, f)),
                  pl.BlockSpec((tf, d), lambda i, j, f: (f, 0))],
        out_specs=xspec,
        out_shape=jax.ShapeDtypeStruct((b, t, d), F32),
        scratch_shapes=[pltpu.VMEM((tm, d), BF16)],
        compiler_params=_cparams(("parallel", "parallel", "arbitrary"), vmem),
        name="ffn",
    )(x, g.reshape(1, d), sh, sc, gate, wg, wu, wd)


def _qk_norm(x, gain, ones_bd):
    ms = _split_dot(x * x, ones_bd) * (1.0 / HEAD)
    return x * lax.rsqrt(ms + QK_EPS) * gain


def _rope(u, cos, sin):
    n = u.shape[-1]
    lane = lax.broadcasted_iota(jnp.int32, u.shape, 1)
    partner = jnp.where(lane % 32 < 16, pltpu.roll(u, n - 16, 1), pltpu.roll(u, 16, 1))
    return u * cos + partner * sin


def _qkv_prep_kernel(*refs, rope, with_q):
    refs = list(refs)
    q_ref = refs.pop(0) if with_q else None
    k_ref, v_ref = refs.pop(0), refs.pop(0)
    qn_ref = refs.pop(0) if with_q else None
    kn_ref = refs.pop(0)
    if rope:
        cos_ref, sin_ref = refs.pop(0), refs.pop(0)
    qo_ref = refs.pop(0) if with_q else None
    ko_ref, vo_ref = refs
    d = ko_ref.shape[-1]
    ones_bd = jnp.where(_block_mask(GROUP), 1.0, 0.0).astype(BF16)
    if rope:
        cos, sin = cos_ref[...], sin_ref[...]
    for j in range(d // GROUP):
        sl = slice(j * GROUP, (j + 1) * GROUP)
        kh = _qk_norm(k_ref[:, sl], kn_ref[:, sl], ones_bd)
        if rope:
            kh = _rope(kh, cos, sin)
        ko_ref[:, sl] = kh.astype(BF16)
        if with_q:
            qh = _qk_norm(q_ref[:, sl], qn_ref[:, sl], ones_bd)
            if rope:
                qh = _rope(qh, cos, sin)
            qo_ref[:, sl] = (qh * (HEAD ** -0.5)).astype(BF16)
    vo_ref[...] = v_ref[...].astype(BF16)


def _qkv_prep(qkv, qn, kn, cos, sin, *, with_q):
    b, t, width = qkv.shape
    parts = 3 if with_q else 2
    d = width // parts
    rope = cos is not None
    tm = min(t, 256)
    col = lambda c: pl.BlockSpec((None, tm, d), lambda i, j, _c=c: (i, j, _c))
    vec = pl.BlockSpec((1, d), lambda i, j: (0, 0))
    tab = pl.BlockSpec((tm, GROUP), lambda i, j: (j, 0))
    in_specs = [col(c) for c in range(parts)] + [vec] * (2 if with_q else 1) + ([tab, tab] if rope else [])
    args = [qkv] * parts + ([qn] if with_q else []) + [kn] + ([cos, sin] if rope else [])
    out = jax.ShapeDtypeStruct((b, t, d), BF16)
    ospec = pl.BlockSpec((None, tm, d), lambda i, j: (i, j, 0))
    return pl.pallas_call(
        functools.partial(_qkv_prep_kernel, rope=rope, with_q=with_q),
        grid=(b, t // tm),
        in_specs=in_specs,
        out_specs=[ospec] * parts,
        out_shape=[out] * parts,
        compiler_params=_cparams(("parallel", "parallel"), 2 * parts * tm * d * 6 + (16 << 20)),
        name="qkv_prep",
    )(*args)


def _attn_kernel(lam_ref, q_ref, k_ref, v_ref, sub_ref, o_ref, *, tk, lambda_init):
    tq = q_ref.shape[0]
    tkeys = k_ref.shape[0]
    first = tkeys % tk or tk
    lp = lam_ref[...]
    lam = (jnp.exp(jnp.sum(lp[0:1] * lp[1:2], axis=-1, keepdims=True))
           - jnp.exp(jnp.sum(lp[2:3] * lp[3:4], axis=-1, keepdims=True)) + lambda_init)
    q = q_ref[...]
    lane = lax.broadcasted_iota(jnp.int32, q.shape, 1)
    zero = jnp.zeros_like(q)
    q2 = jnp.concatenate([jnp.where(lane < HEAD, q, zero), jnp.where(lane >= HEAD, q, zero)], axis=0)

    def step(ks, carry):
        m, l, acc = carry
        s = _dot_nt(q2, k_ref[ks, :])
        m_new = jnp.maximum(m, jnp.max(s, axis=-1, keepdims=True))
        alpha = jnp.exp(m - m_new)
        p = jnp.exp(s - m_new)
        l = alpha * l + jnp.sum(p, axis=-1, keepdims=True)
        acc = alpha * acc + _dot(p.astype(BF16), v_ref[ks, :])
        return m_new, l, acc

    m0 = jnp.full((2 * tq, 1), -jnp.inf, F32)
    l0 = jnp.zeros((2 * tq, 1), F32)
    acc0 = jnp.zeros((2 * tq, 2 * HEAD), F32)
    carry = step(slice(0, first), (m0, l0, acc0))
    m, l, acc = lax.fori_loop(
        0, (tkeys - first) // tk,
        lambda j, cr: step(pl.ds(pl.multiple_of(first + j * tk, first), tk), cr), carry)
    o = acc[:tq] / l[:tq] - lam * (acc[tq:] / l[tq:])
    o = o * lax.rsqrt(jnp.mean(o * o, axis=-1, keepdims=True) + SUBLN_EPS) * sub_ref[...]
    o_ref[...] = (o * (1.0 - lambda_init)).astype(o_ref.dtype)


def _diff_attention(q, k_all, v_all, lam_params, sub_gain, lambda_init):
    b, t, d = q.shape
    tkeys = k_all.shape[1]
    hw = 2 * HEAD
    tq = 256
    tk = 512
    assert t % tq == 0 and (tkeys % tk) % 128 == 0
    qspec = pl.BlockSpec((None, tq, hw), lambda i, h, j: (i, j, h))
    kspec = pl.BlockSpec((None, tkeys, hw), lambda i, h, j: (i, 0, h))
    vmem = 4 * tkeys * hw * 2 + 4 * tq * hw * 2 + 12 * 2 * tq * tk * 4 + (8 << 20)
    return pl.pallas_call(
        functools.partial(_attn_kernel, tk=tk, lambda_init=lambda_init),
        grid=(b, d // hw, t // tq),
        in_specs=[pl.BlockSpec((4, HEAD), lambda i, h, j: (0, 0)), qspec, kspec, kspec,
                  pl.BlockSpec((1, hw), lambda i, h, j: (0, 0))],
        out_specs=qspec,
        out_shape=jax.ShapeDtypeStruct((b, t, d), BF16),
        compiler_params=_cparams(("parallel", "parallel", "parallel"), vmem),
        name="diff_attention",
    )(lam_params, q, k_all, v_all, sub_gain.reshape(1, hw))


def _rope_tables(n_tokens):
    n_freq = HEAD // 4
    pos = jnp.arange(n_tokens)
    freqs = ROPE_BASE ** (-jnp.arange(n_freq, dtype=F32) / n_freq)
    ang_row = (pos // GRID_W).astype(F32)[:, None] * freqs
    ang_col = (pos % GRID_W).astype(F32)[:, None] * freqs
    cos = jnp.concatenate([jnp.cos(ang_row)] * 2 + [jnp.cos(ang_col)] * 2, axis=-1)
    sin = jnp.concatenate([-jnp.sin(ang_row), jnp.sin(ang_row), -jnp.sin(ang_col), jnp.sin(ang_col)], axis=-1)
    reps = GROUP // HEAD
    return jnp.tile(cos, (1, reps)), jnp.tile(sin, (1, reps))


def _mod_rows(mod, layer, j, d, batch):
    m = mod[layer, :, j * d:(j + 1) * d]
    return m[:batch, None, :], m[batch:batch + 1, None, :]


def _rwkv_layer(xs, mods, norm_g, p):
    rkv, lora, outs = [], [], []
    for x, (sh, sc, _) in zip(xs, mods):
        h = _norm_mod(x, norm_g, sh, sc, F32)
        rkv.append(_rwkv_rkv(h, p["mu"], p["wr"], p["wk"], p["wv"]))
        lora.append(_rwkv_lora(h, p["mu"], p["w0"], p["w1"], p["w2"], p["a0"], p["a1"], p["a2"], p["g1"], p["g2"]))
    b, _, d = xs[0].shape
    state = jnp.zeros((b, 2, d // GROUP, GROUP, GROUP), F32)
    for x, (_, _, gate), (r, k, v), (g, lw0, lw1, as0, as1) in zip(xs, mods, rkv, lora):
        y0, y1, state = _rwkv_scan(r, k, v, lw0, as0, lw1, as1, p["kk"], p["ka"], state)
        z = _rwkv_readout(y0, y1, r, k, v, as0, as1, g, p["ka"], p["rk"], p["lnw"], p["lnb"])
        outs.append(_matmul(z, p["wo"], res=x, gate=gate))
    return outs


def kernel(x, c, ctx, c_ctx, mod_w, mod_b, norm1_g, norm2_g, rwkv_mu, rwkv_wr, rwkv_wk, rwkv_wv, rwkv_wo, rwkv_w0, rwkv_w1, rwkv_w2, rwkv_a0, rwkv_a1, rwkv_a2, rwkv_g1, rwkv_g2, rwkv_kk, rwkv_ka, rwkv_rk, rwkv_lnw, rwkv_lnb, diff_wqkv, diff_qn, diff_kn, diff_lq1, diff_lk1, diff_lq2, diff_lk2, diff_subln, diff_wo, ffn_wg, ffn_wu, ffn_wd):
    batch, n_lat, d = x.shape
    depth = mod_w.shape[0]
    assert depth == 2, "layer 0 is the RWKV-7 mixer, layer 1 differential attention"

    cvecs = jnp.zeros((SUBLANES, d), F32).at[:batch].set(c).at[batch].set(c_ctx)
    mod = _adaln(cvecs, mod_w, mod_b)

    def mods(layer, first):
        lat, cx = zip(*[_mod_rows(mod, layer, first + j, d, batch) for j in range(3)])
        return cx, lat

    w1, w2 = _pad_lora(rwkv_w1[0], rwkv_w2[0])
    a1, a2 = _pad_lora(rwkv_a1[0], rwkv_a2[0])
    g1, g2 = _pad_lora(rwkv_g1[0], rwkv_g2[0])
    p = dict(mu=rwkv_mu[0], wr=rwkv_wr[0].astype(BF16), wk=rwkv_wk[0].astype(BF16), wv=rwkv_wv[0].astype(BF16),
             wo=rwkv_wo[0].astype(BF16), w0=rwkv_w0[0], w1=w1, w2=w2, a0=rwkv_a0[0], a1=a1, a2=a2,
             g1=g1, g2=g2, kk=rwkv_kk[0], ka=rwkv_ka[0],
             rk=rwkv_rk[0], lnw=rwkv_lnw[0], lnb=rwkv_lnb[0])
    xc, xl = _rwkv_layer((ctx, x), mods(0, 0), norm1_g[0], p)
    wg, wu, wd = ffn_wg[0].astype(BF16), ffn_wu[0].astype(BF16), ffn_wd[0].astype(BF16)
    (csh, csc, cgt), (lsh, lsc, lgt) = mods(0, 3)
    xc = _ffn(xc, norm2_g[0], csh, csc, cgt, wg, wu, wd)
    xl = _ffn(xl, norm2_g[0], lsh, lsc, lgt, wg, wu, wd)

    (csh, csc, _), (lsh, lsc, lgt) = mods(1, 0)
    lambda_init = 0.8 - 0.6 * math.exp(-0.3 * 1)
    wqkv = diff_wqkv[0].astype(BF16)
    hl = _norm_mod(xl, norm1_g[1], lsh, lsc, BF16)
    hc = _norm_mod(xc, norm1_g[1], csh, csc, BF16)
    qn = jnp.tile(diff_qn[0], d // HEAD).reshape(1, d)
    kn = jnp.tile(diff_kn[0], d // HEAD).reshape(1, d)
    cos, sin = _rope_tables(n_lat)
    q_l, k_l, v_l = _qkv_prep(_matmul(hl, wqkv), qn, kn, cos, sin, with_q=True)
    k_c, v_c = _qkv_prep(_matmul(hc, wqkv[:, d:]), None, kn, None, None, with_q=False)
    k_all = jnp.concatenate([k_c, k_l], axis=1)
    v_all = jnp.concatenate([v_c, v_l], axis=1)
    lam_params = jnp.stack([diff_lq1[0], diff_lk1[0], diff_lq2[0], diff_lk2[0]])
    o = _diff_attention(q_l, k_all, v_all, lam_params, diff_subln[0], lambda_init)
    xl = _matmul(o, diff_wo[0].astype(BF16), res=xl, gate=lgt)
    (_, _, _), (lsh, lsc, lgt) = mods(1, 3)
    return _ffn(xl, norm2_g[1], lsh, lsc, lgt,
                ffn_wg[1].astype(BF16), ffn_wu[1].astype(BF16), ffn_wd[1].astype(BF16))
```

```python
import functools
import math

import jax
import jax.numpy as jnp
from jax import lax
from jax.experimental import pallas as pl
from jax.experimental.pallas import tpu as pltpu

F32 = jnp.float32
BF16 = jnp.bfloat16

HEAD = 64
GROUP = 256
CHUNK = 64
SCAN_UNROLL = 8
ATTN_CHAINS = 4
SUBLANES = 8
V7X_VMEM_CAP = 56 * 1024 * 1024

NORM_EPS = 1e-6
GN_EPS = 64e-5
L2_EPS = 1e-24
QK_EPS = 1e-6
SUBLN_EPS = 1e-5
ROPE_BASE = 10000.0
GRID_W = 64
DECAY_SCALE = math.exp(-0.5)
LOG2E = math.log2(math.e)


def _cparams(semantics, vmem_bytes):
    return pltpu.CompilerParams(dimension_semantics=semantics,
                                vmem_limit_bytes=int(min(max(vmem_bytes, 16 << 20), V7X_VMEM_CAP)))


def _dot(a, b):
    return jnp.dot(a, b, preferred_element_type=F32)


def _dot_nt(a, b):
    return lax.dot_general(a, b, (((1,), (1,)), ((), ())), preferred_element_type=F32)


def _dot_tn(a, b):
    return lax.dot_general(a, b, (((0,), (0,)), ((), ())), preferred_element_type=F32)


def _split_dot(x, w_bf16):
    hi = x.astype(BF16)
    lo = (x - hi.astype(F32)).astype(BF16)
    return _dot(hi, w_bf16) + _dot(lo, w_bf16)


def _block_mask(n):
    r = lax.broadcasted_iota(jnp.int32, (n, n), 0) // HEAD
    c = lax.broadcasted_iota(jnp.int32, (n, n), 1) // HEAD
    return r == c


def _adaln_kernel(c_ref, w_ref, b_ref, o_ref):
    c = c_ref[...]
    s = (c * jax.nn.sigmoid(c)).astype(BF16)
    o_ref[...] = _dot(s, w_ref[...].astype(BF16)) + b_ref[...]


def _adaln(cvecs, mod_w, mod_b):
    depth, d, n = mod_w.shape
    rows = cvecs.shape[0]
    tn = 1024
    return pl.pallas_call(
        _adaln_kernel,
        grid=(depth, n // tn),
        in_specs=[pl.BlockSpec((rows, d), lambda l, j: (0, 0)),
                  pl.BlockSpec((None, d, tn), lambda l, j: (l, 0, j)),
                  pl.BlockSpec((None, 1, tn), lambda l, j: (l, 0, j))],
        out_specs=pl.BlockSpec((None, rows, tn), lambda l, j: (l, 0, j)),
        out_shape=jax.ShapeDtypeStruct((depth, rows, n), F32),
        compiler_params=_cparams(("parallel", "parallel"), 3 * d * tn * 4 + (8 << 20)),
        name="adaln",
    )(cvecs, mod_w, mod_b.reshape(depth, 1, n))


def _norm_mod_value(x, g, sh, sc):
    y = x * lax.rsqrt(jnp.mean(x * x, axis=-1, keepdims=True) + NORM_EPS)
    return (y * g) * (1.0 + sc) + sh


def _norm_mod_kernel(x_ref, g_ref, sh_ref, sc_ref, o_ref):
    o_ref[...] = _norm_mod_value(x_ref[...], g_ref[...], sh_ref[...], sc_ref[...]).astype(o_ref.dtype)


def _row_spec(arr, d):
    if arr.shape[0] == 1:
        return pl.BlockSpec((None, 1, d), lambda b, *_: (0, 0, 0))
    return pl.BlockSpec((None, 1, d), lambda b, *_: (b, 0, 0))


def _norm_mod(x, g, sh, sc, out_dtype):
    b, t, d = x.shape
    tr = min(t, 512)
    return pl.pallas_call(
        _norm_mod_kernel,
        grid=(b, t // tr),
        in_specs=[pl.BlockSpec((None, tr, d), lambda i, j: (i, j, 0)),
                  pl.BlockSpec((1, d), lambda i, j: (0, 0)),
                  _row_spec(sh, d), _row_spec(sc, d)],
        out_specs=pl.BlockSpec((None, tr, d), lambda i, j: (i, j, 0)),
        out_shape=jax.ShapeDtypeStruct((b, t, d), out_dtype),
        compiler_params=_cparams(("parallel", "parallel"), 6 * tr * d * 4),
        name="norm_mod",
    )(x, g.reshape(1, d), sh, sc)


def _shift_delta(h, prev_ref, next_ref, t, nt):
    tm = h.shape[0]
    row = lax.broadcasted_iota(jnp.int32, h.shape, 0)
    prev_row = jnp.where(t > 0, prev_ref[SUBLANES - 1:SUBLANES, :], 0.0)
    next_row = jnp.where(t < nt - 1, next_ref[0:1, :], 0.0)
    hp = jnp.where(row == 0, prev_row, pltpu.roll(h, 1, 0))
    hn = jnp.where(row == tm - 1, next_row, pltpu.roll(h, tm - 1, 0))
    return 0.5 * (hp + hn) - h


def _halo_specs(t, tm, d):
    per = tm // SUBLANES
    last = t // SUBLANES - 1
    return [pl.BlockSpec((None, tm, d), lambda b, i, *_: (b, i, 0)),
            pl.BlockSpec((None, SUBLANES, d), lambda b, i, *_: (b, jnp.maximum(i * per - 1, 0), 0)),
            pl.BlockSpec((None, SUBLANES, d), lambda b, i, *_: (b, jnp.minimum((i + 1) * per, last), 0))]


def _rkv_kernel(h_ref, hp_ref, hn_ref, mu_ref, wr_ref, wk_ref, wv_ref, r_ref, k_ref, v_ref, xs_ref, *, nt):
    t = pl.program_id(1)

    @pl.when(pl.program_id(2) == 0)
    def _():
        h = h_ref[...]
        xx = _shift_delta(h, hp_ref, hn_ref, t, nt)
        xs_ref[0] = (h + xx * mu_ref[0:1, :]).astype(BF16)
        xs_ref[1] = (h + xx * mu_ref[2:3, :]).astype(BF16)
        xs_ref[2] = (h + xx * mu_ref[3:4, :]).astype(BF16)

    r_ref[...] = _dot(xs_ref[0], wr_ref[...])
    k_ref[...] = _dot(xs_ref[1], wk_ref[...])
    v_ref[...] = _dot(xs_ref[2], wv_ref[...])


def _rwkv_rkv(h, mu, wr, wk, wv):
    b, t, d = h.shape
    tm = min(t, 512)
    tn = 512
    nt = t // tm
    out = jax.ShapeDtypeStruct((b, t, d), F32)
    wspec = pl.BlockSpec((d, tn), lambda i, j, n: (0, n))
    ospec = pl.BlockSpec((None, tm, tn), lambda i, j, n: (i, j, n))
    vmem = 2 * tm * d * 4 + 3 * tm * d * 2 + 6 * d * tn * 2 + 6 * tm * tn * 4 + 3 * tm * d * 4 + (4 << 20)
    return pl.pallas_call(
        functools.partial(_rkv_kernel, nt=nt),
        grid=(b, nt, d // tn),
        in_specs=_halo_specs(t, tm, d) + [pl.BlockSpec((6, d), lambda i, j, n: (0, 0)), wspec, wspec, wspec],
        out_specs=[ospec, ospec, ospec],
        out_shape=[out, out, out],
        scratch_shapes=[pltpu.VMEM((3, tm, d), BF16)],
        compiler_params=_cparams(("parallel", "parallel", "arbitrary"), vmem),
        name="rwkv_rkv",
    )(h, h, h, mu, wr, wk, wv)


def _lora_kernel(h_ref, hp_ref, hn_ref, mu_ref, w0_ref, w1_ref, w2_ref, a0_ref, a1_ref, a2_ref, g1_ref, g2_ref,
                 g_ref, lw0_ref, lw1_ref, as0_ref, as1_ref, *, nt):
    t = pl.program_id(1)
    h = h_ref[...]
    xx = _shift_delta(h, hp_ref, hn_ref, t, nt)
    xw = (h + xx * mu_ref[1:2, :]).astype(BF16)
    xa = (h + xx * mu_ref[4:5, :]).astype(BF16)
    xg = (h + xx * mu_ref[5:6, :]).astype(BF16)
    g_ref[...] = _dot(jax.nn.sigmoid(_dot(xg, g1_ref[...])).astype(BF16), g2_ref[...])
    for d, (lw_ref, as_ref) in enumerate(((lw0_ref, as0_ref), (lw1_ref, as1_ref))):
        w_pre = w0_ref[d:d + 1, :] + _dot(jnp.tanh(_dot(xw, w1_ref[d])).astype(BF16), w2_ref[d])
        lw_ref[...] = -DECAY_SCALE * jax.nn.sigmoid(w_pre)
        a_pre = a0_ref[d:d + 1, :] + _dot(_dot(xa, a1_ref[d]).astype(BF16), a2_ref[d])
        as_ref[...] = jax.nn.sigmoid(a_pre)


def _pad_lora(w_in, w_out):
    rank = w_in.shape[-1]
    pad = (-rank) % 128
    w_in = jnp.pad(w_in, [(0, 0)] * (w_in.ndim - 1) + [(0, pad)])
    w_out = jnp.pad(w_out, [(0, 0)] * (w_out.ndim - 2) + [(0, pad), (0, 0)])
    return w_in.astype(BF16), w_out.astype(BF16)


def _rwkv_lora(h, mu, w0, w1, w2, a0, a1, a2, g1, g2):
    b, t, d = h.shape
    tm = min(t, 256)
    nt = t // tm
    out = jax.ShapeDtypeStruct((b, t, d), F32)
    ospec = pl.BlockSpec((None, tm, d), lambda i, j: (i, j, 0))

    def full(a):
        return pl.BlockSpec(a.shape, lambda i, j, _n=a.ndim: (0,) * _n)

    params = (mu, w0, w1, w2, a0, a1, a2, g1, g2)
    vmem = 2 * 6 * tm * d * 4 + 10 * tm * d * 4 + 2 * sum(p.size * p.dtype.itemsize for p in params)
    return pl.pallas_call(
        functools.partial(_lora_kernel, nt=nt),
        grid=(b, nt),
        in_specs=_halo_specs(t, tm, d) + [full(p) for p in params],
        out_specs=[ospec] * 5,
        out_shape=[out] * 5,
        compiler_params=_cparams(("parallel", "parallel"), vmem),
        name="rwkv_lora",
    )(h, h, h, *params)


def _split_dot_left(w_bf16, x):
    hi = x.astype(BF16)
    lo = (x - hi.astype(F32)).astype(BF16)
    return _dot(w_bf16, hi) + _dot(w_bf16, lo)


def _scan_chain(d, refs, kks, ka, state_ref, y_ref, sl, g, consts):
    r_ref, k_ref, v_ref, lw_ref, as_ref = refs
    bmask, ones_bd, tri, m_strict, m_incl = consts
    c = CHUNK

    def stack(x):
        return jnp.where(bmask, jnp.concatenate([x] * (GROUP // c), axis=0), 0.0).astype(BF16)

    lw = lw_ref[:, sl]
    a_sig = as_ref[:, sl]
    k = k_ref[:, sl]
    kk = k * kks
    ssq = _split_dot(kk * kk, ones_bd)
    cum = _split_dot_left(tri[d], lw)
    yield

    kk = kk * lax.rsqrt(jnp.maximum(ssq, L2_EPS))
    b_vec = kk * a_sig
    k_d = k * (1.0 + (a_sig - 1.0) * ka)
    tot = cum[c - 1:c, :] if d == 0 else cum[0:1, :]
    a_t = -kk * jnp.exp(cum - lw)
    r_t = r_ref[:, sl] * jnp.exp(cum)
    p_inv = jnp.exp(-cum)
    lhs = jnp.concatenate([a_t, r_t], axis=0).astype(BF16)
    rhs = jnp.concatenate([stack(b_vec * p_inv), stack(k_d * p_inv)], axis=0)
    aa = _dot_nt(lhs, rhs)
    yield

    n1 = jnp.where(m_strict[d], aa[:c, :GROUP], 0.0)
    a_ak = jnp.where(m_strict[d], aa[:c, GROUP:], 0.0)
    a_rb = jnp.where(m_incl[d], aa[c:, :GROUP], 0.0).astype(BF16)
    a_rk = jnp.where(m_incl[d], aa[c:, GROUP:], 0.0)
    v = v_ref[:, sl]
    z1 = _dot(jnp.concatenate([a_ak, a_rk], axis=0).astype(BF16), stack(v))
    q = n1
    npow = _dot(n1.astype(BF16), stack(n1))
    yield
    span = 2
    while 2 * span < c:
        z = _dot(jnp.concatenate([npow, q], axis=0).astype(BF16), stack(npow))
        yield
        q = q + npow + z[c:]
        npow = z[:c]
        span *= 2
    zq = _dot(q.astype(BF16), stack(npow))
    yield
    q = q + npow + zq

    v_ak = z1[:c]
    y0 = z1[c:]
    z2 = _dot(q.astype(BF16), jnp.concatenate([stack(a_t), stack(v_ak)], axis=1))
    yield
    w = a_t + z2[:, :GROUP]
    u0 = v_ak + z2[:, GROUP:]

    s = state_ref[d, g]
    z3 = _dot_nt(jnp.concatenate([w, r_t], axis=0).astype(BF16), s.astype(BF16))
    yield
    u = z3[:c] + u0
    p_end = jnp.exp(tot - cum)
    yu = _dot(a_rb, stack(u))
    upd = _dot_tn(jnp.concatenate([u, v], axis=0).astype(BF16),
                  jnp.concatenate([b_vec * p_end, k_d * p_end], axis=0).astype(BF16))
    yield
    y_ref[:, sl] = z3[c:] + yu + y0
    state_ref[d, g] = s * jnp.exp(tot) + jnp.where(bmask, upd, 0.0)


def _interleave(chains):
    chains = list(chains)
    while chains:
        alive = []
        for ch in chains:
            try:
                next(ch)
                alive.append(ch)
            except StopIteration:
                pass
        chains = alive


def _scan_kernel(r0, k0, v0, lw0, as0, r1, k1, v1, lw1, as1, kks_ref, ka_ref, sin_ref,
                 y0_ref, y1_ref, state_ref, *, groups):
    i = pl.program_id(1)
    c = CHUNK
    unroll = math.gcd(groups, SCAN_UNROLL)

    @pl.when(i == 0)
    def _():
        state_ref[...] = sin_ref[...]

    bmask = _block_mask(GROUP)
    ones_bd = jnp.where(bmask, 1.0, 0.0).astype(BF16)
    tt = lax.broadcasted_iota(jnp.int32, (c, c), 0)
    ss = lax.broadcasted_iota(jnp.int32, (c, c), 1)
    tri = (jnp.where(ss <= tt, 1.0, 0.0).astype(BF16), jnp.where(ss >= tt, 1.0, 0.0).astype(BF16))
    tp = lax.broadcasted_iota(jnp.int32, (c, GROUP), 0)
    sp = lax.broadcasted_iota(jnp.int32, (c, GROUP), 1) % c
    m_strict = (sp < tp, sp > tp)
    m_incl = (sp <= tp, sp >= tp)
    consts = (bmask, ones_bd, tri, m_strict, m_incl)

    def body(it, carry):
        chains = []
        for j in range(unroll):
            g = it * unroll + j
            sl = pl.ds(pl.multiple_of(g * GROUP, GROUP), GROUP)
            kks = kks_ref[:, sl]
            ka = ka_ref[:, sl]
            chains.append(_scan_chain(0, (r0, k0, v0, lw0, as0), kks, ka, state_ref, y0_ref, sl, g, consts))
            chains.append(_scan_chain(1, (r1, k1, v1, lw1, as1), kks, ka, state_ref, y1_ref, sl, g, consts))
        _interleave(chains)
        return carry

    lax.fori_loop(0, groups // unroll, body, 0)


def _rwkv_scan(r, k, v, lw0, as0, lw1, as1, kk_scale, ka, s_init):
    b, t, d = r.shape
    assert CHUNK * (GROUP // HEAD) == GROUP and t % CHUNK == 0 and d % GROUP == 0
    groups = d // GROUP
    nc = t // CHUNK
    fwd = pl.BlockSpec((None, CHUNK, d), lambda i, j: (i, j, 0))
    bwd = pl.BlockSpec((None, CHUNK, d), lambda i, j: (i, nc - 1 - j, 0))
    vec = pl.BlockSpec((1, d), lambda i, j: (0, 0))
    sspec = pl.BlockSpec((None, 2, groups, GROUP, GROUP), lambda i, j: (i, 0, 0, 0, 0))
    y = jax.ShapeDtypeStruct((b, t, d), F32)
    state_bytes = 2 * groups * GROUP * GROUP * 4
    vmem = 2 * 12 * CHUNK * d * 4 + 4 * state_bytes + 2 * SCAN_UNROLL * (3 << 19) + (4 << 20)
    return pl.pallas_call(
        functools.partial(_scan_kernel, groups=groups),
        grid=(b, nc),
        in_specs=[fwd] * 5 + [bwd] * 5 + [vec, vec, sspec],
        out_specs=[fwd, bwd, sspec],
        out_shape=[y, y, jax.ShapeDtypeStruct(s_init.shape, F32)],
        compiler_params=_cparams(("parallel", "arbitrary"), vmem),
        name="rwkv_scan",
    )(r, k, v, lw0, as0, r, k, v, lw1, as1, kk_scale.reshape(1, d), ka.reshape(1, d), s_init)


def _readout_kernel(y0_ref, y1_ref, r_ref, k_ref, v_ref, as0_ref, as1_ref, g_ref,
                    ka_ref, rk_ref, lnw_ref, lnb_ref, o_ref):
    d = o_ref.shape[-1]
    ones_bd = jnp.where(_block_mask(GROUP), 1.0, 0.0).astype(BF16)
    for j in range(d // GROUP):
        sl = slice(j * GROUP, (j + 1) * GROUP)
        y = y0_ref[:, sl] + y1_ref[:, sl]
        mean = _split_dot(y, ones_bd) * (1.0 / HEAD)
        yc = y - mean
        var = _split_dot(yc * yc, ones_bd) * (1.0 / HEAD)
        yn = (yc * lax.rsqrt(var + GN_EPS)) * lnw_ref[:, sl] + lnb_ref[:, sl]
        k = k_ref[:, sl]
        ka = ka_ref[:, sl]
        k_sum = k * (1.0 + (as0_ref[:, sl] - 1.0) * ka) + k * (1.0 + (as1_ref[:, sl] - 1.0) * ka)
        bonus = _split_dot(r_ref[:, sl] * k_sum * rk_ref[:, sl], ones_bd) * v_ref[:, sl]
        o_ref[:, sl] = ((yn + bonus) * g_ref[:, sl]).astype(o_ref.dtype)


def _rwkv_readout(y0, y1, r, k, v, as0, as1, g, ka, rk, lnw, lnb):
    b, t, d = r.shape
    tm = min(t, 256)
    spec = pl.BlockSpec((None, tm, d), lambda i, j: (i, j, 0))
    vec = pl.BlockSpec((1, d), lambda i, j: (0, 0))
    return pl.pallas_call(
        _readout_kernel,
        grid=(b, t // tm),
        in_specs=[spec] * 8 + [vec] * 4,
        out_specs=spec,
        out_shape=jax.ShapeDtypeStruct((b, t, d), BF16),
        compiler_params=_cparams(("parallel", "parallel"), 2 * 9 * tm * d * 4 + (8 << 20)),
        name="rwkv_readout",
    )(y0, y1, r, k, v, as0, as1, g, ka.reshape(1, d), rk.reshape(1, d), lnw.reshape(1, d), lnb.reshape(1, d))


def _mm_kernel(z_ref, w_ref, o_ref):
    o_ref[...] = _dot(z_ref[...], w_ref[...]).astype(o_ref.dtype)


def _mm_res_kernel(z_ref, w_ref, res_ref, gate_ref, o_ref):
    o_ref[...] = res_ref[...] + gate_ref[...] * _dot(z_ref[...], w_ref[...])


def _matmul(z, w, res=None, gate=None, out_dtype=F32):
    b, t, kdim = z.shape
    n = w.shape[1]
    tm = min(t, 1024)
    tn = 512
    zspec = pl.BlockSpec((None, tm, kdim), lambda i, j, l: (i, j, 0))
    wspec = pl.BlockSpec((kdim, tn), lambda i, j, l: (0, l))
    ospec = pl.BlockSpec((None, tm, tn), lambda i, j, l: (i, j, l))
    vmem = 2 * (tm * kdim * 2 + kdim * tn * 2 + 2 * tm * tn * 4) + tm * tn * 4 + (4 << 20)
    common = dict(grid=(b, t // tm, n // tn), out_specs=ospec,
                  compiler_params=_cparams(("parallel", "parallel", "parallel"), vmem))
    if res is None:
        return pl.pallas_call(_mm_kernel, in_specs=[zspec, wspec],
                              out_shape=jax.ShapeDtypeStruct((b, t, n), out_dtype), name="matmul", **common)(z, w)
    gspec = (pl.BlockSpec((None, 1, tn), lambda i, j, l: (0, 0, l)) if gate.shape[0] == 1
             else pl.BlockSpec((None, 1, tn), lambda i, j, l: (i, 0, l)))
    return pl.pallas_call(_mm_res_kernel, in_specs=[zspec, wspec, ospec, gspec],
                          out_shape=jax.ShapeDtypeStruct((b, t, n), F32), name="matmul_res", **common)(z, w, res, gate)


def _ffn_kernel(x_ref, g_ref, sh_ref, sc_ref, gate_ref, wg_ref, wu_ref, wd_ref, o_ref, h_ref):
    f = pl.program_id(2)

    @pl.when(f == 0)
    def _():
        h_ref[...] = _norm_mod_value(x_ref[...], g_ref[...], sh_ref[...], sc_ref[...]).astype(BF16)

    h = h_ref[...]
    gp = _dot(h, wg_ref[...])
    act = ((gp * jax.nn.sigmoid(gp)) * _dot(h, wu_ref[...])).astype(BF16)
    part = _dot(act, wd_ref[...])

    @pl.when(f == 0)
    def _():
        o_ref[...] = part

    @pl.when(f > 0)
    def _():
        o_ref[...] += part

    @pl.when(f == pl.num_programs(2) - 1)
    def _():
        o_ref[...] = x_ref[...] + gate_ref[...] * o_ref[...]


def _ffn(x, g, sh, sc, gate, wg, wu, wd):
    b, t, d = x.shape
    ff = wg.shape[1]
    tm = min(t, 512)
    tf = 512
    xspec = pl.BlockSpec((None, tm, d), lambda i, j, f: (i, j, 0))
    vmem = 4 * tm * d * 4 + tm * d * 2 + 2 * 3 * d * tf * 2 + 6 * tm * tf * 4 + tm * d * 4 + (4 << 20)
    return pl.pallas_call(
        _ffn_kernel,
        grid=(b, t // tm, ff // tf),
        in_specs=[xspec, pl.BlockSpec((1, d), lambda i, j, f: (0, 0)),
                  _row_spec(sh, d), _row_spec(sc, d), _row_spec(gate, d),
                  pl.BlockSpec((d, tf), lambda i, j, f: (0, f)),
                  pl.BlockSpec((d, tf), lambda i, j, f: (0, f)),
                  pl.BlockSpec((tf, d), lambda i, j, f: (f, 0))],
        out_specs=xspec,
        out_shape=jax.ShapeDtypeStruct((b, t, d), F32),
        scratch_shapes=[pltpu.VMEM((tm, d), BF16)],
        compiler_params=_cparams(("parallel", "parallel", "arbitrary"), vmem),
        name="ffn",
    )(x, g.reshape(1, d), sh, sc, gate, wg, wu, wd)


def _qk_norm(x, gain, ones_bd):
    ms = _split_dot(x * x, ones_bd) * (1.0 / HEAD)
    return x * lax.rsqrt(ms + QK_EPS) * gain


def _rope(u, cos, sin):
    n = u.shape[-1]
    lane = lax.broadcasted_iota(jnp.int32, u.shape, 1)
    partner = jnp.where(lane % 32 < 16, pltpu.roll(u, n - 16, 1), pltpu.roll(u, 16, 1))
    return u * cos + partner * sin


def _qkv_prep_kernel(*refs, rope, with_q):
    refs = list(refs)
    q_ref = refs.pop(0) if with_q else None
    k_ref, v_ref = refs.pop(0), refs.pop(0)
    qn_ref = refs.pop(0) if with_q else None
    kn_ref = refs.pop(0)
    if rope:
        cos_ref, sin_ref = refs.pop(0), refs.pop(0)
    qo_ref = refs.pop(0) if with_q else None
    ko_ref, vo_ref = refs
    d = ko_ref.shape[-1]
    ones_bd = jnp.where(_block_mask(GROUP), 1.0, 0.0).astype(BF16)
    if rope:
        cos, sin = cos_ref[...], sin_ref[...]
    for j in range(d // GROUP):
        sl = slice(j * GROUP, (j + 1) * GROUP)
        kh = _qk_norm(k_ref[:, sl], kn_ref[:, sl], ones_bd)
        if rope:
            kh = _rope(kh, cos, sin)
        ko_ref[:, sl] = kh.astype(BF16)
        if with_q:
            qh = _qk_norm(q_ref[:, sl], qn_ref[:, sl], ones_bd)
            if rope:
                qh = _rope(qh, cos, sin)
            qo_ref[:, sl] = (qh * (HEAD ** -0.5 * LOG2E)).astype(BF16)
    vo_ref[...] = v_ref[...].T.astype(BF16)


def _qkv_prep(qkv, qn, kn, cos, sin, *, with_q):
    b, t, width = qkv.shape
    parts = 3 if with_q else 2
    d = width // parts
    rope = cos is not None
    tm = min(t, 256)
    col = lambda c: pl.BlockSpec((None, tm, d), lambda i, j, _c=c: (i, j, _c))
    vec = pl.BlockSpec((1, d), lambda i, j: (0, 0))
    tab = pl.BlockSpec((tm, GROUP), lambda i, j: (j, 0))
    in_specs = [col(c) for c in range(parts)] + [vec] * (2 if with_q else 1) + ([tab, tab] if rope else [])
    args = [qkv] * parts + ([qn] if with_q else []) + [kn] + ([cos, sin] if rope else [])
    out = jax.ShapeDtypeStruct((b, t, d), BF16)
    ospec = pl.BlockSpec((None, tm, d), lambda i, j: (i, j, 0))
    out_t = jax.ShapeDtypeStruct((b, d, t), BF16)
    ospec_t = pl.BlockSpec((None, d, tm), lambda i, j: (i, 0, j))
    return pl.pallas_call(
        functools.partial(_qkv_prep_kernel, rope=rope, with_q=with_q),
        grid=(b, t // tm),
        in_specs=in_specs,
        out_specs=[ospec] * (parts - 1) + [ospec_t],
        out_shape=[out] * (parts - 1) + [out_t],
        compiler_params=_cparams(("parallel", "parallel"), 2 * parts * tm * d * 6 + (16 << 20)),
        name="qkv_prep",
    )(*args)


def _attn_chain(q2, k, vt, state):
    m, l, acc_t = state
    s_t = _dot_nt(k, q2)
    yield
    m_new = jnp.maximum(m, jnp.max(s_t, axis=0, keepdims=True))
    alpha = jnp.exp2(m - m_new)
    p = jnp.exp2(s_t - m_new)
    state[0] = m_new
    state[1] = alpha * l + jnp.sum(p, axis=0, keepdims=True)
    pv = _dot(vt, p.astype(BF16))
    yield
    state[2] = alpha * acc_t + pv


def _attn_kernel(lam_ref, q_ref, k_ref, vt_ref, sub_ref, o_ref, *, tk, tq, lambda_init):
    n_sub = q_ref.shape[0] // tq
    tkeys = k_ref.shape[0]
    first = tkeys % tk or tk
    lp = lam_ref[...]
    lam = (jnp.exp(jnp.sum(lp[0:1] * lp[1:2], axis=-1, keepdims=True))
           - jnp.exp(jnp.sum(lp[2:3] * lp[3:4], axis=-1, keepdims=True)) + lambda_init)
    q2s = []
    for c in range(n_sub):
        q = q_ref[c * tq:(c + 1) * tq, :]
        lane = lax.broadcasted_iota(jnp.int32, q.shape, 1)
        zero = jnp.zeros_like(q)
        q2s.append(jnp.concatenate([jnp.where(lane < HEAD, q, zero), jnp.where(lane >= HEAD, q, zero)], axis=0))

    def step(ks, carry):
        k = k_ref[ks, :]
        vt = vt_ref[:, ks]
        states = [list(st) for st in carry]
        _interleave([_attn_chain(q2, k, vt, st) for q2, st in zip(q2s, states)])
        return tuple(tuple(st) for st in states)

    init = (jnp.full((1, 2 * tq), -jnp.inf, F32), jnp.zeros((1, 2 * tq), F32),
            jnp.zeros((2 * HEAD, 2 * tq), F32))
    carry = step(slice(0, first), (init,) * n_sub)
    carry = lax.fori_loop(
        0, (tkeys - first) // tk,
        lambda j, cr: step(pl.ds(pl.multiple_of(first + j * tk, first), tk), cr), carry)
    for c, (_, l, acc_t) in enumerate(carry):
        o_t = acc_t[:, :tq] / l[:, :tq] - lam * (acc_t[:, tq:] / l[:, tq:])
        o = o_t.T
        o = o * lax.rsqrt(jnp.mean(o * o, axis=-1, keepdims=True) + SUBLN_EPS) * sub_ref[...]
        o_ref[c * tq:(c + 1) * tq, :] = (o * (1.0 - lambda_init)).astype(o_ref.dtype)


def _diff_attention(q, k_all, vt_all, lam_params, sub_gain, lambda_init):
    b, t, d = q.shape
    tkeys = k_all.shape[1]
    hw = 2 * HEAD
    tq = 256
    n_sub = math.gcd(t // tq, ATTN_CHAINS)
    tk = 512
    assert t % (tq * n_sub) == 0 and (tkeys % tk) % 128 == 0
    qspec = pl.BlockSpec((None, tq * n_sub, hw), lambda i, h, j: (i, j, h))
    kspec = pl.BlockSpec((None, tkeys, hw), lambda i, h, j: (i, 0, h))
    vspec = pl.BlockSpec((None, hw, tkeys), lambda i, h, j: (i, h, 0))
    vmem = 4 * tkeys * hw * 2 + 4 * n_sub * tq * hw * 2 + n_sub * 8 * 2 * tq * tk * 4 + (8 << 20)
    return pl.pallas_call(
        functools.partial(_attn_kernel, tk=tk, tq=tq, lambda_init=lambda_init),
        grid=(b, d // hw, t // (tq * n_sub)),
        in_specs=[pl.BlockSpec((4, HEAD), lambda i, h, j: (0, 0)), qspec, kspec, vspec,
                  pl.BlockSpec((1, hw), lambda i, h, j: (0, 0))],
        out_specs=qspec,
        out_shape=jax.ShapeDtypeStruct((b, t, d), BF16),
        compiler_params=_cparams(("parallel", "parallel", "parallel"), vmem),
        name="diff_attention",
    )(lam_params, q, k_all, vt_all, sub_gain.reshape(1, hw))


def _rope_tables(n_tokens):
    n_freq = HEAD // 4
    pos = jnp.arange(n_tokens)
    freqs = ROPE_BASE ** (-jnp.arange(n_freq, dtype=F32) / n_freq)
    ang_row = (pos // GRID_W).astype(F32)[:, None] * freqs
    ang_col = (pos % GRID_W).astype(F32)[:, None] * freqs
    cos = jnp.concatenate([jnp.cos(ang_row)] * 2 + [jnp.cos(ang_col)] * 2, axis=-1)
    sin = jnp.concatenate([-jnp.sin(ang_row), jnp.sin(ang_row), -jnp.sin(ang_col), jnp.sin(ang_col)], axis=-1)
    reps = GROUP // HEAD
    return jnp.tile(cos, (1, reps)), jnp.tile(sin, (1, reps))


def _mod_rows(mod, layer, j, d, batch):
    m = mod[layer, :, j * d:(j + 1) * d]
    return m[:batch, None, :], m[batch:batch + 1, None, :]


def _rwkv_layer(xs, mods, norm_g, p):
    rkv, lora, outs = [], [], []
    for x, (sh, sc, _) in zip(xs, mods):
        h = _norm_mod(x, norm_g, sh, sc, F32)
        rkv.append(_rwkv_rkv(h, p["mu"], p["wr"], p["wk"], p["wv"]))
        lora.append(_rwkv_lora(h, p["mu"], p["w0"], p["w1"], p["w2"], p["a0"], p["a1"], p["a2"], p["g1"], p["g2"]))
    b, _, d = xs[0].shape
    state = jnp.zeros((b, 2, d // GROUP, GROUP, GROUP), F32)
    for x, (_, _, gate), (r, k, v), (g, lw0, lw1, as0, as1) in zip(xs, mods, rkv, lora):
        y0, y1, state = _rwkv_scan(r, k, v, lw0, as0, lw1, as1, p["kk"], p["ka"], state)
        z = _rwkv_readout(y0, y1, r, k, v, as0, as1, g, p["ka"], p["rk"], p["lnw"], p["lnb"])
        outs.append(_matmul(z, p["wo"], res=x, gate=gate))
    return outs


def kernel(x, c, ctx, c_ctx, mod_w, mod_b, norm1_g, norm2_g, rwkv_mu, rwkv_wr, rwkv_wk, rwkv_wv, rwkv_wo, rwkv_w0, rwkv_w1, rwkv_w2, rwkv_a0, rwkv_a1, rwkv_a2, rwkv_g1, rwkv_g2, rwkv_kk, rwkv_ka, rwkv_rk, rwkv_lnw, rwkv_lnb, diff_wqkv, diff_qn, diff_kn, diff_lq1, diff_lk1, diff_lq2, diff_lk2, diff_subln, diff_wo, ffn_wg, ffn_wu, ffn_wd):
    batch, n_lat, d = x.shape
    depth = mod_w.shape[0]
    assert depth == 2, "layer 0 is the RWKV-7 mixer, layer 1 differential attention"

    cvecs = jnp.zeros((SUBLANES, d), F32).at[:batch].set(c).at[batch].set(c_ctx)
    mod = _adaln(cvecs, mod_w, mod_b)

    def mods(layer, first):
        lat, cx = zip(*[_mod_rows(mod, layer, first + j, d, batch) for j in range(3)])
        return cx, lat

    w1, w2 = _pad_lora(rwkv_w1[0], rwkv_w2[0])
    a1, a2 = _pad_lora(rwkv_a1[0], rwkv_a2[0])
    g1, g2 = _pad_lora(rwkv_g1[0], rwkv_g2[0])
    p = dict(mu=rwkv_mu[0], wr=rwkv_wr[0].astype(BF16), wk=rwkv_wk[0].astype(BF16), wv=rwkv_wv[0].astype(BF16),
             wo=rwkv_wo[0].astype(BF16), w0=rwkv_w0[0], w1=w1, w2=w2, a0=rwkv_a0[0], a1=a1, a2=a2,
             g1=g1, g2=g2, kk=rwkv_kk[0], ka=rwkv_ka[0],
             rk=rwkv_rk[0], lnw=rwkv_lnw[0], lnb=rwkv_lnb[0])
    xc, xl = _rwkv_layer((ctx, x), mods(0, 0), norm1_g[0], p)
    wg, wu, wd = ffn_wg[0].astype(BF16), ffn_wu[0].astype(BF16), ffn_wd[0].astype(BF16)
    (csh, csc, cgt), (lsh, lsc, lgt) = mods(0, 3)
    xc = _ffn(xc, norm2_g[0], csh, csc, cgt, wg, wu, wd)
    xl = _ffn(xl, norm2_g[0], lsh, lsc, lgt, wg, wu, wd)

    (csh, csc, _), (lsh, lsc, lgt) = mods(1, 0)
    lambda_init = 0.8 - 0.6 * math.exp(-0.3 * 1)
    wqkv = diff_wqkv[0].astype(BF16)
    hl = _norm_mod(xl, norm1_g[1], lsh, lsc, BF16)
    hc = _norm_mod(xc, norm1_g[1], csh, csc, BF16)
    qn = jnp.tile(diff_qn[0], d // HEAD).reshape(1, d)
    kn = jnp.tile(diff_kn[0], d // HEAD).reshape(1, d)
    cos, sin = _rope_tables(n_lat)
    q_l, k_l, v_l = _qkv_prep(_matmul(hl, wqkv), qn, kn, cos, sin, with_q=True)
    k_c, v_c = _qkv_prep(_matmul(hc, wqkv[:, d:]), None, kn, None, None, with_q=False)
    k_all = jnp.concatenate([k_c, k_l], axis=1)
    v_all = jnp.concatenate([v_c, v_l], axis=2)
    lam_params = jnp.stack([diff_lq1[0], diff_lk1[0], diff_lq2[0], diff_lk2[0]])
    o = _diff_attention(q_l, k_all, v_all, lam_params, diff_subln[0], lambda_init)
    xl = _matmul(o, diff_wo[0].astype(BF16), res=xl, gate=lgt)
    (_, _, _), (lsh, lsc, lgt) = mods(1, 3)
    return _ffn(xl, norm2_g[1], lsh, lsc, lgt,
                ffn_wg[1].astype(BF16), ffn_wu[1].astype(BF16), ffn_wd[1].astype(BF16))
```

```python
import functools
import math

import jax
import jax.numpy as jnp
from jax import lax
from jax.experimental import pallas as pl
from jax.experimental.pallas import tpu as pltpu

F32 = jnp.float32
BF16 = jnp.bfloat16

HEAD = 64
GROUP = 256
CHUNK = 64
SCAN_UNROLL = 8
ATTN_CHAINS = 4
FFN_ACC_COLS = 512
SUBLANES = 8
V7X_VMEM_CAP = 56 * 1024 * 1024

NORM_EPS = 1e-6
GN_EPS = 64e-5
L2_EPS = 1e-24
QK_EPS = 1e-6
SUBLN_EPS = 1e-5
ROPE_BASE = 10000.0
GRID_W = 64
DECAY_SCALE = math.exp(-0.5)
LOG2E = math.log2(math.e)
SCORE_MARGIN = 1.01
MAX_FIXED_SHIFT = 50.0


def _cparams(semantics, vmem_bytes):
    return pltpu.CompilerParams(dimension_semantics=semantics,
                                vmem_limit_bytes=int(min(max(vmem_bytes, 16 << 20), V7X_VMEM_CAP)))


def _dot(a, b):
    return jnp.dot(a, b, preferred_element_type=F32)


def _dot_nt(a, b):
    return lax.dot_general(a, b, (((1,), (1,)), ((), ())), preferred_element_type=F32)


def _dot_tn(a, b):
    return lax.dot_general(a, b, (((0,), (0,)), ((), ())), preferred_element_type=F32)


def _split_dot(x, w_bf16):
    hi = x.astype(BF16)
    lo = (x - hi.astype(F32)).astype(BF16)
    return _dot(hi, w_bf16) + _dot(lo, w_bf16)


def _block_mask(n):
    r = lax.broadcasted_iota(jnp.int32, (n, n), 0) // HEAD
    c = lax.broadcasted_iota(jnp.int32, (n, n), 1) // HEAD
    return r == c


def _adaln_kernel(c_ref, w_ref, b_ref, o_ref):
    c = c_ref[...]
    s = (c * jax.nn.sigmoid(c)).astype(BF16)
    o_ref[...] = _dot(s, w_ref[...].astype(BF16)) + b_ref[...]


def _adaln(cvecs, mod_w, mod_b):
    depth, d, n = mod_w.shape
    rows = cvecs.shape[0]
    tn = 1024
    return pl.pallas_call(
        _adaln_kernel,
        grid=(depth, n // tn),
        in_specs=[pl.BlockSpec((rows, d), lambda l, j: (0, 0)),
                  pl.BlockSpec((None, d, tn), lambda l, j: (l, 0, j)),
                  pl.BlockSpec((None, 1, tn), lambda l, j: (l, 0, j))],
        out_specs=pl.BlockSpec((None, rows, tn), lambda l, j: (l, 0, j)),
        out_shape=jax.ShapeDtypeStruct((depth, rows, n), F32),
        compiler_params=_cparams(("parallel", "parallel"), 3 * d * tn * 4 + (8 << 20)),
        name="adaln",
    )(cvecs, mod_w, mod_b.reshape(depth, 1, n))


def _norm_mod_value(x, g, sh, sc):
    y = x * lax.rsqrt(jnp.mean(x * x, axis=-1, keepdims=True) + NORM_EPS)
    return (y * g) * (1.0 + sc) + sh


def _norm_mod_kernel(x_ref, g_ref, sh_ref, sc_ref, o_ref):
    o_ref[...] = _norm_mod_value(x_ref[...], g_ref[...], sh_ref[...], sc_ref[...]).astype(o_ref.dtype)


def _row_spec(arr, d):
    if arr.shape[0] == 1:
        return pl.BlockSpec((None, 1, d), lambda b, *_: (0, 0, 0))
    return pl.BlockSpec((None, 1, d), lambda b, *_: (b, 0, 0))


def _norm_mod(x, g, sh, sc, out_dtype):
    b, t, d = x.shape
    tr = min(t, 512)
    return pl.pallas_call(
        _norm_mod_kernel,
        grid=(b, t // tr),
        in_specs=[pl.BlockSpec((None, tr, d), lambda i, j: (i, j, 0)),
                  pl.BlockSpec((1, d), lambda i, j: (0, 0)),
                  _row_spec(sh, d), _row_spec(sc, d)],
        out_specs=pl.BlockSpec((None, tr, d), lambda i, j: (i, j, 0)),
        out_shape=jax.ShapeDtypeStruct((b, t, d), out_dtype),
        compiler_params=_cparams(("parallel", "parallel"), 6 * tr * d * 4),
        name="norm_mod",
    )(x, g.reshape(1, d), sh, sc)


def _shift_delta(h, prev_ref, next_ref, t, nt):
    tm = h.shape[0]
    row = lax.broadcasted_iota(jnp.int32, h.shape, 0)
    prev_row = jnp.where(t > 0, prev_ref[SUBLANES - 1:SUBLANES, :], 0.0)
    next_row = jnp.where(t < nt - 1, next_ref[0:1, :], 0.0)
    hp = jnp.where(row == 0, prev_row, pltpu.roll(h, 1, 0))
    hn = jnp.where(row == tm - 1, next_row, pltpu.roll(h, tm - 1, 0))
    return 0.5 * (hp + hn) - h


def _halo_specs(t, tm, d):
    per = tm // SUBLANES
    last = t // SUBLANES - 1
    return [pl.BlockSpec((None, tm, d), lambda b, i, *_: (b, i, 0)),
            pl.BlockSpec((None, SUBLANES, d), lambda b, i, *_: (b, jnp.maximum(i * per - 1, 0), 0)),
            pl.BlockSpec((None, SUBLANES, d), lambda b, i, *_: (b, jnp.minimum((i + 1) * per, last), 0))]


def _rkv_kernel(h_ref, hp_ref, hn_ref, mu_ref, wr_ref, wk_ref, wv_ref, r_ref, k_ref, v_ref, xs_ref, *, nt):
    t = pl.program_id(1)

    @pl.when(pl.program_id(2) == 0)
    def _():
        h = h_ref[...]
        xx = _shift_delta(h, hp_ref, hn_ref, t, nt)
        xs_ref[0] = (h + xx * mu_ref[0:1, :]).astype(BF16)
        xs_ref[1] = (h + xx * mu_ref[2:3, :]).astype(BF16)
        xs_ref[2] = (h + xx * mu_ref[3:4, :]).astype(BF16)

    r_ref[...] = _dot(xs_ref[0], wr_ref[...])
    k_ref[...] = _dot(xs_ref[1], wk_ref[...])
    v_ref[...] = _dot(xs_ref[2], wv_ref[...])


def _rwkv_rkv(h, mu, wr, wk, wv):
    b, t, d = h.shape
    tm = min(t, 512)
    tn = 512
    nt = t // tm
    out = jax.ShapeDtypeStruct((b, t, d), F32)
    wspec = pl.BlockSpec((d, tn), lambda i, j, n: (0, n))
    ospec = pl.BlockSpec((None, tm, tn), lambda i, j, n: (i, j, n))
    vmem = 2 * tm * d * 4 + 3 * tm * d * 2 + 6 * d * tn * 2 + 6 * tm * tn * 4 + 3 * tm * d * 4 + (4 << 20)
    return pl.pallas_call(
        functools.partial(_rkv_kernel, nt=nt),
        grid=(b, nt, d // tn),
        in_specs=_halo_specs(t, tm, d) + [pl.BlockSpec((6, d), lambda i, j, n: (0, 0)), wspec, wspec, wspec],
        out_specs=[ospec, ospec, ospec],
        out_shape=[out, out, out],
        scratch_shapes=[pltpu.VMEM((3, tm, d), BF16)],
        compiler_params=_cparams(("parallel", "parallel", "arbitrary"), vmem),
        name="rwkv_rkv",
    )(h, h, h, mu, wr, wk, wv)


def _lora_kernel(h_ref, hp_ref, hn_ref, mu_ref, w0_ref, w1_ref, w2_ref, a0_ref, a1_ref, a2_ref, g1_ref, g2_ref,
                 g_ref, lw0_ref, lw1_ref, as0_ref, as1_ref, *, nt):
    t = pl.program_id(1)
    h = h_ref[...]
    xx = _shift_delta(h, hp_ref, hn_ref, t, nt)
    xw = (h + xx * mu_ref[1:2, :]).astype(BF16)
    xa = (h + xx * mu_ref[4:5, :]).astype(BF16)
    xg = (h + xx * mu_ref[5:6, :]).astype(BF16)
    g_ref[...] = _dot(jax.nn.sigmoid(_dot(xg, g1_ref[...])).astype(BF16), g2_ref[...])
    for d, (lw_ref, as_ref) in enumerate(((lw0_ref, as0_ref), (lw1_ref, as1_ref))):
        w_pre = w0_ref[d:d + 1, :] + _dot(jnp.tanh(_dot(xw, w1_ref[d])).astype(BF16), w2_ref[d])
        lw_ref[...] = -DECAY_SCALE * jax.nn.sigmoid(w_pre)
        a_pre = a0_ref[d:d + 1, :] + _dot(_dot(xa, a1_ref[d]).astype(BF16), a2_ref[d])
        as_ref[...] = jax.nn.sigmoid(a_pre)


def _pad_lora(w_in, w_out):
    rank = w_in.shape[-1]
    pad = (-rank) % 128
    w_in = jnp.pad(w_in, [(0, 0)] * (w_in.ndim - 1) + [(0, pad)])
    w_out = jnp.pad(w_out, [(0, 0)] * (w_out.ndim - 2) + [(0, pad), (0, 0)])
    return w_in.astype(BF16), w_out.astype(BF16)


def _rwkv_lora(h, mu, w0, w1, w2, a0, a1, a2, g1, g2):
    b, t, d = h.shape
    tm = min(t, 256)
    nt = t // tm
    out = jax.ShapeDtypeStruct((b, t, d), F32)
    ospec = pl.BlockSpec((None, tm, d), lambda i, j: (i, j, 0))

    def full(a):
        return pl.BlockSpec(a.shape, lambda i, j, _n=a.ndim: (0,) * _n)

    params = (mu, w0, w1, w2, a0, a1, a2, g1, g2)
    vmem = 2 * 6 * tm * d * 4 + 10 * tm * d * 4 + 2 * sum(p.size * p.dtype.itemsize for p in params)
    return pl.pallas_call(
        functools.partial(_lora_kernel, nt=nt),
        grid=(b, nt),
        in_specs=_halo_specs(t, tm, d) + [full(p) for p in params],
        out_specs=[ospec] * 5,
        out_shape=[out] * 5,
        compiler_params=_cparams(("parallel", "parallel"), vmem),
        name="rwkv_lora",
    )(h, h, h, *params)


def _split_dot_left(w_bf16, x):
    hi = x.astype(BF16)
    lo = (x - hi.astype(F32)).astype(BF16)
    return _dot(w_bf16, hi) + _dot(w_bf16, lo)


def _scan_chain(d, refs, kks, ka, state_ref, y_ref, sl, g, consts):
    r_ref, k_ref, v_ref, lw_ref, as_ref = refs
    bmask, ones_bd, tri, m_strict, m_incl = consts
    c = CHUNK

    def stack(x):
        return jnp.where(bmask, jnp.concatenate([x] * (GROUP // c), axis=0), 0.0).astype(BF16)

    lw = lw_ref[:, sl]
    a_sig = as_ref[:, sl]
    k = k_ref[:, sl]
    kk = k * kks
    ssq = _split_dot(kk * kk, ones_bd)
    cum = _split_dot_left(tri[d], lw)
    yield

    kk = kk * lax.rsqrt(jnp.maximum(ssq, L2_EPS))
    b_vec = kk * a_sig
    k_d = k * (1.0 + (a_sig - 1.0) * ka)
    tot = cum[c - 1:c, :] if d == 0 else cum[0:1, :]
    a_t = -kk * jnp.exp(cum - lw)
    r_t = r_ref[:, sl] * jnp.exp(cum)
    p_inv = jnp.exp(-cum)
    lhs = jnp.concatenate([a_t, r_t], axis=0).astype(BF16)
    rhs = jnp.concatenate([stack(b_vec * p_inv), stack(k_d * p_inv)], axis=0)
    aa = _dot_nt(lhs, rhs)
    yield

    n1 = jnp.where(m_strict[d], aa[:c, :GROUP], 0.0)
    a_ak = jnp.where(m_strict[d], aa[:c, GROUP:], 0.0)
    a_rb = jnp.where(m_incl[d], aa[c:, :GROUP], 0.0).astype(BF16)
    a_rk = jnp.where(m_incl[d], aa[c:, GROUP:], 0.0)
    v = v_ref[:, sl]
    z1 = _dot(jnp.concatenate([a_ak, a_rk], axis=0).astype(BF16), stack(v))
    q = n1
    npow = _dot(n1.astype(BF16), stack(n1))
    yield
    span = 2
    while 2 * span < c:
        z = _dot(jnp.concatenate([npow, q], axis=0).astype(BF16), stack(npow))
        yield
        q = q + npow + z[c:]
        npow = z[:c]
        span *= 2
    zq = _dot(q.astype(BF16), stack(npow))
    yield
    q = q + npow + zq

    v_ak = z1[:c]
    y0 = z1[c:]
    z2 = _dot(q.astype(BF16), jnp.concatenate([stack(a_t), stack(v_ak)], axis=1))
    yield
    w = a_t + z2[:, :GROUP]
    u0 = v_ak + z2[:, GROUP:]

    s = state_ref[d, g]
    z3 = _dot_nt(jnp.concatenate([w, r_t], axis=0).astype(BF16), s.astype(BF16))
    yield
    u = z3[:c] + u0
    p_end = jnp.exp(tot - cum)
    yu = _dot(a_rb, stack(u))
    upd = _dot_tn(jnp.concatenate([u, v], axis=0).astype(BF16),
                  jnp.concatenate([b_vec * p_end, k_d * p_end], axis=0).astype(BF16))
    yield
    y_ref[:, sl] = z3[c:] + yu + y0
    state_ref[d, g] = s * jnp.exp(tot) + jnp.where(bmask, upd, 0.0)


def _interleave(chains):
    chains = list(chains)
    while chains:
        alive = []
        for ch in chains:
            try:
                next(ch)
                alive.append(ch)
            except StopIteration:
                pass
        chains = alive


def _scan_kernel(r0, k0, v0, lw0, as0, r1, k1, v1, lw1, as1, kks_ref, ka_ref, sin_ref,
                 y0_ref, y1_ref, state_ref, *, groups):
    i = pl.program_id(1)
    c = CHUNK
    unroll = math.gcd(groups, SCAN_UNROLL)

    @pl.when(i == 0)
    def _():
        state_ref[...] = sin_ref[...]

    bmask = _block_mask(GROUP)
    ones_bd = jnp.where(bmask, 1.0, 0.0).astype(BF16)
    tt = lax.broadcasted_iota(jnp.int32, (c, c), 0)
    ss = lax.broadcasted_iota(jnp.int32, (c, c), 1)
    tri = (jnp.where(ss <= tt, 1.0, 0.0).astype(BF16), jnp.where(ss >= tt, 1.0, 0.0).astype(BF16))
    tp = lax.broadcasted_iota(jnp.int32, (c, GROUP), 0)
    sp = lax.broadcasted_iota(jnp.int32, (c, GROUP), 1) % c
    m_strict = (sp < tp, sp > tp)
    m_incl = (sp <= tp, sp >= tp)
    consts = (bmask, ones_bd, tri, m_strict, m_incl)

    def body(it, carry):
        chains = []
        for j in range(unroll):
            g = it * unroll + j
            sl = pl.ds(pl.multiple_of(g * GROUP, GROUP), GROUP)
            kks = kks_ref[:, sl]
            ka = ka_ref[:, sl]
            chains.append(_scan_chain(0, (r0, k0, v0, lw0, as0), kks, ka, state_ref, y0_ref, sl, g, consts))
            chains.append(_scan_chain(1, (r1, k1, v1, lw1, as1), kks, ka, state_ref, y1_ref, sl, g, consts))
        _interleave(chains)
        return carry

    lax.fori_loop(0, groups // unroll, body, 0)


def _rwkv_scan(r, k, v, lw0, as0, lw1, as1, kk_scale, ka, s_init):
    b, t, d = r.shape
    assert CHUNK * (GROUP // HEAD) == GROUP and t % CHUNK == 0 and d % GROUP == 0
    groups = d // GROUP
    nc = t // CHUNK
    fwd = pl.BlockSpec((None, CHUNK, d), lambda i, j: (i, j, 0))
    bwd = pl.BlockSpec((None, CHUNK, d), lambda i, j: (i, nc - 1 - j, 0))
    vec = pl.BlockSpec((1, d), lambda i, j: (0, 0))
    sspec = pl.BlockSpec((None, 2, groups, GROUP, GROUP), lambda i, j: (i, 0, 0, 0, 0))
    y = jax.ShapeDtypeStruct((b, t, d), F32)
    state_bytes = 2 * groups * GROUP * GROUP * 4
    vmem = 2 * 12 * CHUNK * d * 4 + 4 * state_bytes + 2 * SCAN_UNROLL * (3 << 19) + (4 << 20)
    return pl.pallas_call(
        functools.partial(_scan_kernel, groups=groups),
        grid=(b, nc),
        in_specs=[fwd] * 5 + [bwd] * 5 + [vec, vec, sspec],
        out_specs=[fwd, bwd, sspec],
        out_shape=[y, y, jax.ShapeDtypeStruct(s_init.shape, F32)],
        compiler_params=_cparams(("parallel", "arbitrary"), vmem),
        name="rwkv_scan",
    )(r, k, v, lw0, as0, r, k, v, lw1, as1, kk_scale.reshape(1, d), ka.reshape(1, d), s_init)


def _readout_kernel(y0_ref, y1_ref, r_ref, k_ref, v_ref, as0_ref, as1_ref, g_ref,
                    ka_ref, rk_ref, lnw_ref, lnb_ref, o_ref):
    d = o_ref.shape[-1]
    ones_bd = jnp.where(_block_mask(GROUP), 1.0, 0.0).astype(BF16)
    for j in range(d // GROUP):
        sl = slice(j * GROUP, (j + 1) * GROUP)
        y = y0_ref[:, sl] + y1_ref[:, sl]
        mean = _split_dot(y, ones_bd) * (1.0 / HEAD)
        yc = y - mean
        var = _split_dot(yc * yc, ones_bd) * (1.0 / HEAD)
        yn = (yc * lax.rsqrt(var + GN_EPS)) * lnw_ref[:, sl] + lnb_ref[:, sl]
        k = k_ref[:, sl]
        ka = ka_ref[:, sl]
        k_sum = k * (1.0 + (as0_ref[:, sl] - 1.0) * ka) + k * (1.0 + (as1_ref[:, sl] - 1.0) * ka)
        bonus = _split_dot(r_ref[:, sl] * k_sum * rk_ref[:, sl], ones_bd) * v_ref[:, sl]
        o_ref[:, sl] = ((yn + bonus) * g_ref[:, sl]).astype(o_ref.dtype)


def _rwkv_readout(y0, y1, r, k, v, as0, as1, g, ka, rk, lnw, lnb):
    b, t, d = r.shape
    tm = min(t, 256)
    spec = pl.BlockSpec((None, tm, d), lambda i, j: (i, j, 0))
    vec = pl.BlockSpec((1, d), lambda i, j: (0, 0))
    return pl.pallas_call(
        _readout_kernel,
        grid=(b, t // tm),
        in_specs=[spec] * 8 + [vec] * 4,
        out_specs=spec,
        out_shape=jax.ShapeDtypeStruct((b, t, d), BF16),
        compiler_params=_cparams(("parallel", "parallel"), 2 * 9 * tm * d * 4 + (8 << 20)),
        name="rwkv_readout",
    )(y0, y1, r, k, v, as0, as1, g, ka.reshape(1, d), rk.reshape(1, d), lnw.reshape(1, d), lnb.reshape(1, d))


def _mm_kernel(z_ref, w_ref, o_ref):
    o_ref[...] = _dot(z_ref[...], w_ref[...]).astype(o_ref.dtype)


def _mm_res_kernel(z_ref, w_ref, res_ref, gate_ref, o_ref):
    o_ref[...] = res_ref[...] + gate_ref[...] * _dot(z_ref[...], w_ref[...])


def _matmul(z, w, res=None, gate=None, out_dtype=F32):
    b, t, kdim = z.shape
    n = w.shape[1]
    tm = min(t, 1024)
    tn = 512
    zspec = pl.BlockSpec((None, tm, kdim), lambda i, j, l: (i, j, 0))
    wspec = pl.BlockSpec((kdim, tn), lambda i, j, l: (0, l))
    ospec = pl.BlockSpec((None, tm, tn), lambda i, j, l: (i, j, l))
    vmem = 2 * (tm * kdim * 2 + kdim * tn * 2 + 2 * tm * tn * 4) + tm * tn * 4 + (4 << 20)
    common = dict(grid=(b, t // tm, n // tn), out_specs=ospec,
                  compiler_params=_cparams(("parallel", "parallel", "parallel"), vmem))
    if res is None:
        return pl.pallas_call(_mm_kernel, in_specs=[zspec, wspec],
                              out_shape=jax.ShapeDtypeStruct((b, t, n), out_dtype), name="matmul", **common)(z, w)
    gspec = (pl.BlockSpec((None, 1, tn), lambda i, j, l: (0, 0, l)) if gate.shape[0] == 1
             else pl.BlockSpec((None, 1, tn), lambda i, j, l: (i, 0, l)))
    return pl.pallas_call(_mm_res_kernel, in_specs=[zspec, wspec, ospec, gspec],
                          out_shape=jax.ShapeDtypeStruct((b, t, n), F32), name="matmul_res", **common)(z, w, res, gate)


def _ffn_kernel(x_ref, g_ref, sh_ref, sc_ref, gate_ref, wg_ref, wu_ref, wd_ref, o_ref, h_ref):
    f = pl.program_id(2)

    @pl.when(f == 0)
    def _():
        h_ref[...] = _norm_mod_value(x_ref[...], g_ref[...], sh_ref[...], sc_ref[...]).astype(BF16)
        o_ref[...] = jnp.zeros_like(o_ref)

    h = h_ref[...]
    gp = _dot(h, wg_ref[...])
    act = ((gp * jax.nn.sigmoid(gp)) * _dot(h, wu_ref[...])).astype(BF16)
    d = o_ref.shape[-1]
    for c0 in range(0, d, FFN_ACC_COLS):
        cols = slice(c0, c0 + FFN_ACC_COLS)
        o_ref[:, cols] += _dot(act, wd_ref[:, cols])

    @pl.when(f == pl.num_programs(2) - 1)
    def _():
        o_ref[...] = x_ref[...] + gate_ref[...] * o_ref[...]


def _ffn(x, g, sh, sc, gate, wg, wu, wd):
    b, t, d = x.shape
    ff = wg.shape[1]
    tm = min(t, 512)
    tf = 512
    xspec = pl.BlockSpec((None, tm, d), lambda i, j, f: (i, j, 0))
    vmem = 4 * tm * d * 4 + tm * d * 2 + 2 * 3 * d * tf * 2 + 6 * tm * tf * 4 + tm * d * 4 + (4 << 20)
    return pl.pallas_call(
        _ffn_kernel,
        grid=(b, t // tm, ff // tf),
        in_specs=[xspec, pl.BlockSpec((1, d), lambda i, j, f: (0, 0)),
                  _row_spec(sh, d), _row_spec(sc, d), _row_spec(gate, d),
                  pl.BlockSpec((d, tf), lambda i, j, f: (0, f)),
                  pl.BlockSpec((d, tf), lambda i, j, f: (0, f)),
                  pl.BlockSpec((tf, d), lambda i, j, f: (f, 0))],
        out_specs=xspec,
        out_shape=jax.ShapeDtypeStruct((b, t, d), F32),
        scratch_shapes=[pltpu.VMEM((tm, d), BF16)],
        compiler_params=_cparams(("parallel", "parallel", "arbitrary"), vmem),
        name="ffn",
    )(x, g.reshape(1, d), sh, sc, gate, wg, wu, wd)


def _qk_norm(x, gain, ones_bd):
    ms = _split_dot(x * x, ones_bd) * (1.0 / HEAD)
    return x * lax.rsqrt(ms + QK_EPS) * gain


def _rope(u, cos, sin):
    n = u.shape[-1]
    lane = lax.broadcasted_iota(jnp.int32, u.shape, 1)
    partner = jnp.where(lane % 32 < 16, pltpu.roll(u, n - 16, 1), pltpu.roll(u, 16, 1))
    return u * cos + partner * sin


def _qkv_prep_kernel(*refs, rope, with_q):
    refs = list(refs)
    q_ref = refs.pop(0) if with_q else None
    k_ref, v_ref = refs.pop(0), refs.pop(0)
    qn_ref = refs.pop(0) if with_q else None
    kn_ref = refs.pop(0)
    if rope:
        cos_ref, sin_ref = refs.pop(0), refs.pop(0)
    qo_ref = refs.pop(0) if with_q else None
    ko_ref, vo_ref = refs
    d = ko_ref.shape[-1]
    ones_bd = jnp.where(_block_mask(GROUP), 1.0, 0.0).astype(BF16)
    if rope:
        cos, sin = cos_ref[...], sin_ref[...]
    for j in range(d // GROUP):
        sl = slice(j * GROUP, (j + 1) * GROUP)
        kh = _qk_norm(k_ref[:, sl], kn_ref[:, sl], ones_bd)
        if rope:
            kh = _rope(kh, cos, sin)
        ko_ref[:, sl] = kh.astype(BF16)
        if with_q:
            qh = _qk_norm(q_ref[:, sl], qn_ref[:, sl], ones_bd)
            if rope:
                qh = _rope(qh, cos, sin)
            qo_ref[:, sl] = (qh * (HEAD ** -0.5 * LOG2E)).astype(BF16)
    vo_ref[...] = v_ref[...].T.astype(BF16)


def _qkv_prep(qkv, qn, kn, cos, sin, *, with_q):
    b, t, width = qkv.shape
    parts = 3 if with_q else 2
    d = width // parts
    rope = cos is not None
    tm = min(t, 256)
    col = lambda c: pl.BlockSpec((None, tm, d), lambda i, j, _c=c: (i, j, _c))
    vec = pl.BlockSpec((1, d), lambda i, j: (0, 0))
    tab = pl.BlockSpec((tm, GROUP), lambda i, j: (j, 0))
    in_specs = [col(c) for c in range(parts)] + [vec] * (2 if with_q else 1) + ([tab, tab] if rope else [])
    args = [qkv] * parts + ([qn] if with_q else []) + [kn] + ([cos, sin] if rope else [])
    out = jax.ShapeDtypeStruct((b, t, d), BF16)
    ospec = pl.BlockSpec((None, tm, d), lambda i, j: (i, j, 0))
    out_t = jax.ShapeDtypeStruct((b, d, t), BF16)
    ospec_t = pl.BlockSpec((None, d, tm), lambda i, j: (i, 0, j))
    return pl.pallas_call(
        functools.partial(_qkv_prep_kernel, rope=rope, with_q=with_q),
        grid=(b, t // tm),
        in_specs=in_specs,
        out_specs=[ospec] * (parts - 1) + [ospec_t],
        out_shape=[out] * (parts - 1) + [out_t],
        compiler_params=_cparams(("parallel", "parallel"), 2 * parts * tm * d * 6 + (16 << 20)),
        name="qkv_prep",
    )(*args)


def _attn_chain(q2, k, vt, state, shift):
    m, acc_t = state
    s_t = _dot_nt(k, q2)
    yield
    if shift is None:
        m_new = jnp.maximum(m, jnp.max(s_t, axis=0, keepdims=True))
        alpha = jnp.exp2(m - m_new)
        state[0] = m_new
        p = jnp.exp2(s_t - m_new)
    else:
        p = jnp.exp2(s_t - shift)
    pv = _dot(vt, p.astype(BF16))
    yield
    state[1] = pv + (acc_t if shift is not None else alpha * acc_t)


def _attn_kernel(lam_ref, q_ref, k_ref, vt_ref, sub_ref, o_ref, *, tk, tq, lambda_init):
    n_sub = q_ref.shape[0] // tq
    tkeys = k_ref.shape[0]
    first = tkeys % tk or tk
    lp = lam_ref[...]
    lam = (jnp.exp(jnp.sum(lp[0:1] * lp[1:2], axis=-1, keepdims=True))
           - jnp.exp(jnp.sum(lp[2:3] * lp[3:4], axis=-1, keepdims=True)) + lambda_init)
    bound = (HEAD ** 0.5 * LOG2E * SCORE_MARGIN) * (jnp.max(jnp.abs(lp[4:5]), axis=-1, keepdims=True)
                                                    * jnp.max(jnp.abs(lp[5:6]), axis=-1, keepdims=True))
    fixed_shift_ok = bound[0, 0] <= MAX_FIXED_SHIFT
    q2s = []
    for c in range(n_sub):
        q = q_ref[c * tq:(c + 1) * tq, :]
        lane = lax.broadcasted_iota(jnp.int32, q.shape, 1)
        zero = jnp.zeros_like(q)
        q2s.append(jnp.concatenate([jnp.where(lane < HEAD, q, zero), jnp.where(lane >= HEAD, q, zero)], axis=0))

    hw = 2 * HEAD
    ones_rows = 16

    def run(shift):
        def step(ks, nkeys, carry):
            k = k_ref[ks, :]
            vt = jnp.concatenate([vt_ref[:, ks], jnp.ones((ones_rows, nkeys), BF16)], axis=0)
            states = [list(st) for st in carry]
            _interleave([_attn_chain(q2, k, vt, st, shift) for q2, st in zip(q2s, states)])
            return tuple(tuple(st) for st in states)

        init = (jnp.full((1, 2 * tq), -jnp.inf, F32), jnp.zeros((hw + ones_rows, 2 * tq), F32))
        carry = step(slice(0, first), first, (init,) * n_sub)
        carry = lax.fori_loop(
            0, (tkeys - first) // tk,
            lambda j, cr: step(pl.ds(pl.multiple_of(first + j * tk, first), tk), tk, cr), carry)
        for c, (_, acc) in enumerate(carry):
            acc_t, l = acc[:hw], acc[hw:hw + 1]
            o_t = acc_t[:, :tq] / l[:, :tq] - lam * (acc_t[:, tq:] / l[:, tq:])
            o = o_t.T
            o = o * lax.rsqrt(jnp.mean(o * o, axis=-1, keepdims=True) + SUBLN_EPS) * sub_ref[...]
            o_ref[c * tq:(c + 1) * tq, :] = (o * (1.0 - lambda_init)).astype(o_ref.dtype)

    @pl.when(fixed_shift_ok)
    def _():
        run(bound)

    @pl.when(jnp.logical_not(fixed_shift_ok))
    def _():
        run(None)


def _diff_attention(q, k_all, vt_all, lam_params, sub_gain, lambda_init):
    b, t, d = q.shape
    tkeys = k_all.shape[1]
    hw = 2 * HEAD
    tq = 256
    n_sub = math.gcd(t // tq, ATTN_CHAINS)
    tk = 512
    assert t % (tq * n_sub) == 0 and (tkeys % tk) % 128 == 0
    qspec = pl.BlockSpec((None, tq * n_sub, hw), lambda i, h, j: (i, j, h))
    kspec = pl.BlockSpec((None, tkeys, hw), lambda i, h, j: (i, 0, h))
    vspec = pl.BlockSpec((None, hw, tkeys), lambda i, h, j: (i, h, 0))
    vmem = 4 * tkeys * hw * 2 + 4 * n_sub * tq * hw * 2 + n_sub * 8 * 2 * tq * tk * 4 + (8 << 20)
    return pl.pallas_call(
        functools.partial(_attn_kernel, tk=tk, tq=tq, lambda_init=lambda_init),
        grid=(b, d // hw, t // (tq * n_sub)),
        in_specs=[pl.BlockSpec(lam_params.shape, lambda i, h, j: (0, 0)), qspec, kspec, vspec,
                  pl.BlockSpec((1, hw), lambda i, h, j: (0, 0))],
        out_specs=qspec,
        out_shape=jax.ShapeDtypeStruct((b, t, d), BF16),
        compiler_params=_cparams(("parallel", "parallel", "parallel"), vmem),
        name="diff_attention",
    )(lam_params, q, k_all, vt_all, sub_gain.reshape(1, hw))


def _rope_tables(n_tokens):
    n_freq = HEAD // 4
    pos = jnp.arange(n_tokens)
    freqs = ROPE_BASE ** (-jnp.arange(n_freq, dtype=F32) / n_freq)
    ang_row = (pos // GRID_W).astype(F32)[:, None] * freqs
    ang_col = (pos % GRID_W).astype(F32)[:, None] * freqs
    cos = jnp.concatenate([jnp.cos(ang_row)] * 2 + [jnp.cos(ang_col)] * 2, axis=-1)
    sin = jnp.concatenate([-jnp.sin(ang_row), jnp.sin(ang_row), -jnp.sin(ang_col), jnp.sin(ang_col)], axis=-1)
    reps = GROUP // HEAD
    return jnp.tile(cos, (1, reps)), jnp.tile(sin, (1, reps))


def _mod_rows(mod, layer, j, d, batch):
    m = mod[layer, :, j * d:(j + 1) * d]
    return m[:batch, None, :], m[batch:batch + 1, None, :]


def _rwkv_layer(xs, mods, norm_g, p):
    rkv, lora, outs = [], [], []
    for x, (sh, sc, _) in zip(xs, mods):
        h = _norm_mod(x, norm_g, sh, sc, F32)
        rkv.append(_rwkv_rkv(h, p["mu"], p["wr"], p["wk"], p["wv"]))
        lora.append(_rwkv_lora(h, p["mu"], p["w0"], p["w1"], p["w2"], p["a0"], p["a1"], p["a2"], p["g1"], p["g2"]))
    b, _, d = xs[0].shape
    state = jnp.zeros((b, 2, d // GROUP, GROUP, GROUP), F32)
    for x, (_, _, gate), (r, k, v), (g, lw0, lw1, as0, as1) in zip(xs, mods, rkv, lora):
        y0, y1, state = _rwkv_scan(r, k, v, lw0, as0, lw1, as1, p["kk"], p["ka"], state)
        z = _rwkv_readout(y0, y1, r, k, v, as0, as1, g, p["ka"], p["rk"], p["lnw"], p["lnb"])
        outs.append(_matmul(z, p["wo"], res=x, gate=gate))
    return outs


def kernel(x, c, ctx, c_ctx, mod_w, mod_b, norm1_g, norm2_g, rwkv_mu, rwkv_wr, rwkv_wk, rwkv_wv, rwkv_wo, rwkv_w0, rwkv_w1, rwkv_w2, rwkv_a0, rwkv_a1, rwkv_a2, rwkv_g1, rwkv_g2, rwkv_kk, rwkv_ka, rwkv_rk, rwkv_lnw, rwkv_lnb, diff_wqkv, diff_qn, diff_kn, diff_lq1, diff_lk1, diff_lq2, diff_lk2, diff_subln, diff_wo, ffn_wg, ffn_wu, ffn_wd):
    batch, n_lat, d = x.shape
    depth = mod_w.shape[0]
    assert depth == 2, "layer 0 is the RWKV-7 mixer, layer 1 differential attention"

    cvecs = jnp.zeros((SUBLANES, d), F32).at[:batch].set(c).at[batch].set(c_ctx)
    mod = _adaln(cvecs, mod_w, mod_b)

    def mods(layer, first):
        lat, cx = zip(*[_mod_rows(mod, layer, first + j, d, batch) for j in range(3)])
        return cx, lat

    w1, w2 = _pad_lora(rwkv_w1[0], rwkv_w2[0])
    a1, a2 = _pad_lora(rwkv_a1[0], rwkv_a2[0])
    g1, g2 = _pad_lora(rwkv_g1[0], rwkv_g2[0])
    p = dict(mu=rwkv_mu[0], wr=rwkv_wr[0].astype(BF16), wk=rwkv_wk[0].astype(BF16), wv=rwkv_wv[0].astype(BF16),
             wo=rwkv_wo[0].astype(BF16), w0=rwkv_w0[0], w1=w1, w2=w2, a0=rwkv_a0[0], a1=a1, a2=a2,
             g1=g1, g2=g2, kk=rwkv_kk[0], ka=rwkv_ka[0],
             rk=rwkv_rk[0], lnw=rwkv_lnw[0], lnb=rwkv_lnb[0])
    xc, xl = _rwkv_layer((ctx, x), mods(0, 0), norm1_g[0], p)
    wg, wu, wd = ffn_wg[0].astype(BF16), ffn_wu[0].astype(BF16), ffn_wd[0].astype(BF16)
    (csh, csc, cgt), (lsh, lsc, lgt) = mods(0, 3)
    xc = _ffn(xc, norm2_g[0], csh, csc, cgt, wg, wu, wd)
    xl = _ffn(xl, norm2_g[0], lsh, lsc, lgt, wg, wu, wd)

    (csh, csc, _), (lsh, lsc, lgt) = mods(1, 0)
    lambda_init = 0.8 - 0.6 * math.exp(-0.3 * 1)
    wqkv = diff_wqkv[0].astype(BF16)
    hl = _norm_mod(xl, norm1_g[1], lsh, lsc, BF16)
    hc = _norm_mod(xc, norm1_g[1], csh, csc, BF16)
    qn = jnp.tile(diff_qn[0], d // HEAD).reshape(1, d)
    kn = jnp.tile(diff_kn[0], d // HEAD).reshape(1, d)
    cos, sin = _rope_tables(n_lat)
    q_l, k_l, v_l = _qkv_prep(_matmul(hl, wqkv), qn, kn, cos, sin, with_q=True)
    k_c, v_c = _qkv_prep(_matmul(hc, wqkv[:, d:]), None, kn, None, None, with_q=False)
    k_all = jnp.concatenate([k_c, k_l], axis=1)
    v_all = jnp.concatenate([v_c, v_l], axis=2)
    lam_params = jnp.stack([diff_lq1[0], diff_lk1[0], diff_lq2[0], diff_lk2[0], diff_qn[0], diff_kn[0]])
    o = _diff_attention(q_l, k_all, v_all, lam_params, diff_subln[0], lambda_init)
    xl = _matmul(o, diff_wo[0].astype(BF16), res=xl, gate=lgt)
    (_, _, _), (lsh, lsc, lgt) = mods(1, 3)
    return _ffn(xl, norm2_g[1], lsh, lsc, lgt,
                ffn_wg[1].astype(BF16), ffn_wu[1].astype(BF16), ffn_wd[1].astype(BF16))
```

```python
import functools
import math

import jax
import jax.numpy as jnp
from jax import lax
from jax.experimental import pallas as pl
from jax.experimental.pallas import tpu as pltpu

F32 = jnp.float32
BF16 = jnp.bfloat16

HEAD = 64
GROUP = 256
CHUNK = 64
SCAN_UNROLL = 8
SCAN_SKEW = 4
ATTN_CHAINS = 4
FFN_ACC_COLS = 512
SUBLANES = 8
V7X_VMEM_CAP = 56 * 1024 * 1024

NORM_EPS = 1e-6
GN_EPS = 64e-5
L2_EPS = 1e-24
QK_EPS = 1e-6
SUBLN_EPS = 1e-5
ROPE_BASE = 10000.0
GRID_W = 64
DECAY_SCALE = math.exp(-0.5)
LOG2E = math.log2(math.e)
SCORE_MARGIN = 1.01
MAX_FIXED_SHIFT = 50.0


def _cparams(semantics, vmem_bytes):
    return pltpu.CompilerParams(dimension_semantics=semantics,
                                vmem_limit_bytes=int(min(max(vmem_bytes, 16 << 20), V7X_VMEM_CAP)))


def _dot(a, b):
    return jnp.dot(a, b, preferred_element_type=F32)


def _dot_nt(a, b):
    return lax.dot_general(a, b, (((1,), (1,)), ((), ())), preferred_element_type=F32)


def _dot_tn(a, b):
    return lax.dot_general(a, b, (((0,), (0,)), ((), ())), preferred_element_type=F32)


def _split_dot(x, w_bf16):
    hi = x.astype(BF16)
    lo = (x - hi.astype(F32)).astype(BF16)
    return _dot(hi, w_bf16) + _dot(lo, w_bf16)


def _block_mask(n):
    r = lax.broadcasted_iota(jnp.int32, (n, n), 0) // HEAD
    c = lax.broadcasted_iota(jnp.int32, (n, n), 1) // HEAD
    return r == c


def _adaln_kernel(c_ref, w_ref, b_ref, o_ref):
    c = c_ref[...]
    s = (c * jax.nn.sigmoid(c)).astype(BF16)
    o_ref[...] = _dot(s, w_ref[...].astype(BF16)) + b_ref[...]


def _adaln(cvecs, mod_w, mod_b):
    depth, d, n = mod_w.shape
    rows = cvecs.shape[0]
    tn = 1024
    return pl.pallas_call(
        _adaln_kernel,
        grid=(depth, n // tn),
        in_specs=[pl.BlockSpec((rows, d), lambda l, j: (0, 0)),
                  pl.BlockSpec((None, d, tn), lambda l, j: (l, 0, j)),
                  pl.BlockSpec((None, 1, tn), lambda l, j: (l, 0, j))],
        out_specs=pl.BlockSpec((None, rows, tn), lambda l, j: (l, 0, j)),
        out_shape=jax.ShapeDtypeStruct((depth, rows, n), F32),
        compiler_params=_cparams(("parallel", "parallel"), 3 * d * tn * 4 + (8 << 20)),
        name="adaln",
    )(cvecs, mod_w, mod_b.reshape(depth, 1, n))


def _norm_mod_value(x, g, sh, sc):
    y = x * lax.rsqrt(jnp.mean(x * x, axis=-1, keepdims=True) + NORM_EPS)
    return (y * g) * (1.0 + sc) + sh


def _row_spec(arr, d):
    if arr.shape[0] == 1:
        return pl.BlockSpec((None, 1, d), lambda b, *_: (0, 0, 0))
    return pl.BlockSpec((None, 1, d), lambda b, *_: (b, 0, 0))


def _norm_specs(g, sh, sc, d):
    return [pl.BlockSpec((1, d), lambda b, *_: (0, 0)), _row_spec(sh, d), _row_spec(sc, d)]


def _shifted_input(x_ref, prev_ref, next_ref, nrm_refs, t, nt):
    g, sh, sc = (r[...] for r in nrm_refs)
    h = _norm_mod_value(x_ref[...], g, sh, sc)
    tm = h.shape[0]
    row = lax.broadcasted_iota(jnp.int32, h.shape, 0)
    prev_row = jnp.where(t > 0, _norm_mod_value(prev_ref[...], g, sh, sc)[SUBLANES - 1:SUBLANES, :], 0.0)
    next_row = jnp.where(t < nt - 1, _norm_mod_value(next_ref[...], g, sh, sc)[0:1, :], 0.0)
    hp = jnp.where(row == 0, prev_row, pltpu.roll(h, 1, 0))
    hn = jnp.where(row == tm - 1, next_row, pltpu.roll(h, tm - 1, 0))
    return h, 0.5 * (hp + hn) - h


def _halo_specs(t, tm, d):
    per = tm // SUBLANES
    last = t // SUBLANES - 1
    return [pl.BlockSpec((None, tm, d), lambda b, i, *_: (b, i, 0)),
            pl.BlockSpec((None, SUBLANES, d), lambda b, i, *_: (b, jnp.maximum(i * per - 1, 0), 0)),
            pl.BlockSpec((None, SUBLANES, d), lambda b, i, *_: (b, jnp.minimum((i + 1) * per, last), 0))]


def _rkv_kernel(x_ref, xp_ref, xn_ref, g_ref, sh_ref, sc_ref, mu_ref, wr_ref, wk_ref, wv_ref,
                r_ref, k_ref, v_ref, xs_ref, *, nt):
    t = pl.program_id(1)

    @pl.when(pl.program_id(2) == 0)
    def _():
        h, xx = _shifted_input(x_ref, xp_ref, xn_ref, (g_ref, sh_ref, sc_ref), t, nt)
        xs_ref[0] = (h + xx * mu_ref[0:1, :]).astype(BF16)
        xs_ref[1] = (h + xx * mu_ref[2:3, :]).astype(BF16)
        xs_ref[2] = (h + xx * mu_ref[3:4, :]).astype(BF16)

    r_ref[...] = _dot(xs_ref[0], wr_ref[...])
    k_ref[...] = _dot(xs_ref[1], wk_ref[...])
    v_ref[...] = _dot(xs_ref[2], wv_ref[...])


def _rwkv_rkv(x, g, sh, sc, mu, wr, wk, wv):
    b, t, d = x.shape
    tm = min(t, 512)
    tn = 512
    nt = t // tm
    out = jax.ShapeDtypeStruct((b, t, d), F32)
    wspec = pl.BlockSpec((d, tn), lambda i, j, n: (0, n))
    ospec = pl.BlockSpec((None, tm, tn), lambda i, j, n: (i, j, n))
    vmem = 2 * tm * d * 4 + 3 * tm * d * 2 + 6 * d * tn * 2 + 6 * tm * tn * 4 + 3 * tm * d * 4 + (4 << 20)
    return pl.pallas_call(
        functools.partial(_rkv_kernel, nt=nt),
        grid=(b, nt, d // tn),
        in_specs=(_halo_specs(t, tm, d) + _norm_specs(g, sh, sc, d)
                  + [pl.BlockSpec((6, d), lambda i, j, n: (0, 0)), wspec, wspec, wspec]),
        out_specs=[ospec, ospec, ospec],
        out_shape=[out, out, out],
        scratch_shapes=[pltpu.VMEM((3, tm, d), BF16)],
        compiler_params=_cparams(("parallel", "parallel", "arbitrary"), vmem),
        name="rwkv_rkv",
    )(x, x, x, g.reshape(1, d), sh, sc, mu, wr, wk, wv)


def _lora_kernel(x_ref, xp_ref, xn_ref, ng_ref, sh_ref, sc_ref, mu_ref, w0_ref, w1_ref, w2_ref,
                 a0_ref, a1_ref, a2_ref, g1_ref, g2_ref, g_ref, lw0_ref, lw1_ref, as0_ref, as1_ref, *, nt):
    t = pl.program_id(1)
    h, xx = _shifted_input(x_ref, xp_ref, xn_ref, (ng_ref, sh_ref, sc_ref), t, nt)
    xw = (h + xx * mu_ref[1:2, :]).astype(BF16)
    xa = (h + xx * mu_ref[4:5, :]).astype(BF16)
    xg = (h + xx * mu_ref[5:6, :]).astype(BF16)
    g_ref[...] = _dot(jax.nn.sigmoid(_dot(xg, g1_ref[...])).astype(BF16), g2_ref[...])
    for d, (lw_ref, as_ref) in enumerate(((lw0_ref, as0_ref), (lw1_ref, as1_ref))):
        w_pre = w0_ref[d:d + 1, :] + _dot(jnp.tanh(_dot(xw, w1_ref[d])).astype(BF16), w2_ref[d])
        lw_ref[...] = -DECAY_SCALE * jax.nn.sigmoid(w_pre)
        a_pre = a0_ref[d:d + 1, :] + _dot(_dot(xa, a1_ref[d]).astype(BF16), a2_ref[d])
        as_ref[...] = jax.nn.sigmoid(a_pre)


def _pad_lora(w_in, w_out):
    rank = w_in.shape[-1]
    pad = (-rank) % 128
    w_in = jnp.pad(w_in, [(0, 0)] * (w_in.ndim - 1) + [(0, pad)])
    w_out = jnp.pad(w_out, [(0, 0)] * (w_out.ndim - 2) + [(0, pad), (0, 0)])
    return w_in.astype(BF16), w_out.astype(BF16)


def _rwkv_lora(x, g, sh, sc, mu, w0, w1, w2, a0, a1, a2, g1, g2):
    b, t, d = x.shape
    tm = min(t, 256)
    nt = t // tm
    out = jax.ShapeDtypeStruct((b, t, d), F32)
    ospec = pl.BlockSpec((None, tm, d), lambda i, j: (i, j, 0))

    def full(a):
        return pl.BlockSpec(a.shape, lambda i, j, _n=a.ndim: (0,) * _n)

    params = (mu, w0, w1, w2, a0, a1, a2, g1, g2)
    vmem = 2 * 6 * tm * d * 4 + 10 * tm * d * 4 + 2 * sum(p.size * p.dtype.itemsize for p in params)
    return pl.pallas_call(
        functools.partial(_lora_kernel, nt=nt),
        grid=(b, nt),
        in_specs=_halo_specs(t, tm, d) + _norm_specs(g, sh, sc, d) + [full(p) for p in params],
        out_specs=[ospec] * 5,
        out_shape=[out] * 5,
        compiler_params=_cparams(("parallel", "parallel"), vmem),
        name="rwkv_lora",
    )(x, x, x, g.reshape(1, d), sh, sc, *params)


def _split_dot_left(w_bf16, x):
    hi = x.astype(BF16)
    lo = (x - hi.astype(F32)).astype(BF16)
    return _dot(w_bf16, hi) + _dot(w_bf16, lo)


def _scan_chain(d, refs, kks, ka, state_ref, y_ref, sl, g, consts):
    r_ref, k_ref, v_ref, lw_ref, as_ref = refs
    bmask, ones_bd, tri, m_strict, m_incl = consts
    c = CHUNK

    def stack(x):
        return jnp.where(bmask, jnp.concatenate([x] * (GROUP // c), axis=0), 0.0).astype(BF16)

    lw = lw_ref[:, sl]
    a_sig = as_ref[:, sl]
    k = k_ref[:, sl]
    kk = k * kks
    ssq = _split_dot(kk * kk, ones_bd)
    cum = _split_dot_left(tri[d], lw)
    yield

    kk = kk * lax.rsqrt(jnp.maximum(ssq, L2_EPS))
    b_vec = kk * a_sig
    k_d = k * (1.0 + (a_sig - 1.0) * ka)
    tot = cum[c - 1:c, :] if d == 0 else cum[0:1, :]
    a_t = -kk * jnp.exp(cum - lw)
    r_t = r_ref[:, sl] * jnp.exp(cum)
    p_inv = jnp.exp(-cum)
    lhs = jnp.concatenate([a_t, r_t], axis=0).astype(BF16)
    rhs = jnp.concatenate([stack(b_vec * p_inv), stack(k_d * p_inv)], axis=0)
    aa = _dot_nt(lhs, rhs)
    yield

    n1 = jnp.where(m_strict[d], aa[:c, :GROUP], 0.0)
    a_ak = jnp.where(m_strict[d], aa[:c, GROUP:], 0.0)
    a_rb = jnp.where(m_incl[d], aa[c:, :GROUP], 0.0).astype(BF16)
    a_rk = jnp.where(m_incl[d], aa[c:, GROUP:], 0.0)
    v = v_ref[:, sl]
    z1 = _dot(jnp.concatenate([a_ak, a_rk], axis=0).astype(BF16), stack(v))
    q = n1
    npow = _dot(n1.astype(BF16), stack(n1))
    yield
    span = 2
    while 2 * span < c:
        z = _dot(jnp.concatenate([npow, q], axis=0).astype(BF16), stack(npow))
        yield
        q = q + npow + z[c:]
        npow = z[:c]
        span *= 2
    zq = _dot(q.astype(BF16), stack(npow))
    yield
    q = q + npow + zq

    v_ak = z1[:c]
    y0 = z1[c:]
    z2 = _dot(q.astype(BF16), jnp.concatenate([stack(a_t), stack(v_ak)], axis=1))
    yield
    w = a_t + z2[:, :GROUP]
    u0 = v_ak + z2[:, GROUP:]

    s = state_ref[d, g]
    z3 = _dot_nt(jnp.concatenate([w, r_t], axis=0).astype(BF16), s.astype(BF16))
    yield
    u = z3[:c] + u0
    p_end = jnp.exp(tot - cum)
    yu = _dot(a_rb, stack(u))
    upd = _dot_tn(jnp.concatenate([u, v], axis=0).astype(BF16),
                  jnp.concatenate([b_vec * p_end, k_d * p_end], axis=0).astype(BF16))
    yield
    y_ref[:, sl] = z3[c:] + yu + y0
    state_ref[d, g] = s * jnp.exp(tot) + jnp.where(bmask, upd, 0.0)


def _interleave(chains, skew=0.0):
    pending = list(enumerate(chains))
    rnd = 0
    while pending:
        alive = []
        for i, ch in pending:
            if rnd >= (i % 2) * skew:
                try:
                    next(ch)
                except StopIteration:
                    continue
            alive.append((i, ch))
        pending = alive
        rnd += 1


def _scan_kernel(r0, k0, v0, lw0, as0, r1, k1, v1, lw1, as1, kks_ref, ka_ref, sin_ref,
                 y0_ref, y1_ref, state_ref, *, groups):
    i = pl.program_id(1)
    c = CHUNK
    unroll = math.gcd(groups, SCAN_UNROLL)

    @pl.when(i == 0)
    def _():
        state_ref[...] = sin_ref[...]

    bmask = _block_mask(GROUP)
    ones_bd = jnp.where(bmask, 1.0, 0.0).astype(BF16)
    tt = lax.broadcasted_iota(jnp.int32, (c, c), 0)
    ss = lax.broadcasted_iota(jnp.int32, (c, c), 1)
    tri = (jnp.where(ss <= tt, 1.0, 0.0).astype(BF16), jnp.where(ss >= tt, 1.0, 0.0).astype(BF16))
    tp = lax.broadcasted_iota(jnp.int32, (c, GROUP), 0)
    sp = lax.broadcasted_iota(jnp.int32, (c, GROUP), 1) % c
    m_strict = (sp < tp, sp > tp)
    m_incl = (sp <= tp, sp >= tp)
    consts = (bmask, ones_bd, tri, m_strict, m_incl)

    def body(it, carry):
        chains = []
        for j in range(unroll):
            g = it * unroll + j
            sl = pl.ds(pl.multiple_of(g * GROUP, GROUP), GROUP)
            kks = kks_ref[:, sl]
            ka = ka_ref[:, sl]
            chains.append(_scan_chain(0, (r0, k0, v0, lw0, as0), kks, ka, state_ref, y0_ref, sl, g, consts))
            chains.append(_scan_chain(1, (r1, k1, v1, lw1, as1), kks, ka, state_ref, y1_ref, sl, g, consts))
        _interleave(chains, SCAN_SKEW)
        return carry

    lax.fori_loop(0, groups // unroll, body, 0)


def _rwkv_scan(r, k, v, lw0, as0, lw1, as1, kk_scale, ka, s_init):
    b, t, d = r.shape
    assert CHUNK * (GROUP // HEAD) == GROUP and t % CHUNK == 0 and d % GROUP == 0
    groups = d // GROUP
    nc = t // CHUNK
    fwd = pl.BlockSpec((None, CHUNK, d), lambda i, j: (i, j, 0))
    bwd = pl.BlockSpec((None, CHUNK, d), lambda i, j: (i, nc - 1 - j, 0))
    vec = pl.BlockSpec((1, d), lambda i, j: (0, 0))
    sspec = pl.BlockSpec((None, 2, groups, GROUP, GROUP), lambda i, j: (i, 0, 0, 0, 0))
    y = jax.ShapeDtypeStruct((b, t, d), F32)
    state_bytes = 2 * groups * GROUP * GROUP * 4
    vmem = 2 * 12 * CHUNK * d * 4 + 4 * state_bytes + 2 * SCAN_UNROLL * (3 << 19) + (4 << 20)
    return pl.pallas_call(
        functools.partial(_scan_kernel, groups=groups),
        grid=(b, nc),
        in_specs=[fwd] * 5 + [bwd] * 5 + [vec, vec, sspec],
        out_specs=[fwd, bwd, sspec],
        out_shape=[y, y, jax.ShapeDtypeStruct(s_init.shape, F32)],
        compiler_params=_cparams(("parallel", "arbitrary"), vmem),
        name="rwkv_scan",
    )(r, k, v, lw0, as0, r, k, v, lw1, as1, kk_scale.reshape(1, d), ka.reshape(1, d), s_init)


def _readout_kernel(y0_ref, y1_ref, r_ref, k_ref, v_ref, as0_ref, as1_ref, g_ref,
                    ka_ref, rk_ref, lnw_ref, lnb_ref, o_ref):
    d = o_ref.shape[-1]
    ones_bd = jnp.where(_block_mask(GROUP), 1.0, 0.0).astype(BF16)
    for j in range(d // GROUP):
        sl = slice(j * GROUP, (j + 1) * GROUP)
        y = y0_ref[:, sl] + y1_ref[:, sl]
        mean = _split_dot(y, ones_bd) * (1.0 / HEAD)
        yc = y - mean
        var = _split_dot(yc * yc, ones_bd) * (1.0 / HEAD)
        yn = (yc * lax.rsqrt(var + GN_EPS)) * lnw_ref[:, sl] + lnb_ref[:, sl]
        k = k_ref[:, sl]
        ka = ka_ref[:, sl]
        k_sum = k * (1.0 + (as0_ref[:, sl] - 1.0) * ka) + k * (1.0 + (as1_ref[:, sl] - 1.0) * ka)
        bonus = _split_dot(r_ref[:, sl] * k_sum * rk_ref[:, sl], ones_bd) * v_ref[:, sl]
        o_ref[:, sl] = ((yn + bonus) * g_ref[:, sl]).astype(o_ref.dtype)


def _rwkv_readout(y0, y1, r, k, v, as0, as1, g, ka, rk, lnw, lnb):
    b, t, d = r.shape
    tm = min(t, 256)
    spec = pl.BlockSpec((None, tm, d), lambda i, j: (i, j, 0))
    vec = pl.BlockSpec((1, d), lambda i, j: (0, 0))
    return pl.pallas_call(
        _readout_kernel,
        grid=(b, t // tm),
        in_specs=[spec] * 8 + [vec] * 4,
        out_specs=spec,
        out_shape=jax.ShapeDtypeStruct((b, t, d), BF16),
        compiler_params=_cparams(("parallel", "parallel"), 2 * 9 * tm * d * 4 + (8 << 20)),
        name="rwkv_readout",
    )(y0, y1, r, k, v, as0, as1, g, ka.reshape(1, d), rk.reshape(1, d), lnw.reshape(1, d), lnb.reshape(1, d))


def _mm_res_kernel(z_ref, w_ref, res_ref, gate_ref, o_ref):
    o_ref[...] = res_ref[...] + gate_ref[...] * _dot(z_ref[...], w_ref[...])


def _norm_mm_kernel(x_ref, g_ref, sh_ref, sc_ref, w_ref, o_ref, h_ref):
    @pl.when(pl.program_id(2) == 0)
    def _():
        h_ref[...] = _norm_mod_value(x_ref[...], g_ref[...], sh_ref[...], sc_ref[...]).astype(BF16)

    o_ref[...] = _dot(h_ref[...], w_ref[...])


def _norm_matmul(x, g, sh, sc, w):
    b, t, d = x.shape
    n = w.shape[1]
    tm = min(t, 1024)
    tn = 512
    vmem = 2 * tm * d * 4 + tm * d * 2 + 2 * d * tn * 2 + 3 * tm * tn * 4 + tm * d * 4 + (4 << 20)
    return pl.pallas_call(
        _norm_mm_kernel,
        grid=(b, t // tm, n // tn),
        in_specs=[pl.BlockSpec((None, tm, d), lambda i, j, l: (i, j, 0))] + _norm_specs(g, sh, sc, d)
                 + [pl.BlockSpec((d, tn), lambda i, j, l: (0, l))],
        out_specs=pl.BlockSpec((None, tm, tn), lambda i, j, l: (i, j, l)),
        out_shape=jax.ShapeDtypeStruct((b, t, n), F32),
        scratch_shapes=[pltpu.VMEM((tm, d), BF16)],
        compiler_params=_cparams(("parallel", "parallel", "arbitrary"), vmem),
        name="norm_matmul",
    )(x, g.reshape(1, d), sh, sc, w)


def _matmul_res(z, w, res, gate):
    b, t, kdim = z.shape
    n = w.shape[1]
    tm = min(t, 1024)
    tn = 512
    zspec = pl.BlockSpec((None, tm, kdim), lambda i, j, l: (i, j, 0))
    wspec = pl.BlockSpec((kdim, tn), lambda i, j, l: (0, l))
    ospec = pl.BlockSpec((None, tm, tn), lambda i, j, l: (i, j, l))
    gspec = (pl.BlockSpec((None, 1, tn), lambda i, j, l: (0, 0, l)) if gate.shape[0] == 1
             else pl.BlockSpec((None, 1, tn), lambda i, j, l: (i, 0, l)))
    vmem = 2 * (tm * kdim * 2 + kdim * tn * 2 + 2 * tm * tn * 4) + tm * tn * 4 + (4 << 20)
    return pl.pallas_call(
        _mm_res_kernel,
        grid=(b, t // tm, n // tn),
        in_specs=[zspec, wspec, ospec, gspec],
        out_specs=ospec,
        out_shape=jax.ShapeDtypeStruct((b, t, n), F32),
        compiler_params=_cparams(("parallel", "parallel", "parallel"), vmem),
        name="matmul_res",
    )(z, w, res, gate)


def _ffn_kernel(x_ref, g_ref, sh_ref, sc_ref, gate_ref, wg_ref, wu_ref, wd_ref, o_ref, h_ref):
    f = pl.program_id(2)

    @pl.when(f == 0)
    def _():
        h_ref[...] = _norm_mod_value(x_ref[...], g_ref[...], sh_ref[...], sc_ref[...]).astype(BF16)
        o_ref[...] = jnp.zeros_like(o_ref)

    h = h_ref[...]
    gp = _dot(h, wg_ref[...])
    act = ((gp * jax.nn.sigmoid(gp)) * _dot(h, wu_ref[...])).astype(BF16)
    d = o_ref.shape[-1]
    for c0 in range(0, d, FFN_ACC_COLS):
        cols = slice(c0, c0 + FFN_ACC_COLS)
        o_ref[:, cols] += _dot(act, wd_ref[:, cols])

    @pl.when(f == pl.num_programs(2) - 1)
    def _():
        o_ref[...] = x_ref[...] + gate_ref[...] * o_ref[...]


def _ffn(x, g, sh, sc, gate, wg, wu, wd):
    b, t, d = x.shape
    ff = wg.shape[1]
    tm = min(t, 512)
    tf = 512
    xspec = pl.BlockSpec((None, tm, d), lambda i, j, f: (i, j, 0))
    vmem = 4 * tm * d * 4 + tm * d * 2 + 2 * 3 * d * tf * 2 + 6 * tm * tf * 4 + tm * d * 4 + (4 << 20)
    return pl.pallas_call(
        _ffn_kernel,
        grid=(b, t // tm, ff // tf),
        in_specs=[xspec, pl.BlockSpec((1, d), lambda i, j, f: (0, 0)),
                  _row_spec(sh, d), _row_spec(sc, d), _row_spec(gate, d),
                  pl.BlockSpec((d, tf), lambda i, j, f: (0, f)),
                  pl.BlockSpec((d, tf), lambda i, j, f: (0, f)),
                  pl.BlockSpec((tf, d), lambda i, j, f: (f, 0))],
        out_specs=xspec,
        out_shape=jax.ShapeDtypeStruct((b, t, d), F32),
        scratch_shapes=[pltpu.VMEM((tm, d), BF16)],
        compiler_params=_cparams(("parallel", "parallel", "arbitrary"), vmem),
        name="ffn",
    )(x, g.reshape(1, d), sh, sc, gate, wg, wu, wd)


def _qk_norm(x, gain, ones_bd):
    ms = _split_dot(x * x, ones_bd) * (1.0 / HEAD)
    return x * lax.rsqrt(ms + QK_EPS) * gain


def _rope(u, cos, sin):
    n = u.shape[-1]
    lane = lax.broadcasted_iota(jnp.int32, u.shape, 1)
    partner = jnp.where(lane % 32 < 16, pltpu.roll(u, n - 16, 1), pltpu.roll(u, 16, 1))
    return u * cos + partner * sin


def _qkv_prep_kernel(*refs, rope, with_q):
    refs = list(refs)
    q_ref = refs.pop(0) if with_q else None
    k_ref, v_ref = refs.pop(0), refs.pop(0)
    qn_ref = refs.pop(0) if with_q else None
    kn_ref = refs.pop(0)
    if rope:
        cos_ref, sin_ref = refs.pop(0), refs.pop(0)
    qo_ref = refs.pop(0) if with_q else None
    ko_ref, vo_ref = refs
    d = ko_ref.shape[-1]
    ones_bd = jnp.where(_block_mask(GROUP), 1.0, 0.0).astype(BF16)
    if rope:
        cos, sin = cos_ref[...], sin_ref[...]
    for j in range(d // GROUP):
        sl = slice(j * GROUP, (j + 1) * GROUP)
        kh = _qk_norm(k_ref[:, sl], kn_ref[:, sl], ones_bd)
        if rope:
            kh = _rope(kh, cos, sin)
        ko_ref[:, sl] = kh.astype(BF16)
        if with_q:
            qh = _qk_norm(q_ref[:, sl], qn_ref[:, sl], ones_bd)
            if rope:
                qh = _rope(qh, cos, sin)
            qo_ref[:, sl] = (qh * (HEAD ** -0.5 * LOG2E)).astype(BF16)
    vo_ref[...] = v_ref[...].T.astype(BF16)


def _qkv_prep(qkv, qn, kn, cos, sin, *, with_q):
    b, t, width = qkv.shape
    parts = 3 if with_q else 2
    d = width // parts
    rope = cos is not None
    tm = min(t, 256)
    col = lambda c: pl.BlockSpec((None, tm, d), lambda i, j, _c=c: (i, j, _c))
    vec = pl.BlockSpec((1, d), lambda i, j: (0, 0))
    tab = pl.BlockSpec((tm, GROUP), lambda i, j: (j, 0))
    in_specs = [col(c) for c in range(parts)] + [vec] * (2 if with_q else 1) + ([tab, tab] if rope else [])
    args = [qkv] * parts + ([qn] if with_q else []) + [kn] + ([cos, sin] if rope else [])
    out = jax.ShapeDtypeStruct((b, t, d), BF16)
    ospec = pl.BlockSpec((None, tm, d), lambda i, j: (i, j, 0))
    out_t = jax.ShapeDtypeStruct((b, d, t), BF16)
    ospec_t = pl.BlockSpec((None, d, tm), lambda i, j: (i, 0, j))
    return pl.pallas_call(
        functools.partial(_qkv_prep_kernel, rope=rope, with_q=with_q),
        grid=(b, t // tm),
        in_specs=in_specs,
        out_specs=[ospec] * (parts - 1) + [ospec_t],
        out_shape=[out] * (parts - 1) + [out_t],
        compiler_params=_cparams(("parallel", "parallel"), 2 * parts * tm * d * 6 + (16 << 20)),
        name="qkv_prep",
    )(*args)


def _attn_chain(q2, k, vt, state, shift):
    m, acc_t = state
    s_t = _dot_nt(k, q2)
    yield
    if shift is None:
        m_new = jnp.maximum(m, jnp.max(s_t, axis=0, keepdims=True))
        alpha = jnp.exp2(m - m_new)
        state[0] = m_new
        p = jnp.exp2(s_t - m_new)
    else:
        p = jnp.exp2(s_t - shift)
    pv = _dot(vt, p.astype(BF16))
    yield
    state[1] = pv + (acc_t if shift is not None else alpha * acc_t)


def _attn_kernel(lam_ref, q_ref, k_ref, vt_ref, sub_ref, o_ref, *, tk, tq, lambda_init):
    n_sub = q_ref.shape[0] // tq
    tkeys = k_ref.shape[0]
    first = tkeys % tk or tk
    lp = lam_ref[...]
    lam = (jnp.exp(jnp.sum(lp[0:1] * lp[1:2], axis=-1, keepdims=True))
           - jnp.exp(jnp.sum(lp[2:3] * lp[3:4], axis=-1, keepdims=True)) + lambda_init)
    bound = (HEAD ** 0.5 * LOG2E * SCORE_MARGIN) * (jnp.max(jnp.abs(lp[4:5]), axis=-1, keepdims=True)
                                                    * jnp.max(jnp.abs(lp[5:6]), axis=-1, keepdims=True))
    fixed_shift_ok = bound[0, 0] <= MAX_FIXED_SHIFT
    q2s = []
    for c in range(n_sub):
        q = q_ref[c * tq:(c + 1) * tq, :]
        lane = lax.broadcasted_iota(jnp.int32, q.shape, 1)
        zero = jnp.zeros_like(q)
        q2s.append(jnp.concatenate([jnp.where(lane < HEAD, q, zero), jnp.where(lane >= HEAD, q, zero)], axis=0))

    hw = 2 * HEAD
    ones_rows = 16

    def run(shift):
        def step(ks, nkeys, carry):
            k = k_ref[ks, :]
            vt = jnp.concatenate([vt_ref[:, ks], jnp.ones((ones_rows, nkeys), BF16)], axis=0)
            states = [list(st) for st in carry]
            _interleave([_attn_chain(q2, k, vt, st, shift) for q2, st in zip(q2s, states)])
            return tuple(tuple(st) for st in states)

        init = (jnp.full((1, 2 * tq), -jnp.inf, F32), jnp.zeros((hw + ones_rows, 2 * tq), F32))
        carry = step(slice(0, first), first, (init,) * n_sub)
        carry = lax.fori_loop(
            0, (tkeys - first) // tk,
            lambda j, cr: step(pl.ds(pl.multiple_of(first + j * tk, first), tk), tk, cr), carry)
        for c, (_, acc) in enumerate(carry):
            acc_t, l = acc[:hw], acc[hw:hw + 1]
            o_t = acc_t[:, :tq] / l[:, :tq] - lam * (acc_t[:, tq:] / l[:, tq:])
            o = o_t.T
            o = o * lax.rsqrt(jnp.mean(o * o, axis=-1, keepdims=True) + SUBLN_EPS) * sub_ref[...]
            o_ref[c * tq:(c + 1) * tq, :] = (o * (1.0 - lambda_init)).astype(o_ref.dtype)

    @pl.when(fixed_shift_ok)
    def _():
        run(bound)

    @pl.when(jnp.logical_not(fixed_shift_ok))
    def _():
        run(None)


def _diff_attention(q, k_all, vt_all, lam_params, sub_gain, lambda_init):
    b, t, d = q.shape
    tkeys = k_all.shape[1]
    hw = 2 * HEAD
    tq = 256
    n_sub = math.gcd(t // tq, ATTN_CHAINS)
    tk = 2048
    assert t % (tq * n_sub) == 0 and (tkeys % tk) % 128 == 0
    qspec = pl.BlockSpec((None, tq * n_sub, hw), lambda i, h, j: (i, j, h))
    kspec = pl.BlockSpec((None, tkeys, hw), lambda i, h, j: (i, 0, h))
    vspec = pl.BlockSpec((None, hw, tkeys), lambda i, h, j: (i, h, 0))
    vmem = 4 * tkeys * hw * 2 + 4 * n_sub * tq * hw * 2 + n_sub * 8 * 2 * tq * tk * 4 + (8 << 20)
    return pl.pallas_call(
        functools.partial(_attn_kernel, tk=tk, tq=tq, lambda_init=lambda_init),
        grid=(b, d // hw, t // (tq * n_sub)),
        in_specs=[pl.BlockSpec(lam_params.shape, lambda i, h, j: (0, 0)), qspec, kspec, vspec,
                  pl.BlockSpec((1, hw), lambda i, h, j: (0, 0))],
        out_specs=qspec,
        out_shape=jax.ShapeDtypeStruct((b, t, d), BF16),
        compiler_params=_cparams(("parallel", "parallel", "parallel"), vmem),
        name="diff_attention",
    )(lam_params, q, k_all, vt_all, sub_gain.reshape(1, hw))


def _rope_tables(n_tokens):
    n_freq = HEAD // 4
    pos = jnp.arange(n_tokens)
    freqs = ROPE_BASE ** (-jnp.arange(n_freq, dtype=F32) / n_freq)
    ang_row = (pos // GRID_W).astype(F32)[:, None] * freqs
    ang_col = (pos % GRID_W).astype(F32)[:, None] * freqs
    cos = jnp.concatenate([jnp.cos(ang_row)] * 2 + [jnp.cos(ang_col)] * 2, axis=-1)
    sin = jnp.concatenate([-jnp.sin(ang_row), jnp.sin(ang_row), -jnp.sin(ang_col), jnp.sin(ang_col)], axis=-1)
    reps = GROUP // HEAD
    return jnp.tile(cos, (1, reps)), jnp.tile(sin, (1, reps))


def _mod_rows(mod, layer, j, d, batch):
    m = mod[layer, :, j * d:(j + 1) * d]
    return m[:batch, None, :], m[batch:batch + 1, None, :]


def _rwkv_layer(xs, mods, norm_g, p):
    rkv, lora, outs = [], [], []
    for x, (sh, sc, _) in zip(xs, mods):
        rkv.append(_rwkv_rkv(x, norm_g, sh, sc, p["mu"], p["wr"], p["wk"], p["wv"]))
        lora.append(_rwkv_lora(x, norm_g, sh, sc, p["mu"], p["w0"], p["w1"], p["w2"],
                               p["a0"], p["a1"], p["a2"], p["g1"], p["g2"]))
    b, _, d = xs[0].shape
    state = jnp.zeros((b, 2, d // GROUP, GROUP, GROUP), F32)
    for x, (_, _, gate), (r, k, v), (g, lw0, lw1, as0, as1) in zip(xs, mods, rkv, lora):
        y0, y1, state = _rwkv_scan(r, k, v, lw0, as0, lw1, as1, p["kk"], p["ka"], state)
        z = _rwkv_readout(y0, y1, r, k, v, as0, as1, g, p["ka"], p["rk"], p["lnw"], p["lnb"])
        outs.append(_matmul_res(z, p["wo"], x, gate))
    return outs


def kernel(x, c, ctx, c_ctx, mod_w, mod_b, norm1_g, norm2_g, rwkv_mu, rwkv_wr, rwkv_wk, rwkv_wv, rwkv_wo, rwkv_w0, rwkv_w1, rwkv_w2, rwkv_a0, rwkv_a1, rwkv_a2, rwkv_g1, rwkv_g2, rwkv_kk, rwkv_ka, rwkv_rk, rwkv_lnw, rwkv_lnb, diff_wqkv, diff_qn, diff_kn, diff_lq1, diff_lk1, diff_lq2, diff_lk2, diff_subln, diff_wo, ffn_wg, ffn_wu, ffn_wd):
    batch, n_lat, d = x.shape
    depth = mod_w.shape[0]
    assert depth == 2, "layer 0 is the RWKV-7 mixer, layer 1 differential attention"

    cvecs = jnp.zeros((SUBLANES, d), F32).at[:batch].set(c).at[batch].set(c_ctx)
    mod = _adaln(cvecs, mod_w, mod_b)

    def mods(layer, first):
        lat, cx = zip(*[_mod_rows(mod, layer, first + j, d, batch) for j in range(3)])
        return cx, lat

    w1, w2 = _pad_lora(rwkv_w1[0], rwkv_w2[0])
    a1, a2 = _pad_lora(rwkv_a1[0], rwkv_a2[0])
    g1, g2 = _pad_lora(rwkv_g1[0], rwkv_g2[0])
    p = dict(mu=rwkv_mu[0], wr=rwkv_wr[0].astype(BF16), wk=rwkv_wk[0].astype(BF16), wv=rwkv_wv[0].astype(BF16),
             wo=rwkv_wo[0].astype(BF16), w0=rwkv_w0[0], w1=w1, w2=w2, a0=rwkv_a0[0], a1=a1, a2=a2,
             g1=g1, g2=g2, kk=rwkv_kk[0], ka=rwkv_ka[0],
             rk=rwkv_rk[0], lnw=rwkv_lnw[0], lnb=rwkv_lnb[0])
    xc, xl = _rwkv_layer((ctx, x), mods(0, 0), norm1_g[0], p)
    wg, wu, wd = ffn_wg[0].astype(BF16), ffn_wu[0].astype(BF16), ffn_wd[0].astype(BF16)
    (csh, csc, cgt), (lsh, lsc, lgt) = mods(0, 3)
    xc = _ffn(xc, norm2_g[0], csh, csc, cgt, wg, wu, wd)
    xl = _ffn(xl, norm2_g[0], lsh, lsc, lgt, wg, wu, wd)

    (csh, csc, _), (lsh, lsc, lgt) = mods(1, 0)
    lambda_init = 0.8 - 0.6 * math.exp(-0.3 * 1)
    wqkv = diff_wqkv[0].astype(BF16)
    qn = jnp.tile(diff_qn[0], d // HEAD).reshape(1, d)
    kn = jnp.tile(diff_kn[0], d // HEAD).reshape(1, d)
    cos, sin = _rope_tables(n_lat)
    q_l, k_l, v_l = _qkv_prep(_norm_matmul(xl, norm1_g[1], lsh, lsc, wqkv), qn, kn, cos, sin, with_q=True)
    k_c, v_c = _qkv_prep(_norm_matmul(xc, norm1_g[1], csh, csc, wqkv[:, d:]), None, kn, None, None, with_q=False)
    k_all = jnp.concatenate([k_c, k_l], axis=1)
    v_all = jnp.concatenate([v_c, v_l], axis=2)
    lam_params = jnp.stack([diff_lq1[0], diff_lk1[0], diff_lq2[0], diff_lk2[0], diff_qn[0], diff_kn[0]])
    o = _diff_attention(q_l, k_all, v_all, lam_params, diff_subln[0], lambda_init)
    xl = _matmul_res(o, diff_wo[0].astype(BF16), xl, lgt)
    (_, _, _), (lsh, lsc, lgt) = mods(1, 3)
    return _ffn(xl, norm2_g[1], lsh, lsc, lgt,
                ffn_wg[1].astype(BF16), ffn_wu[1].astype(BF16), ffn_wd[1].astype(BF16))
```

```python
import functools
import math

import jax
import jax.numpy as jnp
from jax import lax
from jax.experimental import pallas as pl
from jax.experimental.pallas import tpu as pltpu

F32 = jnp.float32
BF16 = jnp.bfloat16

HEAD = 64
GROUP = 256
CHUNK = 64
SCAN_UNROLL = 8
SCAN_SKEW = 4
ATTN_CHAINS = 4
FFN_ACC_COLS = 512
SUBLANES = 8
V7X_VMEM_CAP = 56 * 1024 * 1024

NORM_EPS = 1e-6
GN_EPS = 64e-5
L2_EPS = 1e-24
QK_EPS = 1e-6
SUBLN_EPS = 1e-5
ROPE_BASE = 10000.0
GRID_W = 64
DECAY_SCALE = math.exp(-0.5)
LOG2E = math.log2(math.e)
SCORE_MARGIN = 1.01
MAX_FIXED_SHIFT = 50.0


def _cparams(semantics, vmem_bytes):
    return pltpu.CompilerParams(dimension_semantics=semantics,
                                vmem_limit_bytes=int(min(max(vmem_bytes, 16 << 20), V7X_VMEM_CAP)))


def _dot(a, b):
    return jnp.dot(a, b, preferred_element_type=F32)


def _dot_nt(a, b):
    return lax.dot_general(a, b, (((1,), (1,)), ((), ())), preferred_element_type=F32)


def _dot_tn(a, b):
    return lax.dot_general(a, b, (((0,), (0,)), ((), ())), preferred_element_type=F32)


def _split_dot(x, w_bf16):
    hi = x.astype(BF16)
    lo = (x - hi.astype(F32)).astype(BF16)
    return _dot(hi, w_bf16) + _dot(lo, w_bf16)


def _block_mask(n):
    r = lax.broadcasted_iota(jnp.int32, (n, n), 0) // HEAD
    c = lax.broadcasted_iota(jnp.int32, (n, n), 1) // HEAD
    return r == c


def _adaln_kernel(c_ref, w_ref, b_ref, o_ref):
    c = c_ref[...]
    s = (c * jax.nn.sigmoid(c)).astype(BF16)
    o_ref[...] = _dot(s, w_ref[...].astype(BF16)) + b_ref[...]


def _adaln(cvecs, mod_w, mod_b):
    depth, d, n = mod_w.shape
    rows = cvecs.shape[0]
    tn = 1024
    return pl.pallas_call(
        _adaln_kernel,
        grid=(depth, n // tn),
        in_specs=[pl.BlockSpec((rows, d), lambda l, j: (0, 0)),
                  pl.BlockSpec((None, d, tn), lambda l, j: (l, 0, j)),
                  pl.BlockSpec((None, 1, tn), lambda l, j: (l, 0, j))],
        out_specs=pl.BlockSpec((None, rows, tn), lambda l, j: (l, 0, j)),
        out_shape=jax.ShapeDtypeStruct((depth, rows, n), F32),
        compiler_params=_cparams(("parallel", "parallel"), 3 * d * tn * 4 + (8 << 20)),
        name="adaln",
    )(cvecs, mod_w, mod_b.reshape(depth, 1, n))


def _norm_mod_value(x, g, sh, sc):
    y = x * lax.rsqrt(jnp.mean(x * x, axis=-1, keepdims=True) + NORM_EPS)
    return (y * g) * (1.0 + sc) + sh


def _row_spec(arr, d):
    if arr.shape[0] == 1:
        return pl.BlockSpec((None, 1, d), lambda b, *_: (0, 0, 0))
    return pl.BlockSpec((None, 1, d), lambda b, *_: (b, 0, 0))


def _norm_specs(g, sh, sc, d):
    return [pl.BlockSpec((1, d), lambda b, *_: (0, 0)), _row_spec(sh, d), _row_spec(sc, d)]


def _shifted_input(x_ref, prev_ref, next_ref, nrm_refs, t, nt):
    g, sh, sc = (r[...] for r in nrm_refs)
    h = _norm_mod_value(x_ref[...], g, sh, sc)
    tm = h.shape[0]
    row = lax.broadcasted_iota(jnp.int32, h.shape, 0)
    prev_row = jnp.where(t > 0, _norm_mod_value(prev_ref[...], g, sh, sc)[SUBLANES - 1:SUBLANES, :], 0.0)
    next_row = jnp.where(t < nt - 1, _norm_mod_value(next_ref[...], g, sh, sc)[0:1, :], 0.0)
    hp = jnp.where(row == 0, prev_row, pltpu.roll(h, 1, 0))
    hn = jnp.where(row == tm - 1, next_row, pltpu.roll(h, tm - 1, 0))
    return h, 0.5 * (hp + hn) - h


def _halo_specs(t, tm, d):
    per = tm // SUBLANES
    last = t // SUBLANES - 1
    return [pl.BlockSpec((None, tm, d), lambda b, i, *_: (b, i, 0)),
            pl.BlockSpec((None, SUBLANES, d), lambda b, i, *_: (b, jnp.maximum(i * per - 1, 0), 0)),
            pl.BlockSpec((None, SUBLANES, d), lambda b, i, *_: (b, jnp.minimum((i + 1) * per, last), 0))]


def _rkv_kernel(x_ref, xp_ref, xn_ref, g_ref, sh_ref, sc_ref, mu_ref, wr_ref, wk_ref, wv_ref,
                r_ref, k_ref, v_ref, xs_ref, *, nt):
    t = pl.program_id(1)

    @pl.when(pl.program_id(2) == 0)
    def _():
        h, xx = _shifted_input(x_ref, xp_ref, xn_ref, (g_ref, sh_ref, sc_ref), t, nt)
        xs_ref[0] = (h + xx * mu_ref[0:1, :]).astype(BF16)
        xs_ref[1] = (h + xx * mu_ref[2:3, :]).astype(BF16)
        xs_ref[2] = (h + xx * mu_ref[3:4, :]).astype(BF16)

    r_ref[...] = _dot(xs_ref[0], wr_ref[...])
    k_ref[...] = _dot(xs_ref[1], wk_ref[...])
    v_ref[...] = _dot(xs_ref[2], wv_ref[...])


def _rwkv_rkv(x, g, sh, sc, mu, wr, wk, wv):
    b, t, d = x.shape
    tm = min(t, 512)
    tn = 512
    nt = t // tm
    out = jax.ShapeDtypeStruct((b, t, d), F32)
    wspec = pl.BlockSpec((d, tn), lambda i, j, n: (0, n))
    ospec = pl.BlockSpec((None, tm, tn), lambda i, j, n: (i, j, n))
    vmem = 2 * tm * d * 4 + 3 * tm * d * 2 + 6 * d * tn * 2 + 6 * tm * tn * 4 + 3 * tm * d * 4 + (4 << 20)
    return pl.pallas_call(
        functools.partial(_rkv_kernel, nt=nt),
        grid=(b, nt, d // tn),
        in_specs=(_halo_specs(t, tm, d) + _norm_specs(g, sh, sc, d)
                  + [pl.BlockSpec((6, d), lambda i, j, n: (0, 0)), wspec, wspec, wspec]),
        out_specs=[ospec, ospec, ospec],
        out_shape=[out, out, out],
        scratch_shapes=[pltpu.VMEM((3, tm, d), BF16)],
        compiler_params=_cparams(("parallel", "parallel", "arbitrary"), vmem),
        name="rwkv_rkv",
    )(x, x, x, g.reshape(1, d), sh, sc, mu, wr, wk, wv)


def _lora_kernel(x_ref, xp_ref, xn_ref, ng_ref, sh_ref, sc_ref, mu_ref, w0_ref, w1_ref, w2_ref,
                 a0_ref, a1_ref, a2_ref, g1_ref, g2_ref, g_ref, lw0_ref, lw1_ref, as0_ref, as1_ref, *, nt):
    t = pl.program_id(1)
    h, xx = _shifted_input(x_ref, xp_ref, xn_ref, (ng_ref, sh_ref, sc_ref), t, nt)
    xw = (h + xx * mu_ref[1:2, :]).astype(BF16)
    xa = (h + xx * mu_ref[4:5, :]).astype(BF16)
    xg = (h + xx * mu_ref[5:6, :]).astype(BF16)
    g_ref[...] = _dot(jax.nn.sigmoid(_dot(xg, g1_ref[...])).astype(BF16), g2_ref[...]).astype(g_ref.dtype)
    for d, (lw_ref, as_ref) in enumerate(((lw0_ref, as0_ref), (lw1_ref, as1_ref))):
        w_pre = w0_ref[d:d + 1, :] + _dot(jnp.tanh(_dot(xw, w1_ref[d])).astype(BF16), w2_ref[d])
        lw_ref[...] = (-DECAY_SCALE * LOG2E) * jax.nn.sigmoid(w_pre)
        a_pre = a0_ref[d:d + 1, :] + _dot(_dot(xa, a1_ref[d]).astype(BF16), a2_ref[d])
        as_ref[...] = jax.nn.sigmoid(a_pre).astype(as_ref.dtype)


def _pad_lora(w_in, w_out):
    rank = w_in.shape[-1]
    pad = (-rank) % 128
    w_in = jnp.pad(w_in, [(0, 0)] * (w_in.ndim - 1) + [(0, pad)])
    w_out = jnp.pad(w_out, [(0, 0)] * (w_out.ndim - 2) + [(0, pad), (0, 0)])
    return w_in.astype(BF16), w_out.astype(BF16)


def _rwkv_lora(x, g, sh, sc, mu, w0, w1, w2, a0, a1, a2, g1, g2):
    b, t, d = x.shape
    tm = min(t, 256)
    nt = t // tm
    out = jax.ShapeDtypeStruct((b, t, d), F32)
    out16 = jax.ShapeDtypeStruct((b, t, d), BF16)
    ospec = pl.BlockSpec((None, tm, d), lambda i, j: (i, j, 0))

    def full(a):
        return pl.BlockSpec(a.shape, lambda i, j, _n=a.ndim: (0,) * _n)

    params = (mu, w0, w1, w2, a0, a1, a2, g1, g2)
    vmem = 2 * 6 * tm * d * 4 + 10 * tm * d * 4 + 2 * sum(p.size * p.dtype.itemsize for p in params)
    return pl.pallas_call(
        functools.partial(_lora_kernel, nt=nt),
        grid=(b, nt),
        in_specs=_halo_specs(t, tm, d) + _norm_specs(g, sh, sc, d) + [full(p) for p in params],
        out_specs=[ospec] * 5,
        out_shape=[out16, out, out, out16, out16],
        compiler_params=_cparams(("parallel", "parallel"), vmem),
        name="rwkv_lora",
    )(x, x, x, g.reshape(1, d), sh, sc, *params)


def _chunk_cumsum(x, reverse):
    n = x.shape[0]
    row = lax.broadcasted_iota(jnp.int32, x.shape, 0)
    shift = 1
    while shift < n:
        if reverse:
            x = x + jnp.where(row < n - shift, pltpu.roll(x, n - shift, 0), 0.0)
        else:
            x = x + jnp.where(row >= shift, pltpu.roll(x, shift, 0), 0.0)
        shift *= 2
    return x


def _scan_chain(d, refs, kks, ka, state_ref, y_ref, sl, g, consts):
    r_ref, k_ref, v_ref, lw_ref, as_ref = refs
    bmask, ones_bd, m_strict, m_incl = consts
    c = CHUNK

    def stack(x):
        return jnp.where(bmask, jnp.concatenate([x] * (GROUP // c), axis=0), 0.0).astype(BF16)

    lw = lw_ref[:, sl]
    a_sig = as_ref[:, sl].astype(F32)
    k = k_ref[:, sl]
    kk = k * kks
    ssq = _dot((kk * kk).astype(BF16), ones_bd)
    cum = _chunk_cumsum(lw, reverse=d == 1)
    yield

    kk = kk * lax.rsqrt(jnp.maximum(ssq, L2_EPS))
    b_vec = kk * a_sig
    k_d = k * (1.0 + (a_sig - 1.0) * ka)
    tot = cum[c - 1:c, :] if d == 0 else cum[0:1, :]
    a_t = -kk * jnp.exp2(cum - lw)
    r_t = r_ref[:, sl] * jnp.exp2(cum)
    p_inv = jnp.exp2(-cum)
    lhs = jnp.concatenate([a_t, r_t], axis=0).astype(BF16)
    rhs = jnp.concatenate([stack(b_vec * p_inv), stack(k_d * p_inv)], axis=0)
    aa = _dot_nt(lhs, rhs)
    yield

    n1 = jnp.where(m_strict[d], aa[:c, :GROUP], 0.0)
    a_ak = jnp.where(m_strict[d], aa[:c, GROUP:], 0.0)
    a_rb = jnp.where(m_incl[d], aa[c:, :GROUP], 0.0).astype(BF16)
    a_rk = jnp.where(m_incl[d], aa[c:, GROUP:], 0.0)
    v = v_ref[:, sl]
    z1 = _dot(jnp.concatenate([a_ak, a_rk], axis=0).astype(BF16), stack(v))
    q = n1
    npow = _dot(n1.astype(BF16), stack(n1))
    yield
    span = 2
    while 2 * span < c:
        z = _dot(jnp.concatenate([npow, q], axis=0).astype(BF16), stack(npow))
        yield
        q = q + npow + z[c:]
        npow = z[:c]
        span *= 2
    zq = _dot(q.astype(BF16), stack(npow))
    yield
    q = q + npow + zq

    v_ak = z1[:c]
    y0 = z1[c:]
    z2 = _dot(q.astype(BF16), jnp.concatenate([stack(a_t), stack(v_ak)], axis=1))
    yield
    w = a_t + z2[:, :GROUP]
    u0 = v_ak + z2[:, GROUP:]

    s = state_ref[d, g]
    z3 = _dot_nt(jnp.concatenate([w, r_t], axis=0).astype(BF16), s.astype(BF16))
    yield
    u = z3[:c] + u0
    p_end = jnp.exp2(tot - cum)
    yu = _dot(a_rb, stack(u))
    upd = _dot_tn(jnp.concatenate([u, v], axis=0).astype(BF16),
                  jnp.concatenate([b_vec * p_end, k_d * p_end], axis=0).astype(BF16))
    yield
    y_ref[:, sl] = z3[c:] + yu + y0
    state_ref[d, g] = s * jnp.exp2(tot) + jnp.where(bmask, upd, 0.0)


def _interleave(chains, skew=0.0):
    pending = list(enumerate(chains))
    rnd = 0
    while pending:
        alive = []
        for i, ch in pending:
            if rnd >= (i % 2) * skew:
                try:
                    next(ch)
                except StopIteration:
                    continue
            alive.append((i, ch))
        pending = alive
        rnd += 1


def _scan_kernel(r0, k0, v0, lw0, as0, r1, k1, v1, lw1, as1, kks_ref, ka_ref, sin_ref,
                 y0_ref, y1_ref, state_ref, *, groups):
    i = pl.program_id(1)
    c = CHUNK
    unroll = math.gcd(groups, SCAN_UNROLL)

    @pl.when(i == 0)
    def _():
        state_ref[...] = sin_ref[...]

    bmask = _block_mask(GROUP)
    ones_bd = jnp.where(bmask, 1.0, 0.0).astype(BF16)
    tp = lax.broadcasted_iota(jnp.int32, (c, GROUP), 0)
    sp = lax.broadcasted_iota(jnp.int32, (c, GROUP), 1) % c
    m_strict = (sp < tp, sp > tp)
    m_incl = (sp <= tp, sp >= tp)
    consts = (bmask, ones_bd, m_strict, m_incl)

    def body(it, carry):
        chains = []
        for j in range(unroll):
            g = it * unroll + j
            sl = pl.ds(pl.multiple_of(g * GROUP, GROUP), GROUP)
            kks = kks_ref[:, sl]
            ka = ka_ref[:, sl]
            chains.append(_scan_chain(0, (r0, k0, v0, lw0, as0), kks, ka, state_ref, y0_ref, sl, g, consts))
            chains.append(_scan_chain(1, (r1, k1, v1, lw1, as1), kks, ka, state_ref, y1_ref, sl, g, consts))
        _interleave(chains, SCAN_SKEW)
        return carry

    lax.fori_loop(0, groups // unroll, body, 0)


def _rwkv_scan(r, k, v, lw0, as0, lw1, as1, kk_scale, ka, s_init):
    b, t, d = r.shape
    assert CHUNK * (GROUP // HEAD) == GROUP and t % CHUNK == 0 and d % GROUP == 0
    groups = d // GROUP
    nc = t // CHUNK
    fwd = pl.BlockSpec((None, CHUNK, d), lambda i, j: (i, j, 0))
    bwd = pl.BlockSpec((None, CHUNK, d), lambda i, j: (i, nc - 1 - j, 0))
    vec = pl.BlockSpec((1, d), lambda i, j: (0, 0))
    sspec = pl.BlockSpec((None, 2, groups, GROUP, GROUP), lambda i, j: (i, 0, 0, 0, 0))
    y = jax.ShapeDtypeStruct((b, t, d), F32)
    state_bytes = 2 * groups * GROUP * GROUP * 4
    vmem = 2 * 12 * CHUNK * d * 4 + 4 * state_bytes + 2 * SCAN_UNROLL * (3 << 19) + (4 << 20)
    return pl.pallas_call(
        functools.partial(_scan_kernel, groups=groups),
        grid=(b, nc),
        in_specs=[fwd] * 5 + [bwd] * 5 + [vec, vec, sspec],
        out_specs=[fwd, bwd, sspec],
        out_shape=[y, y, jax.ShapeDtypeStruct(s_init.shape, F32)],
        compiler_params=_cparams(("parallel", "arbitrary"), vmem),
        name="rwkv_scan",
    )(r, k, v, lw0, as0, r, k, v, lw1, as1, kk_scale.reshape(1, d), ka.reshape(1, d), s_init)


def _readout_kernel(y0_ref, y1_ref, r_ref, k_ref, v_ref, as0_ref, as1_ref, g_ref,
                    ka_ref, rk_ref, lnw_ref, lnb_ref, o_ref):
    d = o_ref.shape[-1]
    ones_bd = jnp.where(_block_mask(GROUP), 1.0, 0.0).astype(BF16)
    for j in range(d // GROUP):
        sl = slice(j * GROUP, (j + 1) * GROUP)
        y = y0_ref[:, sl] + y1_ref[:, sl]
        mean = _split_dot(y, ones_bd) * (1.0 / HEAD)
        yc = y - mean
        var = _split_dot(yc * yc, ones_bd) * (1.0 / HEAD)
        yn = (yc * lax.rsqrt(var + GN_EPS)) * lnw_ref[:, sl] + lnb_ref[:, sl]
        k = k_ref[:, sl]
        ka = ka_ref[:, sl]
        a0, a1 = as0_ref[:, sl].astype(F32), as1_ref[:, sl].astype(F32)
        k_sum = k * (1.0 + (a0 - 1.0) * ka) + k * (1.0 + (a1 - 1.0) * ka)
        bonus = _split_dot(r_ref[:, sl] * k_sum * rk_ref[:, sl], ones_bd) * v_ref[:, sl]
        o_ref[:, sl] = ((yn + bonus) * g_ref[:, sl].astype(F32)).astype(o_ref.dtype)


def _rwkv_readout(y0, y1, r, k, v, as0, as1, g, ka, rk, lnw, lnb):
    b, t, d = r.shape
    tm = min(t, 256)
    spec = pl.BlockSpec((None, tm, d), lambda i, j: (i, j, 0))
    vec = pl.BlockSpec((1, d), lambda i, j: (0, 0))
    return pl.pallas_call(
        _readout_kernel,
        grid=(b, t // tm),
        in_specs=[spec] * 8 + [vec] * 4,
        out_specs=spec,
        out_shape=jax.ShapeDtypeStruct((b, t, d), BF16),
        compiler_params=_cparams(("parallel", "parallel"), 2 * 9 * tm * d * 4 + (8 << 20)),
        name="rwkv_readout",
    )(y0, y1, r, k, v, as0, as1, g, ka.reshape(1, d), rk.reshape(1, d), lnw.reshape(1, d), lnb.reshape(1, d))


def _mm_res_kernel(z_ref, w_ref, res_ref, gate_ref, o_ref):
    o_ref[...] = res_ref[...] + gate_ref[...] * _dot(z_ref[...], w_ref[...])


def _norm_mm_kernel(x_ref, g_ref, sh_ref, sc_ref, w_ref, o_ref, h_ref):
    @pl.when(pl.program_id(2) == 0)
    def _():
        h_ref[...] = _norm_mod_value(x_ref[...], g_ref[...], sh_ref[...], sc_ref[...]).astype(BF16)

    o_ref[...] = _dot(h_ref[...], w_ref[...])


def _norm_matmul(x, g, sh, sc, w, col0=0):
    b, t, d = x.shape
    n = w.shape[1] - col0
    tm = min(t, 1024)
    tn = 512
    assert col0 % tn == 0
    cb0 = col0 // tn
    vmem = 2 * tm * d * 4 + tm * d * 2 + 2 * d * tn * 2 + 3 * tm * tn * 4 + tm * d * 4 + (4 << 20)
    return pl.pallas_call(
        _norm_mm_kernel,
        grid=(b, t // tm, n // tn),
        in_specs=[pl.BlockSpec((None, tm, d), lambda i, j, l: (i, j, 0))] + _norm_specs(g, sh, sc, d)
                 + [pl.BlockSpec((d, tn), lambda i, j, l: (0, l + cb0))],
        out_specs=pl.BlockSpec((None, tm, tn), lambda i, j, l: (i, j, l)),
        out_shape=jax.ShapeDtypeStruct((b, t, n), F32),
        scratch_shapes=[pltpu.VMEM((tm, d), BF16)],
        compiler_params=_cparams(("parallel", "parallel", "arbitrary"), vmem),
        name="norm_matmul",
    )(x, g.reshape(1, d), sh, sc, w)


def _matmul_res(z, w, res, gate):
    b, t, kdim = z.shape
    n = w.shape[1]
    tm = min(t, 1024)
    tn = 512
    zspec = pl.BlockSpec((None, tm, kdim), lambda i, j, l: (i, j, 0))
    wspec = pl.BlockSpec((kdim, tn), lambda i, j, l: (0, l))
    ospec = pl.BlockSpec((None, tm, tn), lambda i, j, l: (i, j, l))
    gspec = (pl.BlockSpec((None, 1, tn), lambda i, j, l: (0, 0, l)) if gate.shape[0] == 1
             else pl.BlockSpec((None, 1, tn), lambda i, j, l: (i, 0, l)))
    vmem = 2 * (tm * kdim * 2 + kdim * tn * 2 + 2 * tm * tn * 4) + tm * tn * 4 + (4 << 20)
    return pl.pallas_call(
        _mm_res_kernel,
        grid=(b, t // tm, n // tn),
        in_specs=[zspec, wspec, ospec, gspec],
        out_specs=ospec,
        out_shape=jax.ShapeDtypeStruct((b, t, n), F32),
        compiler_params=_cparams(("parallel", "parallel", "parallel"), vmem),
        name="matmul_res",
    )(z, w, res, gate)


def _ffn_kernel(x_ref, g_ref, sh_ref, sc_ref, gate_ref, wg_ref, wu_ref, wd_ref, o_ref, h_ref):
    f = pl.program_id(2)

    @pl.when(f == 0)
    def _():
        h_ref[...] = _norm_mod_value(x_ref[...], g_ref[...], sh_ref[...], sc_ref[...]).astype(BF16)
        o_ref[...] = jnp.zeros_like(o_ref)

    h = h_ref[...]
    gp = _dot(h, wg_ref[...])
    act = ((gp * jax.nn.sigmoid(gp)) * _dot(h, wu_ref[...])).astype(BF16)
    d = o_ref.shape[-1]
    for c0 in range(0, d, FFN_ACC_COLS):
        cols = slice(c0, c0 + FFN_ACC_COLS)
        o_ref[:, cols] += _dot(act, wd_ref[:, cols])

    @pl.when(f == pl.num_programs(2) - 1)
    def _():
        o_ref[...] = x_ref[...] + gate_ref[...] * o_ref[...]


def _ffn(x, g, sh, sc, gate, wg, wu, wd):
    b, t, d = x.shape
    ff = wg.shape[1]
    tm = min(t, 512)
    tf = 512
    xspec = pl.BlockSpec((None, tm, d), lambda i, j, f: (i, j, 0))
    vmem = 4 * tm * d * 4 + tm * d * 2 + 2 * 3 * d * tf * 2 + 6 * tm * tf * 4 + tm * d * 4 + (4 << 20)
    return pl.pallas_call(
        _ffn_kernel,
        grid=(b, t // tm, ff // tf),
        in_specs=[xspec, pl.BlockSpec((1, d), lambda i, j, f: (0, 0)),
                  _row_spec(sh, d), _row_spec(sc, d), _row_spec(gate, d),
                  pl.BlockSpec((d, tf), lambda i, j, f: (0, f)),
                  pl.BlockSpec((d, tf), lambda i, j, f: (0, f)),
                  pl.BlockSpec((tf, d), lambda i, j, f: (f, 0))],
        out_specs=xspec,
        out_shape=jax.ShapeDtypeStruct((b, t, d), F32),
        scratch_shapes=[pltpu.VMEM((tm, d), BF16)],
        compiler_params=_cparams(("parallel", "parallel", "arbitrary"), vmem),
        name="ffn",
    )(x, g.reshape(1, d), sh, sc, gate, wg, wu, wd)


def _qk_norm(x, gain, ones_bd):
    ms = _split_dot(x * x, ones_bd) * (1.0 / HEAD)
    return x * lax.rsqrt(ms + QK_EPS) * gain


def _rope(u, cos, sin):
    n = u.shape[-1]
    lane = lax.broadcasted_iota(jnp.int32, u.shape, 1)
    partner = jnp.where(lane % 32 < 16, pltpu.roll(u, n - 16, 1), pltpu.roll(u, 16, 1))
    return u * cos + partner * sin


def _qkv_prep_kernel(*refs, rope, with_q):
    refs = list(refs)
    q_ref = refs.pop(0) if with_q else None
    k_ref, v_ref = refs.pop(0), refs.pop(0)
    qn_ref = refs.pop(0) if with_q else None
    kn_ref = refs.pop(0)
    if rope:
        cos_ref, sin_ref = refs.pop(0), refs.pop(0)
    qo_ref = refs.pop(0) if with_q else None
    ko_ref, vo_ref = refs
    d = ko_ref.shape[-1]
    ones_bd = jnp.where(_block_mask(GROUP), 1.0, 0.0).astype(BF16)
    if rope:
        cos, sin = cos_ref[...], sin_ref[...]
    for j in range(d // GROUP):
        sl = slice(j * GROUP, (j + 1) * GROUP)
        kh = _qk_norm(k_ref[:, sl], kn_ref[:, sl], ones_bd)
        if rope:
            kh = _rope(kh, cos, sin)
        ko_ref[:, sl] = kh.astype(BF16)
        if with_q:
            qh = _qk_norm(q_ref[:, sl], qn_ref[:, sl], ones_bd)
            if rope:
                qh = _rope(qh, cos, sin)
            qo_ref[:, sl] = (qh * (HEAD ** -0.5 * LOG2E)).astype(BF16)
    vo_ref[...] = v_ref[...].T.astype(BF16)


def _qkv_prep(qkv, qn, kn, cos, sin, *, with_q):
    b, t, width = qkv.shape
    parts = 3 if with_q else 2
    d = width // parts
    rope = cos is not None
    tm = min(t, 256)
    col = lambda c: pl.BlockSpec((None, tm, d), lambda i, j, _c=c: (i, j, _c))
    vec = pl.BlockSpec((1, d), lambda i, j: (0, 0))
    tab = pl.BlockSpec((tm, GROUP), lambda i, j: (j, 0))
    in_specs = [col(c) for c in range(parts)] + [vec] * (2 if with_q else 1) + ([tab, tab] if rope else [])
    args = [qkv] * parts + ([qn] if with_q else []) + [kn] + ([cos, sin] if rope else [])
    out = jax.ShapeDtypeStruct((b, t, d), BF16)
    ospec = pl.BlockSpec((None, tm, d), lambda i, j: (i, j, 0))
    out_t = jax.ShapeDtypeStruct((b, d, t), BF16)
    ospec_t = pl.BlockSpec((None, d, tm), lambda i, j: (i, 0, j))
    return pl.pallas_call(
        functools.partial(_qkv_prep_kernel, rope=rope, with_q=with_q),
        grid=(b, t // tm),
        in_specs=in_specs,
        out_specs=[ospec] * (parts - 1) + [ospec_t],
        out_shape=[out] * (parts - 1) + [out_t],
        compiler_params=_cparams(("parallel", "parallel"), 2 * parts * tm * d * 6 + (16 << 20)),
        name="qkv_prep",
    )(*args)


def _attn_chain(q2, k, vt, state, shift):
    m, acc_t = state
    s_t = _dot_nt(k, q2)
    yield
    if shift is None:
        m_new = jnp.maximum(m, jnp.max(s_t, axis=0, keepdims=True))
        alpha = jnp.exp2(m - m_new)
        state[0] = m_new
        p = jnp.exp2(s_t - m_new)
    else:
        p = jnp.exp2(s_t - shift)
    pv = _dot(vt, p.astype(BF16))
    yield
    state[1] = pv + (acc_t if shift is not None else alpha * acc_t)


def _attn_kernel(lam_ref, q_ref, kc_ref, vtc_ref, kl_ref, vtl_ref, sub_ref, o_ref, *, tk, tq, lambda_init):
    n_sub = q_ref.shape[0] // tq
    lp = lam_ref[...]
    lam = (jnp.exp(jnp.sum(lp[0:1] * lp[1:2], axis=-1, keepdims=True))
           - jnp.exp(jnp.sum(lp[2:3] * lp[3:4], axis=-1, keepdims=True)) + lambda_init)
    bound = (HEAD ** 0.5 * LOG2E * SCORE_MARGIN) * (jnp.max(jnp.abs(lp[4:5]), axis=-1, keepdims=True)
                                                    * jnp.max(jnp.abs(lp[5:6]), axis=-1, keepdims=True))
    fixed_shift_ok = bound[0, 0] <= MAX_FIXED_SHIFT
    q2s = []
    for c in range(n_sub):
        q = q_ref[c * tq:(c + 1) * tq, :]
        lane = lax.broadcasted_iota(jnp.int32, q.shape, 1)
        zero = jnp.zeros_like(q)
        q2s.append(jnp.concatenate([jnp.where(lane < HEAD, q, zero), jnp.where(lane >= HEAD, q, zero)], axis=0))

    hw = 2 * HEAD
    ones_rows = 16

    def run(shift):
        def step(k, v_t, carry):
            vt = jnp.concatenate([v_t, jnp.ones((ones_rows, k.shape[0]), BF16)], axis=0)
            states = [list(st) for st in carry]
            _interleave([_attn_chain(q2, k, vt, st, shift) for q2, st in zip(q2s, states)])
            return tuple(tuple(st) for st in states)

        def latent_step(j, carry):
            ks = pl.ds(pl.multiple_of(j * tk, tk), tk)
            return step(kl_ref[ks, :], vtl_ref[:, ks], carry)

        init = (jnp.full((1, 2 * tq), -jnp.inf, F32), jnp.zeros((hw + ones_rows, 2 * tq), F32))
        carry = step(kc_ref[...], vtc_ref[...], (init,) * n_sub)
        carry = lax.fori_loop(0, kl_ref.shape[0] // tk, latent_step, carry)
        for c, (_, acc) in enumerate(carry):
            acc_t, l = acc[:hw], acc[hw:hw + 1]
            o_t = acc_t[:, :tq] / l[:, :tq] - lam * (acc_t[:, tq:] / l[:, tq:])
            o = o_t.T
            o = o * lax.rsqrt(jnp.mean(o * o, axis=-1, keepdims=True) + SUBLN_EPS) * sub_ref[...]
            o_ref[c * tq:(c + 1) * tq, :] = (o * (1.0 - lambda_init)).astype(o_ref.dtype)

    @pl.when(fixed_shift_ok)
    def _():
        run(bound)

    @pl.when(jnp.logical_not(fixed_shift_ok))
    def _():
        run(None)


def _diff_attention(q, k_c, vt_c, k_l, vt_l, lam_params, sub_gain, lambda_init):
    b, t, d = q.shape
    tc = k_c.shape[1]
    hw = 2 * HEAD
    tq = 256
    n_sub = math.gcd(t // tq, ATTN_CHAINS)
    tk = min(t, 2048)
    assert t % (tq * n_sub) == 0 and t % tk == 0 and tc % 128 == 0
    qspec = pl.BlockSpec((None, tq * n_sub, hw), lambda i, h, j: (i, j, h))

    def kspec(n):
        return pl.BlockSpec((None, n, hw), lambda i, h, j: (i, 0, h))

    def vspec(n):
        return pl.BlockSpec((None, hw, n), lambda i, h, j: (i, h, 0))

    vmem = 4 * (t + tc) * hw * 2 + 4 * n_sub * tq * hw * 2 + n_sub * 8 * 2 * tq * tk * 4 + (8 << 20)
    return pl.pallas_call(
        functools.partial(_attn_kernel, tk=tk, tq=tq, lambda_init=lambda_init),
        grid=(b, d // hw, t // (tq * n_sub)),
        in_specs=[pl.BlockSpec(lam_params.shape, lambda i, h, j: (0, 0)), qspec,
                  kspec(tc), vspec(tc), kspec(t), vspec(t), pl.BlockSpec((1, hw), lambda i, h, j: (0, 0))],
        out_specs=qspec,
        out_shape=jax.ShapeDtypeStruct((b, t, d), BF16),
        compiler_params=_cparams(("parallel", "parallel", "parallel"), vmem),
        name="diff_attention",
    )(lam_params, q, k_c, vt_c, k_l, vt_l, sub_gain.reshape(1, hw))


def _rope_tables(n_tokens):
    n_freq = HEAD // 4
    pos = jnp.arange(n_tokens)
    freqs = ROPE_BASE ** (-jnp.arange(n_freq, dtype=F32) / n_freq)
    ang_row = (pos // GRID_W).astype(F32)[:, None] * freqs
    ang_col = (pos % GRID_W).astype(F32)[:, None] * freqs
    cos = jnp.concatenate([jnp.cos(ang_row)] * 2 + [jnp.cos(ang_col)] * 2, axis=-1)
    sin = jnp.concatenate([-jnp.sin(ang_row), jnp.sin(ang_row), -jnp.sin(ang_col), jnp.sin(ang_col)], axis=-1)
    reps = GROUP // HEAD
    return jnp.tile(cos, (1, reps)), jnp.tile(sin, (1, reps))


def _mod_rows(mod, layer, j, d, batch):
    m = mod[layer, :, j * d:(j + 1) * d]
    return m[:batch, None, :], m[batch:batch + 1, None, :]


def _rwkv_layer(xs, mods, norm_g, p):
    rkv, lora, outs = [], [], []
    for x, (sh, sc, _) in zip(xs, mods):
        rkv.append(_rwkv_rkv(x, norm_g, sh, sc, p["mu"], p["wr"], p["wk"], p["wv"]))
        lora.append(_rwkv_lora(x, norm_g, sh, sc, p["mu"], p["w0"], p["w1"], p["w2"],
                               p["a0"], p["a1"], p["a2"], p["g1"], p["g2"]))
    b, _, d = xs[0].shape
    state = jnp.zeros((b, 2, d // GROUP, GROUP, GROUP), F32)
    for x, (_, _, gate), (r, k, v), (g, lw0, lw1, as0, as1) in zip(xs, mods, rkv, lora):
        y0, y1, state = _rwkv_scan(r, k, v, lw0, as0, lw1, as1, p["kk"], p["ka"], state)
        z = _rwkv_readout(y0, y1, r, k, v, as0, as1, g, p["ka"], p["rk"], p["lnw"], p["lnb"])
        outs.append(_matmul_res(z, p["wo"], x, gate))
    return outs


def kernel(x, c, ctx, c_ctx, mod_w, mod_b, norm1_g, norm2_g, rwkv_mu, rwkv_wr, rwkv_wk, rwkv_wv, rwkv_wo, rwkv_w0, rwkv_w1, rwkv_w2, rwkv_a0, rwkv_a1, rwkv_a2, rwkv_g1, rwkv_g2, rwkv_kk, rwkv_ka, rwkv_rk, rwkv_lnw, rwkv_lnb, diff_wqkv, diff_qn, diff_kn, diff_lq1, diff_lk1, diff_lq2, diff_lk2, diff_subln, diff_wo, ffn_wg, ffn_wu, ffn_wd):
    batch, n_lat, d = x.shape
    depth = mod_w.shape[0]
    assert depth == 2, "layer 0 is the RWKV-7 mixer, layer 1 differential attention"

    cvecs = jnp.zeros((SUBLANES, d), F32).at[:batch].set(c).at[batch].set(c_ctx)
    mod = _adaln(cvecs, mod_w, mod_b)

    def mods(layer, first):
        lat, cx = zip(*[_mod_rows(mod, layer, first + j, d, batch) for j in range(3)])
        return cx, lat

    w1, w2 = _pad_lora(rwkv_w1[0], rwkv_w2[0])
    a1, a2 = _pad_lora(rwkv_a1[0], rwkv_a2[0])
    g1, g2 = _pad_lora(rwkv_g1[0], rwkv_g2[0])
    p = dict(mu=rwkv_mu[0], wr=rwkv_wr[0].astype(BF16), wk=rwkv_wk[0].astype(BF16), wv=rwkv_wv[0].astype(BF16),
             wo=rwkv_wo[0].astype(BF16), w0=rwkv_w0[0], w1=w1, w2=w2, a0=rwkv_a0[0], a1=a1, a2=a2,
             g1=g1, g2=g2, kk=rwkv_kk[0], ka=rwkv_ka[0],
             rk=rwkv_rk[0], lnw=rwkv_lnw[0], lnb=rwkv_lnb[0])
    xc, xl = _rwkv_layer((ctx, x), mods(0, 0), norm1_g[0], p)
    wg, wu, wd = ffn_wg[0].astype(BF16), ffn_wu[0].astype(BF16), ffn_wd[0].astype(BF16)
    (csh, csc, cgt), (lsh, lsc, lgt) = mods(0, 3)
    xc = _ffn(xc, norm2_g[0], csh, csc, cgt, wg, wu, wd)
    xl = _ffn(xl, norm2_g[0], lsh, lsc, lgt, wg, wu, wd)

    (csh, csc, _), (lsh, lsc, lgt) = mods(1, 0)
    lambda_init = 0.8 - 0.6 * math.exp(-0.3 * 1)
    wqkv = diff_wqkv[0].astype(BF16)
    qn = jnp.tile(diff_qn[0], d // HEAD).reshape(1, d)
    kn = jnp.tile(diff_kn[0], d // HEAD).reshape(1, d)
    cos, sin = _rope_tables(n_lat)
    q_l, k_l, v_l = _qkv_prep(_norm_matmul(xl, norm1_g[1], lsh, lsc, wqkv), qn, kn, cos, sin, with_q=True)
    k_c, v_c = _qkv_prep(_norm_matmul(xc, norm1_g[1], csh, csc, wqkv, col0=d), None, kn, None, None, with_q=False)
    lam_params = jnp.stack([diff_lq1[0], diff_lk1[0], diff_lq2[0], diff_lk2[0], diff_qn[0], diff_kn[0]])
    o = _diff_attention(q_l, k_c, v_c, k_l, v_l, lam_params, diff_subln[0], lambda_init)
    xl = _matmul_res(o, diff_wo[0].astype(BF16), xl, lgt)
    (_, _, _), (lsh, lsc, lgt) = mods(1, 3)
    return _ffn(xl, norm2_g[1], lsh, lsc, lgt,
                ffn_wg[1].astype(BF16), ffn_wu[1].astype(BF16), ffn_wd[1].astype(BF16))
```

```python
import functools
import math

import jax
import jax.numpy as jnp
from jax import lax
from jax.experimental import pallas as pl
from jax.experimental.pallas import tpu as pltpu

F32 = jnp.float32
BF16 = jnp.bfloat16

HEAD = 64
GROUP = 256
CHUNK = 64
SCAN_UNROLL = 8
SCAN_SKEW = 4
ATTN_CHAINS = 4
FFN_ACC_COLS = 512
SUBLANES = 8
V7X_VMEM_CAP = 56 * 1024 * 1024

NORM_EPS = 1e-6
GN_EPS = 64e-5
L2_EPS = 1e-24
QK_EPS = 1e-6
SUBLN_EPS = 1e-5
ROPE_BASE = 10000.0
GRID_W = 64
DECAY_SCALE = math.exp(-0.5)
LOG2E = math.log2(math.e)
SCORE_MARGIN = 1.01
MAX_FIXED_SHIFT = 50.0


def _cparams(semantics, vmem_bytes):
    return pltpu.CompilerParams(dimension_semantics=semantics,
                                vmem_limit_bytes=int(min(max(vmem_bytes, 16 << 20), V7X_VMEM_CAP)))


def _dot(a, b):
    return jnp.dot(a, b, preferred_element_type=F32)


def _dot_nt(a, b):
    return lax.dot_general(a, b, (((1,), (1,)), ((), ())), preferred_element_type=F32)


def _dot_tn(a, b):
    return lax.dot_general(a, b, (((0,), (0,)), ((), ())), preferred_element_type=F32)


def _split_dot(x, w_bf16):
    hi = x.astype(BF16)
    lo = (x - hi.astype(F32)).astype(BF16)
    return _dot(hi, w_bf16) + _dot(lo, w_bf16)


def _block_mask(n):
    r = lax.broadcasted_iota(jnp.int32, (n, n), 0) // HEAD
    c = lax.broadcasted_iota(jnp.int32, (n, n), 1) // HEAD
    return r == c


def _adaln_kernel(c_ref, w_ref, b_ref, o_ref):
    c = c_ref[...]
    s = (c * jax.nn.sigmoid(c)).astype(BF16)
    o_ref[...] = _dot(s, w_ref[...].astype(BF16)) + b_ref[...]


def _adaln(cvecs, mod_w, mod_b):
    depth, d, n = mod_w.shape
    rows = cvecs.shape[0]
    tn = 1024
    return pl.pallas_call(
        _adaln_kernel,
        grid=(depth, n // tn),
        in_specs=[pl.BlockSpec((rows, d), lambda l, j: (0, 0)),
                  pl.BlockSpec((None, d, tn), lambda l, j: (l, 0, j)),
                  pl.BlockSpec((None, 1, tn), lambda l, j: (l, 0, j))],
        out_specs=pl.BlockSpec((None, rows, tn), lambda l, j: (l, 0, j)),
        out_shape=jax.ShapeDtypeStruct((depth, rows, n), F32),
        compiler_params=_cparams(("parallel", "parallel"), 3 * d * tn * 4 + (8 << 20)),
        name="adaln",
    )(cvecs, mod_w, mod_b.reshape(depth, 1, n))


def _norm_mod_value(x, g, sh, sc):
    y = x * lax.rsqrt(jnp.mean(x * x, axis=-1, keepdims=True) + NORM_EPS)
    return (y * g) * (1.0 + sc) + sh


def _row_spec(arr, d):
    if arr.shape[0] == 1:
        return pl.BlockSpec((None, 1, d), lambda b, *_: (0, 0, 0))
    return pl.BlockSpec((None, 1, d), lambda b, *_: (b, 0, 0))


def _norm_specs(g, sh, sc, d):
    return [pl.BlockSpec((1, d), lambda b, *_: (0, 0)), _row_spec(sh, d), _row_spec(sc, d)]


def _mix_kernel(x_ref, xp_ref, xn_ref, g_ref, sh_ref, sc_ref, mu_ref, *o_refs, nt):
    t = pl.program_id(1)
    tm, d = x_ref.shape

    def inv_rms(ref):
        x = ref[...]
        return lax.rsqrt(jnp.mean(x * x, axis=-1, keepdims=True) + NORM_EPS)

    r, r_prev, r_next = inv_rms(x_ref), inv_rms(xp_ref), inv_rms(xn_ref)
    row = lax.broadcasted_iota(jnp.int32, (tm, GROUP), 0)
    for c0 in range(0, d, GROUP):
        sl = slice(c0, c0 + GROUP)
        g, sh, sc = g_ref[:, sl], sh_ref[:, sl], sc_ref[:, sl]

        def norm(ref, rinv):
            return ((ref[:, sl] * rinv) * g) * (1.0 + sc) + sh

        h = norm(x_ref, r)
        prev_row = jnp.where(t > 0, norm(xp_ref, r_prev)[SUBLANES - 1:SUBLANES, :], 0.0)
        next_row = jnp.where(t < nt - 1, norm(xn_ref, r_next)[0:1, :], 0.0)
        hp = jnp.where(row == 0, prev_row, pltpu.roll(h, 1, 0))
        hn = jnp.where(row == tm - 1, next_row, pltpu.roll(h, tm - 1, 0))
        xx = 0.5 * (hp + hn) - h
        for j, o_ref in enumerate(o_refs):
            o_ref[:, sl] = (h + xx * mu_ref[j:j + 1, sl]).astype(o_ref.dtype)


def _rwkv_mix(x, g, sh, sc, mu):
    b, t, d = x.shape
    n_mix = mu.shape[0]
    tm = min(t, 256)
    nt = t // tm
    per = tm // SUBLANES
    last = t // SUBLANES - 1
    spec = pl.BlockSpec((None, tm, d), lambda i, j: (i, j, 0))
    halo = [pl.BlockSpec((None, SUBLANES, d), lambda i, j: (i, jnp.maximum(j * per - 1, 0), 0)),
            pl.BlockSpec((None, SUBLANES, d), lambda i, j: (i, jnp.minimum((j + 1) * per, last), 0))]
    return pl.pallas_call(
        functools.partial(_mix_kernel, nt=nt),
        grid=(b, nt),
        in_specs=[spec] + halo + _norm_specs(g, sh, sc, d) + [pl.BlockSpec(mu.shape, lambda i, j: (0, 0))],
        out_specs=[spec] * n_mix,
        out_shape=[jax.ShapeDtypeStruct((b, t, d), BF16)] * n_mix,
        compiler_params=_cparams(("parallel", "parallel"), 2 * tm * d * (4 + 2 * n_mix) + 8 * tm * d * 4),
        name="rwkv_mix",
    )(x, x, x, g.reshape(1, d), sh, sc, mu)


def _rkv_kernel(xr_ref, xk_ref, xv_ref, wr_ref, wk_ref, wv_ref, r_ref, k_ref, v_ref):
    r_ref[...] = _dot(xr_ref[...], wr_ref[...])
    k_ref[...] = _dot(xk_ref[...], wk_ref[...])
    v_ref[...] = _dot(xv_ref[...], wv_ref[...])


def _rwkv_rkv(xr, xk, xv, wr, wk, wv):
    b, t, d = xr.shape
    tm = min(t, 1024)
    tn = 512
    out = jax.ShapeDtypeStruct((b, t, d), F32)
    xspec = pl.BlockSpec((None, tm, d), lambda i, j, n: (i, j, 0))
    wspec = pl.BlockSpec((d, tn), lambda i, j, n: (0, n))
    ospec = pl.BlockSpec((None, tm, tn), lambda i, j, n: (i, j, n))
    vmem = 6 * tm * d * 2 + 6 * d * tn * 2 + 9 * tm * tn * 4 + (4 << 20)
    return pl.pallas_call(
        _rkv_kernel,
        grid=(b, t // tm, d // tn),
        in_specs=[xspec, xspec, xspec, wspec, wspec, wspec],
        out_specs=[ospec, ospec, ospec],
        out_shape=[out, out, out],
        compiler_params=_cparams(("parallel", "parallel", "parallel"), vmem),
        name="rwkv_rkv",
    )(xr, xk, xv, wr, wk, wv)


def _lora_kernel(xw_ref, xa_ref, xg_ref, w0_ref, w1_ref, w2_ref, a0_ref, a1_ref, a2_ref, g1_ref, g2_ref,
                 g_ref, lw0_ref, lw1_ref, as0_ref, as1_ref):
    xw, xa, xg = xw_ref[...], xa_ref[...], xg_ref[...]
    g_ref[...] = _dot(jax.nn.sigmoid(_dot(xg, g1_ref[...])).astype(BF16), g2_ref[...]).astype(g_ref.dtype)
    for d, (lw_ref, as_ref) in enumerate(((lw0_ref, as0_ref), (lw1_ref, as1_ref))):
        w_pre = w0_ref[d:d + 1, :] + _dot(jnp.tanh(_dot(xw, w1_ref[d])).astype(BF16), w2_ref[d])
        lw_ref[...] = (-DECAY_SCALE * LOG2E) * jax.nn.sigmoid(w_pre)
        a_pre = a0_ref[d:d + 1, :] + _dot(_dot(xa, a1_ref[d]).astype(BF16), a2_ref[d])
        as_ref[...] = jax.nn.sigmoid(a_pre).astype(as_ref.dtype)


def _pad_lora(w_in, w_out):
    rank = w_in.shape[-1]
    pad = (-rank) % 128
    w_in = jnp.pad(w_in, [(0, 0)] * (w_in.ndim - 1) + [(0, pad)])
    w_out = jnp.pad(w_out, [(0, 0)] * (w_out.ndim - 2) + [(0, pad), (0, 0)])
    return w_in.astype(BF16), w_out.astype(BF16)


def _rwkv_lora(xw, xa, xg, w0, w1, w2, a0, a1, a2, g1, g2):
    b, t, d = xw.shape
    tm = min(t, 256)
    out = jax.ShapeDtypeStruct((b, t, d), F32)
    out16 = jax.ShapeDtypeStruct((b, t, d), BF16)
    spec = pl.BlockSpec((None, tm, d), lambda i, j: (i, j, 0))

    def full(a):
        return pl.BlockSpec(a.shape, lambda i, j, _n=a.ndim: (0,) * _n)

    params = (w0, w1, w2, a0, a1, a2, g1, g2)
    vmem = 2 * tm * d * (3 * 2 + 2 * 4 + 3 * 2) + 10 * tm * d * 4 + 2 * sum(p.size * p.dtype.itemsize for p in params)
    return pl.pallas_call(
        _lora_kernel,
        grid=(b, t // tm),
        in_specs=[spec] * 3 + [full(p) for p in params],
        out_specs=[spec] * 5,
        out_shape=[out16, out, out, out16, out16],
        compiler_params=_cparams(("parallel", "parallel"), vmem),
        name="rwkv_lora",
    )(xw, xa, xg, *params)


def _chunk_cumsum(x, reverse):
    n = x.shape[0]
    row = lax.broadcasted_iota(jnp.int32, x.shape, 0)
    shift = 1
    while shift < n:
        if reverse:
            x = x + jnp.where(row < n - shift, pltpu.roll(x, n - shift, 0), 0.0)
        else:
            x = x + jnp.where(row >= shift, pltpu.roll(x, shift, 0), 0.0)
        shift *= 2
    return x


def _scan_chain(d, refs, kks, ka, state_ref, y_ref, sl, g, consts):
    r_ref, k_ref, v_ref, lw_ref, as_ref = refs
    bmask, ones_bd, m_strict, m_incl = consts
    c = CHUNK

    def stack(x):
        return jnp.where(bmask, jnp.concatenate([x] * (GROUP // c), axis=0), 0.0).astype(BF16)

    lw = lw_ref[:, sl]
    a_sig = as_ref[:, sl].astype(F32)
    k = k_ref[:, sl]
    kk = k * kks
    ssq = _dot((kk * kk).astype(BF16), ones_bd)
    cum = _chunk_cumsum(lw, reverse=d == 1)
    yield

    kk = kk * lax.rsqrt(jnp.maximum(ssq, L2_EPS))
    b_vec = kk * a_sig
    k_d = k * (1.0 + (a_sig - 1.0) * ka)
    tot = cum[c - 1:c, :] if d == 0 else cum[0:1, :]
    a_t = -kk * jnp.exp2(cum - lw)
    r_t = r_ref[:, sl] * jnp.exp2(cum)
    p_inv = jnp.exp2(-cum)
    lhs = jnp.concatenate([a_t, r_t], axis=0).astype(BF16)
    rhs = jnp.concatenate([stack(b_vec * p_inv), stack(k_d * p_inv)], axis=0)
    aa = _dot_nt(lhs, rhs)
    yield

    n1 = jnp.where(m_strict[d], aa[:c, :GROUP], 0.0)
    a_ak = jnp.where(m_strict[d], aa[:c, GROUP:], 0.0)
    a_rb = jnp.where(m_incl[d], aa[c:, :GROUP], 0.0).astype(BF16)
    a_rk = jnp.where(m_incl[d], aa[c:, GROUP:], 0.0)
    v = v_ref[:, sl]
    z1 = _dot(jnp.concatenate([a_ak, a_rk], axis=0).astype(BF16), stack(v))
    q = n1
    npow = _dot(n1.astype(BF16), stack(n1))
    yield
    span = 2
    while 2 * span < c:
        z = _dot(jnp.concatenate([npow, q], axis=0).astype(BF16), stack(npow))
        yield
        q = q + npow + z[c:]
        npow = z[:c]
        span *= 2
    zq = _dot(q.astype(BF16), stack(npow))
    yield
    q = q + npow + zq

    v_ak = z1[:c]
    y0 = z1[c:]
    z2 = _dot(q.astype(BF16), jnp.concatenate([stack(a_t), stack(v_ak)], axis=1))
    yield
    w = a_t + z2[:, :GROUP]
    u0 = v_ak + z2[:, GROUP:]

    s = state_ref[d, g]
    z3 = _dot_nt(jnp.concatenate([w, r_t], axis=0).astype(BF16), s.astype(BF16))
    yield
    u = z3[:c] + u0
    p_end = jnp.exp2(tot - cum)
    yu = _dot(a_rb, stack(u))
    upd = _dot_tn(jnp.concatenate([u, v], axis=0).astype(BF16),
                  jnp.concatenate([b_vec * p_end, k_d * p_end], axis=0).astype(BF16))
    yield
    y_ref[:, sl] = z3[c:] + yu + y0
    state_ref[d, g] = s * jnp.exp2(tot) + jnp.where(bmask, upd, 0.0)


def _interleave(chains, skew=0.0):
    pending = list(enumerate(chains))
    rnd = 0
    while pending:
        alive = []
        for i, ch in pending:
            if rnd >= (i % 2) * skew:
                try:
                    next(ch)
                except StopIteration:
                    continue
            alive.append((i, ch))
        pending = alive
        rnd += 1


def _scan_kernel(r0, k0, v0, lw0, as0, r1, k1, v1, lw1, as1, kks_ref, ka_ref, sin_ref,
                 y0_ref, y1_ref, state_ref, *, groups):
    i = pl.program_id(1)
    c = CHUNK
    unroll = math.gcd(groups, SCAN_UNROLL)

    @pl.when(i == 0)
    def _():
        state_ref[...] = sin_ref[...]

    bmask = _block_mask(GROUP)
    ones_bd = jnp.where(bmask, 1.0, 0.0).astype(BF16)
    tp = lax.broadcasted_iota(jnp.int32, (c, GROUP), 0)
    sp = lax.broadcasted_iota(jnp.int32, (c, GROUP), 1) % c
    m_strict = (sp < tp, sp > tp)
    m_incl = (sp <= tp, sp >= tp)
    consts = (bmask, ones_bd, m_strict, m_incl)

    def body(it, carry):
        chains = []
        for j in range(unroll):
            g = it * unroll + j
            sl = pl.ds(pl.multiple_of(g * GROUP, GROUP), GROUP)
            kks = kks_ref[:, sl]
            ka = ka_ref[:, sl]
            chains.append(_scan_chain(0, (r0, k0, v0, lw0, as0), kks, ka, state_ref, y0_ref, sl, g, consts))
            chains.append(_scan_chain(1, (r1, k1, v1, lw1, as1), kks, ka, state_ref, y1_ref, sl, g, consts))
        _interleave(chains, SCAN_SKEW)
        return carry

    lax.fori_loop(0, groups // unroll, body, 0)


def _rwkv_scan(r, k, v, lw0, as0, lw1, as1, kk_scale, ka, s_init):
    b, t, d = r.shape
    assert CHUNK * (GROUP // HEAD) == GROUP and t % CHUNK == 0 and d % GROUP == 0
    groups = d // GROUP
    nc = t // CHUNK
    fwd = pl.BlockSpec((None, CHUNK, d), lambda i, j: (i, j, 0))
    bwd = pl.BlockSpec((None, CHUNK, d), lambda i, j: (i, nc - 1 - j, 0))
    vec = pl.BlockSpec((1, d), lambda i, j: (0, 0))
    sspec = pl.BlockSpec((None, 2, groups, GROUP, GROUP), lambda i, j: (i, 0, 0, 0, 0))
    y = jax.ShapeDtypeStruct((b, t, d), F32)
    state_bytes = 2 * groups * GROUP * GROUP * 4
    vmem = 2 * 12 * CHUNK * d * 4 + 4 * state_bytes + 2 * SCAN_UNROLL * (3 << 19) + (4 << 20)
    return pl.pallas_call(
        functools.partial(_scan_kernel, groups=groups),
        grid=(b, nc),
        in_specs=[fwd] * 5 + [bwd] * 5 + [vec, vec, sspec],
        out_specs=[fwd, bwd, sspec],
        out_shape=[y, y, jax.ShapeDtypeStruct(s_init.shape, F32)],
        compiler_params=_cparams(("parallel", "arbitrary"), vmem),
        name="rwkv_scan",
    )(r, k, v, lw0, as0, r, k, v, lw1, as1, kk_scale.reshape(1, d), ka.reshape(1, d), s_init)


def _readout_kernel(y0_ref, y1_ref, r_ref, k_ref, v_ref, as0_ref, as1_ref, g_ref,
                    ka_ref, rk_ref, lnw_ref, lnb_ref, o_ref):
    d = o_ref.shape[-1]
    ones_bd = jnp.where(_block_mask(GROUP), 1.0, 0.0).astype(BF16)
    for j in range(d // GROUP):
        sl = slice(j * GROUP, (j + 1) * GROUP)
        y = y0_ref[:, sl] + y1_ref[:, sl]
        mean = _split_dot(y, ones_bd) * (1.0 / HEAD)
        yc = y - mean
        var = _split_dot(yc * yc, ones_bd) * (1.0 / HEAD)
        yn = (yc * lax.rsqrt(var + GN_EPS)) * lnw_ref[:, sl] + lnb_ref[:, sl]
        k = k_ref[:, sl]
        ka = ka_ref[:, sl]
        a0, a1 = as0_ref[:, sl].astype(F32), as1_ref[:, sl].astype(F32)
        k_sum = k * (1.0 + (a0 - 1.0) * ka) + k * (1.0 + (a1 - 1.0) * ka)
        bonus = _split_dot(r_ref[:, sl] * k_sum * rk_ref[:, sl], ones_bd) * v_ref[:, sl]
        o_ref[:, sl] = ((yn + bonus) * g_ref[:, sl].astype(F32)).astype(o_ref.dtype)


def _rwkv_readout(y0, y1, r, k, v, as0, as1, g, ka, rk, lnw, lnb):
    b, t, d = r.shape
    tm = min(t, 256)
    spec = pl.BlockSpec((None, tm, d), lambda i, j: (i, j, 0))
    vec = pl.BlockSpec((1, d), lambda i, j: (0, 0))
    return pl.pallas_call(
        _readout_kernel,
        grid=(b, t // tm),
        in_specs=[spec] * 8 + [vec] * 4,
        out_specs=spec,
        out_shape=jax.ShapeDtypeStruct((b, t, d), BF16),
        compiler_params=_cparams(("parallel", "parallel"), 2 * 9 * tm * d * 4 + (8 << 20)),
        name="rwkv_readout",
    )(y0, y1, r, k, v, as0, as1, g, ka.reshape(1, d), rk.reshape(1, d), lnw.reshape(1, d), lnb.reshape(1, d))


def _mm_res_kernel(z_ref, w_ref, res_ref, gate_ref, o_ref):
    o_ref[...] = res_ref[...] + gate_ref[...] * _dot(z_ref[...], w_ref[...])


def _norm_mm_kernel(x_ref, g_ref, sh_ref, sc_ref, w_ref, o_ref, h_ref):
    @pl.when(pl.program_id(2) == 0)
    def _():
        h_ref[...] = _norm_mod_value(x_ref[...], g_ref[...], sh_ref[...], sc_ref[...]).astype(BF16)

    o_ref[...] = _dot(h_ref[...], w_ref[...])


def _norm_matmul(x, g, sh, sc, w, col0=0):
    b, t, d = x.shape
    n = w.shape[1] - col0
    tm = min(t, 1024)
    tn = 512
    assert col0 % tn == 0
    cb0 = col0 // tn
    vmem = 2 * tm * d * 4 + tm * d * 2 + 2 * d * tn * 2 + 3 * tm * tn * 4 + tm * d * 4 + (4 << 20)
    return pl.pallas_call(
        _norm_mm_kernel,
        grid=(b, t // tm, n // tn),
        in_specs=[pl.BlockSpec((None, tm, d), lambda i, j, l: (i, j, 0))] + _norm_specs(g, sh, sc, d)
                 + [pl.BlockSpec((d, tn), lambda i, j, l: (0, l + cb0))],
        out_specs=pl.BlockSpec((None, tm, tn), lambda i, j, l: (i, j, l)),
        out_shape=jax.ShapeDtypeStruct((b, t, n), F32),
        scratch_shapes=[pltpu.VMEM((tm, d), BF16)],
        compiler_params=_cparams(("parallel", "parallel", "arbitrary"), vmem),
        name="norm_matmul",
    )(x, g.reshape(1, d), sh, sc, w)


def _matmul_res(z, w, res, gate):
    b, t, kdim = z.shape
    n = w.shape[1]
    tm = min(t, 1024)
    tn = 512
    zspec = pl.BlockSpec((None, tm, kdim), lambda i, j, l: (i, j, 0))
    wspec = pl.BlockSpec((kdim, tn), lambda i, j, l: (0, l))
    ospec = pl.BlockSpec((None, tm, tn), lambda i, j, l: (i, j, l))
    gspec = (pl.BlockSpec((None, 1, tn), lambda i, j, l: (0, 0, l)) if gate.shape[0] == 1
             else pl.BlockSpec((None, 1, tn), lambda i, j, l: (i, 0, l)))
    vmem = 2 * (tm * kdim * 2 + kdim * tn * 2 + 2 * tm * tn * 4) + tm * tn * 4 + (4 << 20)
    return pl.pallas_call(
        _mm_res_kernel,
        grid=(b, t // tm, n // tn),
        in_specs=[zspec, wspec, ospec, gspec],
        out_specs=ospec,
        out_shape=jax.ShapeDtypeStruct((b, t, n), F32),
        compiler_params=_cparams(("parallel", "parallel", "parallel"), vmem),
        name="matmul_res",
    )(z, w, res, gate)


def _ffn_kernel(x_ref, g_ref, sh_ref, sc_ref, gate_ref, wg_ref, wu_ref, wd_ref, o_ref, h_ref):
    f = pl.program_id(2)

    @pl.when(f == 0)
    def _():
        h_ref[...] = _norm_mod_value(x_ref[...], g_ref[...], sh_ref[...], sc_ref[...]).astype(BF16)
        o_ref[...] = jnp.zeros_like(o_ref)

    h = h_ref[...]
    gp = _dot(h, wg_ref[...])
    act = ((gp * jax.nn.sigmoid(gp)) * _dot(h, wu_ref[...])).astype(BF16)
    d = o_ref.shape[-1]
    for c0 in range(0, d, FFN_ACC_COLS):
        cols = slice(c0, c0 + FFN_ACC_COLS)
        o_ref[:, cols] += _dot(act, wd_ref[:, cols])

    @pl.when(f == pl.num_programs(2) - 1)
    def _():
        o_ref[...] = x_ref[...] + gate_ref[...] * o_ref[...]


def _ffn(x, g, sh, sc, gate, wg, wu, wd):
    b, t, d = x.shape
    ff = wg.shape[1]
    tm = min(t, 512)
    tf = 512
    xspec = pl.BlockSpec((None, tm, d), lambda i, j, f: (i, j, 0))
    vmem = 4 * tm * d * 4 + tm * d * 2 + 2 * 3 * d * tf * 2 + 6 * tm * tf * 4 + tm * d * 4 + (4 << 20)
    return pl.pallas_call(
        _ffn_kernel,
        grid=(b, t // tm, ff // tf),
        in_specs=[xspec, pl.BlockSpec((1, d), lambda i, j, f: (0, 0)),
                  _row_spec(sh, d), _row_spec(sc, d), _row_spec(gate, d),
                  pl.BlockSpec((d, tf), lambda i, j, f: (0, f)),
                  pl.BlockSpec((d, tf), lambda i, j, f: (0, f)),
                  pl.BlockSpec((tf, d), lambda i, j, f: (f, 0))],
        out_specs=xspec,
        out_shape=jax.ShapeDtypeStruct((b, t, d), F32),
        scratch_shapes=[pltpu.VMEM((tm, d), BF16)],
        compiler_params=_cparams(("parallel", "parallel", "arbitrary"), vmem),
        name="ffn",
    )(x, g.reshape(1, d), sh, sc, gate, wg, wu, wd)


def _qk_norm(x, gain, ones_bd):
    ms = _split_dot(x * x, ones_bd) * (1.0 / HEAD)
    return x * lax.rsqrt(ms + QK_EPS) * gain


def _rope(u, cos, sin):
    n = u.shape[-1]
    lane = lax.broadcasted_iota(jnp.int32, u.shape, 1)
    partner = jnp.where(lane % 32 < 16, pltpu.roll(u, n - 16, 1), pltpu.roll(u, 16, 1))
    return u * cos + partner * sin


def _qkv_prep_kernel(*refs, rope, with_q):
    refs = list(refs)
    q_ref = refs.pop(0) if with_q else None
    k_ref, v_ref = refs.pop(0), refs.pop(0)
    qn_ref = refs.pop(0) if with_q else None
    kn_ref = refs.pop(0)
    if rope:
        cos_ref, sin_ref = refs.pop(0), refs.pop(0)
    qo_ref = refs.pop(0) if with_q else None
    ko_ref, vo_ref = refs
    d = ko_ref.shape[-1]
    ones_bd = jnp.where(_block_mask(GROUP), 1.0, 0.0).astype(BF16)
    if rope:
        cos, sin = cos_ref[...], sin_ref[...]
    for j in range(d // GROUP):
        sl = slice(j * GROUP, (j + 1) * GROUP)
        kh = _qk_norm(k_ref[:, sl], kn_ref[:, sl], ones_bd)
        if rope:
            kh = _rope(kh, cos, sin)
        ko_ref[:, sl] = kh.astype(BF16)
        if with_q:
            qh = _qk_norm(q_ref[:, sl], qn_ref[:, sl], ones_bd)
            if rope:
                qh = _rope(qh, cos, sin)
            qo_ref[:, sl] = (qh * (HEAD ** -0.5 * LOG2E)).astype(BF16)
    vo_ref[...] = v_ref[...].T.astype(BF16)


def _qkv_prep(qkv, qn, kn, cos, sin, *, with_q):
    b, t, width = qkv.shape
    parts = 3 if with_q else 2
    d = width // parts
    rope = cos is not None
    tm = min(t, 256)
    col = lambda c: pl.BlockSpec((None, tm, d), lambda i, j, _c=c: (i, j, _c))
    vec = pl.BlockSpec((1, d), lambda i, j: (0, 0))
    tab = pl.BlockSpec((tm, GROUP), lambda i, j: (j, 0))
    in_specs = [col(c) for c in range(parts)] + [vec] * (2 if with_q else 1) + ([tab, tab] if rope else [])
    args = [qkv] * parts + ([qn] if with_q else []) + [kn] + ([cos, sin] if rope else [])
    out = jax.ShapeDtypeStruct((b, t, d), BF16)
    ospec = pl.BlockSpec((None, tm, d), lambda i, j: (i, j, 0))
    out_t = jax.ShapeDtypeStruct((b, d, t), BF16)
    ospec_t = pl.BlockSpec((None, d, tm), lambda i, j: (i, 0, j))
    return pl.pallas_call(
        functools.partial(_qkv_prep_kernel, rope=rope, with_q=with_q),
        grid=(b, t // tm),
        in_specs=in_specs,
        out_specs=[ospec] * (parts - 1) + [ospec_t],
        out_shape=[out] * (parts - 1) + [out_t],
        compiler_params=_cparams(("parallel", "parallel"), 2 * parts * tm * d * 6 + (16 << 20)),
        name="qkv_prep",
    )(*args)


def _attn_chain(q2, k, vt, state, shift):
    m, acc_t = state
    s_t = _dot_nt(k, q2)
    yield
    if shift is None:
        m_new = jnp.maximum(m, jnp.max(s_t, axis=0, keepdims=True))
        alpha = jnp.exp2(m - m_new)
        state[0] = m_new
        p = jnp.exp2(s_t - m_new)
    else:
        p = jnp.exp2(s_t - shift)
    pv = _dot(vt, p.astype(BF16))
    yield
    state[1] = pv + (acc_t if shift is not None else alpha * acc_t)


def _attn_kernel(lam_ref, q_ref, kc_ref, vtc_ref, kl_ref, vtl_ref, sub_ref, o_ref, *, tk, tq, lambda_init):
    n_sub = q_ref.shape[0] // tq
    lp = lam_ref[...]
    lam = (jnp.exp(jnp.sum(lp[0:1] * lp[1:2], axis=-1, keepdims=True))
           - jnp.exp(jnp.sum(lp[2:3] * lp[3:4], axis=-1, keepdims=True)) + lambda_init)
    bound = (HEAD ** 0.5 * LOG2E * SCORE_MARGIN) * (jnp.max(jnp.abs(lp[4:5]), axis=-1, keepdims=True)
                                                    * jnp.max(jnp.abs(lp[5:6]), axis=-1, keepdims=True))
    fixed_shift_ok = bound[0, 0] <= MAX_FIXED_SHIFT
    q2s = []
    for c in range(n_sub):
        q = q_ref[c * tq:(c + 1) * tq, :]
        lane = lax.broadcasted_iota(jnp.int32, q.shape, 1)
        zero = jnp.zeros_like(q)
        q2s.append(jnp.concatenate([jnp.where(lane < HEAD, q, zero), jnp.where(lane >= HEAD, q, zero)], axis=0))

    hw = 2 * HEAD
    ones_rows = 16

    def run(shift):
        def step(k, v_t, carry):
            vt = jnp.concatenate([v_t, jnp.ones((ones_rows, k.shape[0]), BF16)], axis=0)
            states = [list(st) for st in carry]
            _interleave([_attn_chain(q2, k, vt, st, shift) for q2, st in zip(q2s, states)])
            return tuple(tuple(st) for st in states)

        def latent_step(j, carry):
            ks = pl.ds(pl.multiple_of(j * tk, tk), tk)
            return step(kl_ref[ks, :], vtl_ref[:, ks], carry)

        init = (jnp.full((1, 2 * tq), -jnp.inf, F32), jnp.zeros((hw + ones_rows, 2 * tq), F32))
        carry = step(kc_ref[...], vtc_ref[...], (init,) * n_sub)
        carry = lax.fori_loop(0, kl_ref.shape[0] // tk, latent_step, carry)
        for c, (_, acc) in enumerate(carry):
            acc_t, l = acc[:hw], acc[hw:hw + 1]
            o_t = acc_t[:, :tq] / l[:, :tq] - lam * (acc_t[:, tq:] / l[:, tq:])
            o = o_t.T
            o = o * lax.rsqrt(jnp.mean(o * o, axis=-1, keepdims=True) + SUBLN_EPS) * sub_ref[...]
            o_ref[c * tq:(c + 1) * tq, :] = (o * (1.0 - lambda_init)).astype(o_ref.dtype)

    @pl.when(fixed_shift_ok)
    def _():
        run(bound)

    @pl.when(jnp.logical_not(fixed_shift_ok))
    def _():
        run(None)


def _diff_attention(q, k_c, vt_c, k_l, vt_l, lam_params, sub_gain, lambda_init):
    b, t, d = q.shape
    tc = k_c.shape[1]
    hw = 2 * HEAD
    tq = 256
    n_sub = math.gcd(t // tq, ATTN_CHAINS)
    tk = min(t, 2048)
    assert t % (tq * n_sub) == 0 and t % tk == 0 and tc % 128 == 0
    qspec = pl.BlockSpec((None, tq * n_sub, hw), lambda i, h, j: (i, j, h))

    def kspec(n):
        return pl.BlockSpec((None, n, hw), lambda i, h, j: (i, 0, h))

    def vspec(n):
        return pl.BlockSpec((None, hw, n), lambda i, h, j: (i, h, 0))

    vmem = 4 * (t + tc) * hw * 2 + 4 * n_sub * tq * hw * 2 + n_sub * 8 * 2 * tq * tk * 4 + (8 << 20)
    return pl.pallas_call(
        functools.partial(_attn_kernel, tk=tk, tq=tq, lambda_init=lambda_init),
        grid=(b, d // hw, t // (tq * n_sub)),
        in_specs=[pl.BlockSpec(lam_params.shape, lambda i, h, j: (0, 0)), qspec,
                  kspec(tc), vspec(tc), kspec(t), vspec(t), pl.BlockSpec((1, hw), lambda i, h, j: (0, 0))],
        out_specs=qspec,
        out_shape=jax.ShapeDtypeStruct((b, t, d), BF16),
        compiler_params=_cparams(("parallel", "parallel", "parallel"), vmem),
        name="diff_attention",
    )(lam_params, q, k_c, vt_c, k_l, vt_l, sub_gain.reshape(1, hw))


def _rope_tables(n_tokens):
    n_freq = HEAD // 4
    pos = jnp.arange(n_tokens)
    freqs = ROPE_BASE ** (-jnp.arange(n_freq, dtype=F32) / n_freq)
    ang_row = (pos // GRID_W).astype(F32)[:, None] * freqs
    ang_col = (pos % GRID_W).astype(F32)[:, None] * freqs
    cos = jnp.concatenate([jnp.cos(ang_row)] * 2 + [jnp.cos(ang_col)] * 2, axis=-1)
    sin = jnp.concatenate([-jnp.sin(ang_row), jnp.sin(ang_row), -jnp.sin(ang_col), jnp.sin(ang_col)], axis=-1)
    reps = GROUP // HEAD
    return jnp.tile(cos, (1, reps)), jnp.tile(sin, (1, reps))


def _mod_rows(mod, layer, j, d, batch):
    m = mod[layer, :, j * d:(j + 1) * d]
    return m[:batch, None, :], m[batch:batch + 1, None, :]


def _rwkv_layer(xs, mods, norm_g, p):
    rkv, lora, outs = [], [], []
    for x, (sh, sc, _) in zip(xs, mods):
        xr, xw, xk, xv, xa, xg = _rwkv_mix(x, norm_g, sh, sc, p["mu"])
        rkv.append(_rwkv_rkv(xr, xk, xv, p["wr"], p["wk"], p["wv"]))
        lora.append(_rwkv_lora(xw, xa, xg, p["w0"], p["w1"], p["w2"], p["a0"], p["a1"], p["a2"], p["g1"], p["g2"]))
    b, _, d = xs[0].shape
    state = jnp.zeros((b, 2, d // GROUP, GROUP, GROUP), F32)
    for x, (_, _, gate), (r, k, v), (g, lw0, lw1, as0, as1) in zip(xs, mods, rkv, lora):
        y0, y1, state = _rwkv_scan(r, k, v, lw0, as0, lw1, as1, p["kk"], p["ka"], state)
        z = _rwkv_readout(y0, y1, r, k, v, as0, as1, g, p["ka"], p["rk"], p["lnw"], p["lnb"])
        outs.append(_matmul_res(z, p["wo"], x, gate))
    return outs


def kernel(x, c, ctx, c_ctx, mod_w, mod_b, norm1_g, norm2_g, rwkv_mu, rwkv_wr, rwkv_wk, rwkv_wv, rwkv_wo, rwkv_w0, rwkv_w1, rwkv_w2, rwkv_a0, rwkv_a1, rwkv_a2, rwkv_g1, rwkv_g2, rwkv_kk, rwkv_ka, rwkv_rk, rwkv_lnw, rwkv_lnb, diff_wqkv, diff_qn, diff_kn, diff_lq1, diff_lk1, diff_lq2, diff_lk2, diff_subln, diff_wo, ffn_wg, ffn_wu, ffn_wd):
    batch, n_lat, d = x.shape
    depth = mod_w.shape[0]
    assert depth == 2, "layer 0 is the RWKV-7 mixer, layer 1 differential attention"

    cvecs = jnp.zeros((SUBLANES, d), F32).at[:batch].set(c).at[batch].set(c_ctx)
    mod = _adaln(cvecs, mod_w, mod_b)

    def mods(layer, first):
        lat, cx = zip(*[_mod_rows(mod, layer, first + j, d, batch) for j in range(3)])
        return cx, lat

    w1, w2 = _pad_lora(rwkv_w1[0], rwkv_w2[0])
    a1, a2 = _pad_lora(rwkv_a1[0], rwkv_a2[0])
    g1, g2 = _pad_lora(rwkv_g1[0], rwkv_g2[0])
    p = dict(mu=rwkv_mu[0], wr=rwkv_wr[0].astype(BF16), wk=rwkv_wk[0].astype(BF16), wv=rwkv_wv[0].astype(BF16),
             wo=rwkv_wo[0].astype(BF16), w0=rwkv_w0[0], w1=w1, w2=w2, a0=rwkv_a0[0], a1=a1, a2=a2,
             g1=g1, g2=g2, kk=rwkv_kk[0], ka=rwkv_ka[0],
             rk=rwkv_rk[0], lnw=rwkv_lnw[0], lnb=rwkv_lnb[0])
    xc, xl = _rwkv_layer((ctx, x), mods(0, 0), norm1_g[0], p)
    wg, wu, wd = ffn_wg[0].astype(BF16), ffn_wu[0].astype(BF16), ffn_wd[0].astype(BF16)
    (csh, csc, cgt), (lsh, lsc, lgt) = mods(0, 3)
    xc = _ffn(xc, norm2_g[0], csh, csc, cgt, wg, wu, wd)
    xl = _ffn(xl, norm2_g[0], lsh, lsc, lgt, wg, wu, wd)

    (csh, csc, _), (lsh, lsc, lgt) = mods(1, 0)
    lambda_init = 0.8 - 0.6 * math.exp(-0.3 * 1)
    wqkv = diff_wqkv[0].astype(BF16)
    qn = jnp.tile(diff_qn[0], d // HEAD).reshape(1, d)
    kn = jnp.tile(diff_kn[0], d // HEAD).reshape(1, d)
    cos, sin = _rope_tables(n_lat)
    q_l, k_l, v_l = _qkv_prep(_norm_matmul(xl, norm1_g[1], lsh, lsc, wqkv), qn, kn, cos, sin, with_q=True)
    k_c, v_c = _qkv_prep(_norm_matmul(xc, norm1_g[1], csh, csc, wqkv, col0=d), None, kn, None, None, with_q=False)
    lam_params = jnp.stack([diff_lq1[0], diff_lk1[0], diff_lq2[0], diff_lk2[0], diff_qn[0], diff_kn[0]])
    o = _diff_attention(q_l, k_c, v_c, k_l, v_l, lam_params, diff_subln[0], lambda_init)
    xl = _matmul_res(o, diff_wo[0].astype(BF16), xl, lgt)
    (_, _, _), (lsh, lsc, lgt) = mods(1, 3)
    return _ffn(xl, norm2_g[1], lsh, lsc, lgt,
                ffn_wg[1].astype(BF16), ffn_wu[1].astype(BF16), ffn_wd[1].astype(BF16))
```

```python
import functools
import math

import jax
import jax.numpy as jnp
from jax import lax
from jax.experimental import pallas as pl
from jax.experimental.pallas import tpu as pltpu

F32 = jnp.float32
BF16 = jnp.bfloat16

HEAD = 64
GROUP = 256
CHUNK = 64
SCAN_UNROLL = 8
SCAN_SKEW = 4
ATTN_CHAINS = 4
FFN_ACC_COLS = 512
CAST_ROWS = 256
SUBLANES = 8
V7X_VMEM_CAP = 56 * 1024 * 1024

NORM_EPS = 1e-6
GN_EPS = 64e-5
L2_EPS = 1e-24
QK_EPS = 1e-6
SUBLN_EPS = 1e-5
ROPE_BASE = 10000.0
GRID_W = 64
DECAY_SCALE = math.exp(-0.5)
LOG2E = math.log2(math.e)
SCORE_MARGIN = 1.01
MAX_FIXED_SHIFT = 50.0


def _cparams(semantics, vmem_bytes):
    return pltpu.CompilerParams(dimension_semantics=semantics,
                                vmem_limit_bytes=int(min(max(vmem_bytes, 16 << 20), V7X_VMEM_CAP)))


def _dot(a, b):
    return jnp.dot(a, b, preferred_element_type=F32)


def _dot_nt(a, b):
    return lax.dot_general(a, b, (((1,), (1,)), ((), ())), preferred_element_type=F32)


def _dot_tn(a, b):
    return lax.dot_general(a, b, (((0,), (0,)), ((), ())), preferred_element_type=F32)


def _split_dot(x, w_bf16):
    hi = x.astype(BF16)
    lo = (x - hi.astype(F32)).astype(BF16)
    return _dot(hi, w_bf16) + _dot(lo, w_bf16)


def _block_mask(n):
    r = lax.broadcasted_iota(jnp.int32, (n, n), 0) // HEAD
    c = lax.broadcasted_iota(jnp.int32, (n, n), 1) // HEAD
    return r == c


def _cast_kernel(w_ref, o_ref):
    o_ref[...] = w_ref[...].astype(o_ref.dtype)


def _weight_bf16(w, layer):
    _, rows, cols = w.shape
    tr = CAST_ROWS
    assert rows % tr == 0
    return pl.pallas_call(
        _cast_kernel,
        grid=(rows // tr,),
        in_specs=[pl.BlockSpec((None, tr, cols), lambda i: (layer, i, 0))],
        out_specs=pl.BlockSpec((tr, cols), lambda i: (i, 0)),
        out_shape=jax.ShapeDtypeStruct((rows, cols), BF16),
        compiler_params=_cparams(("parallel",), 2 * tr * cols * 6 + tr * cols * 4),
        name="weight_bf16",
    )(w)


def _adaln_kernel(c_ref, w_ref, b_ref, o_ref):
    c = c_ref[...]
    s = (c * jax.nn.sigmoid(c)).astype(BF16)
    o_ref[...] = _dot(s, w_ref[...].astype(BF16)) + b_ref[...]


def _adaln(cvecs, mod_w, mod_b):
    depth, d, n = mod_w.shape
    rows = cvecs.shape[0]
    tn = 1024
    return pl.pallas_call(
        _adaln_kernel,
        grid=(depth, n // tn),
        in_specs=[pl.BlockSpec((rows, d), lambda l, j: (0, 0)),
                  pl.BlockSpec((None, d, tn), lambda l, j: (l, 0, j)),
                  pl.BlockSpec((None, 1, tn), lambda l, j: (l, 0, j))],
        out_specs=pl.BlockSpec((None, rows, tn), lambda l, j: (l, 0, j)),
        out_shape=jax.ShapeDtypeStruct((depth, rows, n), F32),
        compiler_params=_cparams(("parallel", "parallel"), 3 * d * tn * 4 + (8 << 20)),
        name="adaln",
    )(cvecs, mod_w, mod_b.reshape(depth, 1, n))


def _norm_mod_value(x, g, sh, sc):
    y = x * lax.rsqrt(jnp.mean(x * x, axis=-1, keepdims=True) + NORM_EPS)
    return (y * g) * (1.0 + sc) + sh


def _row_spec(arr, d):
    if arr.shape[0] == 1:
        return pl.BlockSpec((None, 1, d), lambda b, *_: (0, 0, 0))
    return pl.BlockSpec((None, 1, d), lambda b, *_: (b, 0, 0))


def _norm_specs(g, sh, sc, d):
    return [pl.BlockSpec((1, d), lambda b, *_: (0, 0)), _row_spec(sh, d), _row_spec(sc, d)]


def _mix_kernel(x_ref, xp_ref, xn_ref, g_ref, sh_ref, sc_ref, mu_ref, *o_refs, nt):
    t = pl.program_id(1)
    tm, d = x_ref.shape

    def inv_rms(ref):
        x = ref[...]
        return lax.rsqrt(jnp.mean(x * x, axis=-1, keepdims=True) + NORM_EPS)

    r, r_prev, r_next = inv_rms(x_ref), inv_rms(xp_ref), inv_rms(xn_ref)
    row = lax.broadcasted_iota(jnp.int32, (tm, GROUP), 0)
    for c0 in range(0, d, GROUP):
        sl = slice(c0, c0 + GROUP)
        g, sh, sc = g_ref[:, sl], sh_ref[:, sl], sc_ref[:, sl]

        def norm(ref, rinv):
            return ((ref[:, sl] * rinv) * g) * (1.0 + sc) + sh

        h = norm(x_ref, r)
        prev_row = jnp.where(t > 0, norm(xp_ref, r_prev)[SUBLANES - 1:SUBLANES, :], 0.0)
        next_row = jnp.where(t < nt - 1, norm(xn_ref, r_next)[0:1, :], 0.0)
        hp = jnp.where(row == 0, prev_row, pltpu.roll(h, 1, 0))
        hn = jnp.where(row == tm - 1, next_row, pltpu.roll(h, tm - 1, 0))
        xx = 0.5 * (hp + hn) - h
        for j, o_ref in enumerate(o_refs):
            o_ref[:, sl] = (h + xx * mu_ref[j:j + 1, sl]).astype(o_ref.dtype)


def _rwkv_mix(x, g, sh, sc, mu):
    b, t, d = x.shape
    n_mix = mu.shape[0]
    tm = min(t, 256)
    nt = t // tm
    per = tm // SUBLANES
    last = t // SUBLANES - 1
    spec = pl.BlockSpec((None, tm, d), lambda i, j: (i, j, 0))
    halo = [pl.BlockSpec((None, SUBLANES, d), lambda i, j: (i, jnp.maximum(j * per - 1, 0), 0)),
            pl.BlockSpec((None, SUBLANES, d), lambda i, j: (i, jnp.minimum((j + 1) * per, last), 0))]
    return pl.pallas_call(
        functools.partial(_mix_kernel, nt=nt),
        grid=(b, nt),
        in_specs=[spec] + halo + _norm_specs(g, sh, sc, d) + [pl.BlockSpec(mu.shape, lambda i, j: (0, 0))],
        out_specs=[spec] * n_mix,
        out_shape=[jax.ShapeDtypeStruct((b, t, d), BF16)] * n_mix,
        compiler_params=_cparams(("parallel", "parallel"), 2 * tm * d * (4 + 2 * n_mix) + 8 * tm * d * 4),
        name="rwkv_mix",
    )(x, x, x, g.reshape(1, d), sh, sc, mu)


def _rkv_kernel(xr_ref, xk_ref, xv_ref, wr_ref, wk_ref, wv_ref, r_ref, k_ref, v_ref):
    r_ref[...] = _dot(xr_ref[...], wr_ref[...])
    k_ref[...] = _dot(xk_ref[...], wk_ref[...])
    v_ref[...] = _dot(xv_ref[...], wv_ref[...])


def _rwkv_rkv(xr, xk, xv, wr, wk, wv):
    b, t, d = xr.shape
    tm = min(t, 1024)
    tn = 512
    out = jax.ShapeDtypeStruct((b, t, d), F32)
    xspec = pl.BlockSpec((None, tm, d), lambda i, j, n: (i, j, 0))
    wspec = pl.BlockSpec((d, tn), lambda i, j, n: (0, n))
    ospec = pl.BlockSpec((None, tm, tn), lambda i, j, n: (i, j, n))
    vmem = 6 * tm * d * 2 + 6 * d * tn * 2 + 9 * tm * tn * 4 + (4 << 20)
    return pl.pallas_call(
        _rkv_kernel,
        grid=(b, t // tm, d // tn),
        in_specs=[xspec, xspec, xspec, wspec, wspec, wspec],
        out_specs=[ospec, ospec, ospec],
        out_shape=[out, out, out],
        compiler_params=_cparams(("parallel", "parallel", "parallel"), vmem),
        name="rwkv_rkv",
    )(xr, xk, xv, wr, wk, wv)


def _lora_kernel(xw_ref, xa_ref, xg_ref, w0_ref, w1_ref, w2_ref, a0_ref, a1_ref, a2_ref, g1_ref, g2_ref,
                 g_ref, lw0_ref, lw1_ref, as0_ref, as1_ref):
    xw, xa, xg = xw_ref[...], xa_ref[...], xg_ref[...]
    g_ref[...] = _dot(jax.nn.sigmoid(_dot(xg, g1_ref[...])).astype(BF16), g2_ref[...]).astype(g_ref.dtype)
    for d, (lw_ref, as_ref) in enumerate(((lw0_ref, as0_ref), (lw1_ref, as1_ref))):
        w_pre = w0_ref[d:d + 1, :] + _dot(jnp.tanh(_dot(xw, w1_ref[d])).astype(BF16), w2_ref[d])
        lw_ref[...] = (-DECAY_SCALE * LOG2E) * jax.nn.sigmoid(w_pre)
        a_pre = a0_ref[d:d + 1, :] + _dot(_dot(xa, a1_ref[d]).astype(BF16), a2_ref[d])
        as_ref[...] = jax.nn.sigmoid(a_pre).astype(as_ref.dtype)


def _pad_lora(w_in, w_out):
    rank = w_in.shape[-1]
    pad = (-rank) % 128
    w_in = jnp.pad(w_in, [(0, 0)] * (w_in.ndim - 1) + [(0, pad)])
    w_out = jnp.pad(w_out, [(0, 0)] * (w_out.ndim - 2) + [(0, pad), (0, 0)])
    return w_in.astype(BF16), w_out.astype(BF16)


def _rwkv_lora(xw, xa, xg, w0, w1, w2, a0, a1, a2, g1, g2):
    b, t, d = xw.shape
    tm = min(t, 256)
    out = jax.ShapeDtypeStruct((b, t, d), F32)
    out16 = jax.ShapeDtypeStruct((b, t, d), BF16)
    spec = pl.BlockSpec((None, tm, d), lambda i, j: (i, j, 0))

    def full(a):
        return pl.BlockSpec(a.shape, lambda i, j, _n=a.ndim: (0,) * _n)

    params = (w0, w1, w2, a0, a1, a2, g1, g2)
    vmem = 2 * tm * d * (3 * 2 + 2 * 4 + 3 * 2) + 10 * tm * d * 4 + 2 * sum(p.size * p.dtype.itemsize for p in params)
    return pl.pallas_call(
        _lora_kernel,
        grid=(b, t // tm),
        in_specs=[spec] * 3 + [full(p) for p in params],
        out_specs=[spec] * 5,
        out_shape=[out16, out, out, out16, out16],
        compiler_params=_cparams(("parallel", "parallel"), vmem),
        name="rwkv_lora",
    )(xw, xa, xg, *params)


def _chunk_cumsum(x, reverse):
    n = x.shape[0]
    row = lax.broadcasted_iota(jnp.int32, x.shape, 0)
    shift = 1
    while shift < n:
        if reverse:
            x = x + jnp.where(row < n - shift, pltpu.roll(x, n - shift, 0), 0.0)
        else:
            x = x + jnp.where(row >= shift, pltpu.roll(x, shift, 0), 0.0)
        shift *= 2
    return x


def _scan_chain(d, refs, kks, ka, state_ref, y_ref, sl, g, consts):
    r_ref, k_ref, v_ref, lw_ref, as_ref = refs
    bmask, ones_bd, m_strict, m_incl = consts
    c = CHUNK

    def stack(x):
        return jnp.where(bmask, jnp.concatenate([x] * (GROUP // c), axis=0), 0.0).astype(BF16)

    lw = lw_ref[:, sl]
    a_sig = as_ref[:, sl].astype(F32)
    k = k_ref[:, sl]
    kk = k * kks
    ssq = _dot((kk * kk).astype(BF16), ones_bd)
    cum = _chunk_cumsum(lw, reverse=d == 1)
    yield

    kk = kk * lax.rsqrt(jnp.maximum(ssq, L2_EPS))
    b_vec = kk * a_sig
    k_d = k * (1.0 + (a_sig - 1.0) * ka)
    tot = cum[c - 1:c, :] if d == 0 else cum[0:1, :]
    a_t = -kk * jnp.exp2(cum - lw)
    r_t = r_ref[:, sl] * jnp.exp2(cum)
    p_inv = jnp.exp2(-cum)
    lhs = jnp.concatenate([a_t, r_t], axis=0).astype(BF16)
    rhs = jnp.concatenate([stack(b_vec * p_inv), stack(k_d * p_inv)], axis=0)
    aa = _dot_nt(lhs, rhs)
    yield

    n1 = jnp.where(m_strict[d], aa[:c, :GROUP], 0.0)
    a_ak = jnp.where(m_strict[d], aa[:c, GROUP:], 0.0)
    a_rb = jnp.where(m_incl[d], aa[c:, :GROUP], 0.0).astype(BF16)
    a_rk = jnp.where(m_incl[d], aa[c:, GROUP:], 0.0)
    v = v_ref[:, sl]
    z1 = _dot(jnp.concatenate([a_ak, a_rk], axis=0).astype(BF16), stack(v))
    q = n1
    npow = _dot(n1.astype(BF16), stack(n1))
    yield
    span = 2
    while 2 * span < c:
        z = _dot(jnp.concatenate([npow, q], axis=0).astype(BF16), stack(npow))
        yield
        q = q + npow + z[c:]
        npow = z[:c]
        span *= 2
    zq = _dot(q.astype(BF16), stack(npow))
    yield
    q = q + npow + zq

    v_ak = z1[:c]
    y0 = z1[c:]
    z2 = _dot(q.astype(BF16), jnp.concatenate([stack(a_t), stack(v_ak)], axis=1))
    yield
    w = a_t + z2[:, :GROUP]
    u0 = v_ak + z2[:, GROUP:]

    s = state_ref[d, g]
    z3 = _dot_nt(jnp.concatenate([w, r_t], axis=0).astype(BF16), s.astype(BF16))
    yield
    u = z3[:c] + u0
    p_end = jnp.exp2(tot - cum)
    yu = _dot(a_rb, stack(u))
    upd = _dot_tn(jnp.concatenate([u, v], axis=0).astype(BF16),
                  jnp.concatenate([b_vec * p_end, k_d * p_end], axis=0).astype(BF16))
    yield
    y_ref[:, sl] = z3[c:] + yu + y0
    state_ref[d, g] = s * jnp.exp2(tot) + jnp.where(bmask, upd, 0.0)


def _interleave(chains, skew=0.0):
    pending = list(enumerate(chains))
    rnd = 0
    while pending:
        alive = []
        for i, ch in pending:
            if rnd >= (i % 2) * skew:
                try:
                    next(ch)
                except StopIteration:
                    continue
            alive.append((i, ch))
        pending = alive
        rnd += 1


def _scan_kernel(r0, k0, v0, lw0, as0, r1, k1, v1, lw1, as1, kks_ref, ka_ref, sin_ref,
                 y0_ref, y1_ref, state_ref, *, groups):
    i = pl.program_id(1)
    c = CHUNK
    unroll = math.gcd(groups, SCAN_UNROLL)

    @pl.when(i == 0)
    def _():
        state_ref[...] = sin_ref[...]

    bmask = _block_mask(GROUP)
    ones_bd = jnp.where(bmask, 1.0, 0.0).astype(BF16)
    tp = lax.broadcasted_iota(jnp.int32, (c, GROUP), 0)
    sp = lax.broadcasted_iota(jnp.int32, (c, GROUP), 1) % c
    m_strict = (sp < tp, sp > tp)
    m_incl = (sp <= tp, sp >= tp)
    consts = (bmask, ones_bd, m_strict, m_incl)

    def body(it, carry):
        chains = []
        for j in range(unroll):
            g = it * unroll + j
            sl = pl.ds(pl.multiple_of(g * GROUP, GROUP), GROUP)
            kks = kks_ref[:, sl]
            ka = ka_ref[:, sl]
            chains.append(_scan_chain(0, (r0, k0, v0, lw0, as0), kks, ka, state_ref, y0_ref, sl, g, consts))
            chains.append(_scan_chain(1, (r1, k1, v1, lw1, as1), kks, ka, state_ref, y1_ref, sl, g, consts))
        _interleave(chains, SCAN_SKEW)
        return carry

    lax.fori_loop(0, groups // unroll, body, 0)


def _rwkv_scan(r, k, v, lw0, as0, lw1, as1, kk_scale, ka, s_init):
    b, t, d = r.shape
    assert CHUNK * (GROUP // HEAD) == GROUP and t % CHUNK == 0 and d % GROUP == 0
    groups = d // GROUP
    nc = t // CHUNK
    fwd = pl.BlockSpec((None, CHUNK, d), lambda i, j: (i, j, 0))
    bwd = pl.BlockSpec((None, CHUNK, d), lambda i, j: (i, nc - 1 - j, 0))
    vec = pl.BlockSpec((1, d), lambda i, j: (0, 0))
    sspec = pl.BlockSpec((None, 2, groups, GROUP, GROUP), lambda i, j: (i, 0, 0, 0, 0))
    y = jax.ShapeDtypeStruct((b, t, d), F32)
    state_bytes = 2 * groups * GROUP * GROUP * 4
    vmem = 2 * 12 * CHUNK * d * 4 + 4 * state_bytes + 2 * SCAN_UNROLL * (3 << 19) + (4 << 20)
    return pl.pallas_call(
        functools.partial(_scan_kernel, groups=groups),
        grid=(b, nc),
        in_specs=[fwd] * 5 + [bwd] * 5 + [vec, vec, sspec],
        out_specs=[fwd, bwd, sspec],
        out_shape=[y, y, jax.ShapeDtypeStruct(s_init.shape, F32)],
        compiler_params=_cparams(("parallel", "arbitrary"), vmem),
        name="rwkv_scan",
    )(r, k, v, lw0, as0, r, k, v, lw1, as1, kk_scale.reshape(1, d), ka.reshape(1, d), s_init)


def _readout_kernel(y0_ref, y1_ref, r_ref, k_ref, v_ref, as0_ref, as1_ref, g_ref,
                    ka_ref, rk_ref, lnw_ref, lnb_ref, o_ref):
    d = o_ref.shape[-1]
    ones_bd = jnp.where(_block_mask(GROUP), 1.0, 0.0).astype(BF16)
    for j in range(d // GROUP):
        sl = slice(j * GROUP, (j + 1) * GROUP)
        y = y0_ref[:, sl] + y1_ref[:, sl]
        mean = _split_dot(y, ones_bd) * (1.0 / HEAD)
        yc = y - mean
        var = _split_dot(yc * yc, ones_bd) * (1.0 / HEAD)
        yn = (yc * lax.rsqrt(var + GN_EPS)) * lnw_ref[:, sl] + lnb_ref[:, sl]
        k = k_ref[:, sl]
        ka = ka_ref[:, sl]
        a0, a1 = as0_ref[:, sl].astype(F32), as1_ref[:, sl].astype(F32)
        k_sum = k * (1.0 + (a0 - 1.0) * ka) + k * (1.0 + (a1 - 1.0) * ka)
        bonus = _split_dot(r_ref[:, sl] * k_sum * rk_ref[:, sl], ones_bd) * v_ref[:, sl]
        o_ref[:, sl] = ((yn + bonus) * g_ref[:, sl].astype(F32)).astype(o_ref.dtype)


def _rwkv_readout(y0, y1, r, k, v, as0, as1, g, ka, rk, lnw, lnb):
    b, t, d = r.shape
    tm = min(t, 256)
    spec = pl.BlockSpec((None, tm, d), lambda i, j: (i, j, 0))
    vec = pl.BlockSpec((1, d), lambda i, j: (0, 0))
    return pl.pallas_call(
        _readout_kernel,
        grid=(b, t // tm),
        in_specs=[spec] * 8 + [vec] * 4,
        out_specs=spec,
        out_shape=jax.ShapeDtypeStruct((b, t, d), BF16),
        compiler_params=_cparams(("parallel", "parallel"), 2 * 9 * tm * d * 4 + (8 << 20)),
        name="rwkv_readout",
    )(y0, y1, r, k, v, as0, as1, g, ka.reshape(1, d), rk.reshape(1, d), lnw.reshape(1, d), lnb.reshape(1, d))


def _mm_res_kernel(z_ref, w_ref, res_ref, gate_ref, o_ref):
    o_ref[...] = res_ref[...] + gate_ref[...] * _dot(z_ref[...], w_ref[...])


def _norm_mm_kernel(x_ref, g_ref, sh_ref, sc_ref, w_ref, o_ref, h_ref):
    @pl.when(pl.program_id(2) == 0)
    def _():
        h_ref[...] = _norm_mod_value(x_ref[...], g_ref[...], sh_ref[...], sc_ref[...]).astype(BF16)

    o_ref[...] = _dot(h_ref[...], w_ref[...])


def _norm_matmul(x, g, sh, sc, w, col0=0):
    b, t, d = x.shape
    n = w.shape[1] - col0
    tm = min(t, 1024)
    tn = 512
    assert col0 % tn == 0
    cb0 = col0 // tn
    vmem = 2 * tm * d * 4 + tm * d * 2 + 2 * d * tn * 2 + 3 * tm * tn * 4 + tm * d * 4 + (4 << 20)
    return pl.pallas_call(
        _norm_mm_kernel,
        grid=(b, t // tm, n // tn),
        in_specs=[pl.BlockSpec((None, tm, d), lambda i, j, l: (i, j, 0))] + _norm_specs(g, sh, sc, d)
                 + [pl.BlockSpec((d, tn), lambda i, j, l: (0, l + cb0))],
        out_specs=pl.BlockSpec((None, tm, tn), lambda i, j, l: (i, j, l)),
        out_shape=jax.ShapeDtypeStruct((b, t, n), F32),
        scratch_shapes=[pltpu.VMEM((tm, d), BF16)],
        compiler_params=_cparams(("parallel", "parallel", "arbitrary"), vmem),
        name="norm_matmul",
    )(x, g.reshape(1, d), sh, sc, w)


def _matmul_res(z, w, res, gate):
    b, t, kdim = z.shape
    n = w.shape[1]
    tm = min(t, 1024)
    tn = 512
    zspec = pl.BlockSpec((None, tm, kdim), lambda i, j, l: (i, j, 0))
    wspec = pl.BlockSpec((kdim, tn), lambda i, j, l: (0, l))
    ospec = pl.BlockSpec((None, tm, tn), lambda i, j, l: (i, j, l))
    gspec = (pl.BlockSpec((None, 1, tn), lambda i, j, l: (0, 0, l)) if gate.shape[0] == 1
             else pl.BlockSpec((None, 1, tn), lambda i, j, l: (i, 0, l)))
    vmem = 2 * (tm * kdim * 2 + kdim * tn * 2 + 2 * tm * tn * 4) + tm * tn * 4 + (4 << 20)
    return pl.pallas_call(
        _mm_res_kernel,
        grid=(b, t // tm, n // tn),
        in_specs=[zspec, wspec, ospec, gspec],
        out_specs=ospec,
        out_shape=jax.ShapeDtypeStruct((b, t, n), F32),
        compiler_params=_cparams(("parallel", "parallel", "parallel"), vmem),
        name="matmul_res",
    )(z, w, res, gate)


def _ffn_kernel(x_ref, g_ref, sh_ref, sc_ref, gate_ref, wg_ref, wu_ref, wd_ref, o_ref, h_ref):
    f = pl.program_id(2)

    @pl.when(f == 0)
    def _():
        h_ref[...] = _norm_mod_value(x_ref[...], g_ref[...], sh_ref[...], sc_ref[...]).astype(BF16)
        o_ref[...] = jnp.zeros_like(o_ref)

    h = h_ref[...]
    gp = _dot(h, wg_ref[...])
    act = ((gp * jax.nn.sigmoid(gp)) * _dot(h, wu_ref[...])).astype(BF16)
    d = o_ref.shape[-1]
    for c0 in range(0, d, FFN_ACC_COLS):
        cols = slice(c0, c0 + FFN_ACC_COLS)
        o_ref[:, cols] += _dot(act, wd_ref[:, cols])

    @pl.when(f == pl.num_programs(2) - 1)
    def _():
        o_ref[...] = x_ref[...] + gate_ref[...] * o_ref[...]


def _ffn(x, g, sh, sc, gate, wg, wu, wd):
    b, t, d = x.shape
    ff = wg.shape[1]
    tm = min(t, 512)
    tf = 512
    xspec = pl.BlockSpec((None, tm, d), lambda i, j, f: (i, j, 0))
    vmem = 4 * tm * d * 4 + tm * d * 2 + 2 * 3 * d * tf * 2 + 6 * tm * tf * 4 + tm * d * 4 + (4 << 20)
    return pl.pallas_call(
        _ffn_kernel,
        grid=(b, t // tm, ff // tf),
        in_specs=[xspec, pl.BlockSpec((1, d), lambda i, j, f: (0, 0)),
                  _row_spec(sh, d), _row_spec(sc, d), _row_spec(gate, d),
                  pl.BlockSpec((d, tf), lambda i, j, f: (0, f)),
                  pl.BlockSpec((d, tf), lambda i, j, f: (0, f)),
                  pl.BlockSpec((tf, d), lambda i, j, f: (f, 0))],
        out_specs=xspec,
        out_shape=jax.ShapeDtypeStruct((b, t, d), F32),
        scratch_shapes=[pltpu.VMEM((tm, d), BF16)],
        compiler_params=_cparams(("parallel", "parallel", "arbitrary"), vmem),
        name="ffn",
    )(x, g.reshape(1, d), sh, sc, gate, wg, wu, wd)


def _qk_norm(x, gain, ones_bd):
    ms = _split_dot(x * x, ones_bd) * (1.0 / HEAD)
    return x * lax.rsqrt(ms + QK_EPS) * gain


def _rope(u, cos, sin):
    n = u.shape[-1]
    lane = lax.broadcasted_iota(jnp.int32, u.shape, 1)
    partner = jnp.where(lane % 32 < 16, pltpu.roll(u, n - 16, 1), pltpu.roll(u, 16, 1))
    return u * cos + partner * sin


def _qkv_prep_kernel(*refs, rope, with_q):
    refs = list(refs)
    q_ref = refs.pop(0) if with_q else None
    k_ref, v_ref = refs.pop(0), refs.pop(0)
    qn_ref = refs.pop(0) if with_q else None
    kn_ref = refs.pop(0)
    if rope:
        cos_ref, sin_ref = refs.pop(0), refs.pop(0)
    qo_ref = refs.pop(0) if with_q else None
    ko_ref, vo_ref = refs
    d = ko_ref.shape[-1]
    ones_bd = jnp.where(_block_mask(GROUP), 1.0, 0.0).astype(BF16)
    if rope:
        cos, sin = cos_ref[...], sin_ref[...]
    for j in range(d // GROUP):
        sl = slice(j * GROUP, (j + 1) * GROUP)
        kh = _qk_norm(k_ref[:, sl], kn_ref[:, sl], ones_bd)
        if rope:
            kh = _rope(kh, cos, sin)
        ko_ref[:, sl] = kh.astype(BF16)
        if with_q:
            qh = _qk_norm(q_ref[:, sl], qn_ref[:, sl], ones_bd)
            if rope:
                qh = _rope(qh, cos, sin)
            qo_ref[:, sl] = (qh * (HEAD ** -0.5 * LOG2E)).astype(BF16)
    vo_ref[...] = v_ref[...].T.astype(BF16)


def _qkv_prep(qkv, qn, kn, cos, sin, *, with_q):
    b, t, width = qkv.shape
    parts = 3 if with_q else 2
    d = width // parts
    rope = cos is not None
    tm = min(t, 256)
    col = lambda c: pl.BlockSpec((None, tm, d), lambda i, j, _c=c: (i, j, _c))
    vec = pl.BlockSpec((1, d), lambda i, j: (0, 0))
    tab = pl.BlockSpec((tm, GROUP), lambda i, j: (j, 0))
    in_specs = [col(c) for c in range(parts)] + [vec] * (2 if with_q else 1) + ([tab, tab] if rope else [])
    args = [qkv] * parts + ([qn] if with_q else []) + [kn] + ([cos, sin] if rope else [])
    out = jax.ShapeDtypeStruct((b, t, d), BF16)
    ospec = pl.BlockSpec((None, tm, d), lambda i, j: (i, j, 0))
    out_t = jax.ShapeDtypeStruct((b, d, t), BF16)
    ospec_t = pl.BlockSpec((None, d, tm), lambda i, j: (i, 0, j))
    return pl.pallas_call(
        functools.partial(_qkv_prep_kernel, rope=rope, with_q=with_q),
        grid=(b, t // tm),
        in_specs=in_specs,
        out_specs=[ospec] * (parts - 1) + [ospec_t],
        out_shape=[out] * (parts - 1) + [out_t],
        compiler_params=_cparams(("parallel", "parallel"), 2 * parts * tm * d * 6 + (16 << 20)),
        name="qkv_prep",
    )(*args)


def _attn_chain(q2, k, vt, state, shift):
    m, acc_t = state
    s_t = _dot_nt(k, q2)
    yield
    if shift is None:
        m_new = jnp.maximum(m, jnp.max(s_t, axis=0, keepdims=True))
        alpha = jnp.exp2(m - m_new)
        state[0] = m_new
        p = jnp.exp2(s_t - m_new)
    else:
        p = jnp.exp2(s_t - shift)
    pv = _dot(vt, p.astype(BF16))
    yield
    state[1] = pv + (acc_t if shift is not None else alpha * acc_t)


def _attn_kernel(lam_ref, q_ref, kc_ref, vtc_ref, kl_ref, vtl_ref, sub_ref, o_ref, *, tk, tq, lambda_init):
    n_sub = q_ref.shape[0] // tq
    lp = lam_ref[...]
    lam = (jnp.exp(jnp.sum(lp[0:1] * lp[1:2], axis=-1, keepdims=True))
           - jnp.exp(jnp.sum(lp[2:3] * lp[3:4], axis=-1, keepdims=True)) + lambda_init)
    bound = (HEAD ** 0.5 * LOG2E * SCORE_MARGIN) * (jnp.max(jnp.abs(lp[4:5]), axis=-1, keepdims=True)
                                                    * jnp.max(jnp.abs(lp[5:6]), axis=-1, keepdims=True))
    fixed_shift_ok = bound[0, 0] <= MAX_FIXED_SHIFT
    q2s = []
    for c in range(n_sub):
        q = q_ref[c * tq:(c + 1) * tq, :]
        lane = lax.broadcasted_iota(jnp.int32, q.shape, 1)
        zero = jnp.zeros_like(q)
        q2s.append(jnp.concatenate([jnp.where(lane < HEAD, q, zero), jnp.where(lane >= HEAD, q, zero)], axis=0))

    hw = 2 * HEAD
    ones_rows = 16

    def run(shift):
        def step(k, v_t, carry):
            vt = jnp.concatenate([v_t, jnp.ones((ones_rows, k.shape[0]), BF16)], axis=0)
            states = [list(st) for st in carry]
            _interleave([_attn_chain(q2, k, vt, st, shift) for q2, st in zip(q2s, states)])
            return tuple(tuple(st) for st in states)

        def latent_step(j, carry):
            ks = pl.ds(pl.multiple_of(j * tk, tk), tk)
            return step(kl_ref[ks, :], vtl_ref[:, ks], carry)

        init = (jnp.full((1, 2 * tq), -jnp.inf, F32), jnp.zeros((hw + ones_rows, 2 * tq), F32))
        carry = step(kc_ref[...], vtc_ref[...], (init,) * n_sub)
        carry = lax.fori_loop(0, kl_ref.shape[0] // tk, latent_step, carry)
        for c, (_, acc) in enumerate(carry):
            acc_t, l = acc[:hw], acc[hw:hw + 1]
            o_t = acc_t[:, :tq] / l[:, :tq] - lam * (acc_t[:, tq:] / l[:, tq:])
            o = o_t.T
            o = o * lax.rsqrt(jnp.mean(o * o, axis=-1, keepdims=True) + SUBLN_EPS) * sub_ref[...]
            o_ref[c * tq:(c + 1) * tq, :] = (o * (1.0 - lambda_init)).astype(o_ref.dtype)

    @pl.when(fixed_shift_ok)
    def _():
        run(bound)

    @pl.when(jnp.logical_not(fixed_shift_ok))
    def _():
        run(None)


def _diff_attention(q, k_c, vt_c, k_l, vt_l, lam_params, sub_gain, lambda_init):
    b, t, d = q.shape
    tc = k_c.shape[1]
    hw = 2 * HEAD
    tq = 256
    n_sub = math.gcd(t // tq, ATTN_CHAINS)
    tk = min(t, 2048)
    assert t % (tq * n_sub) == 0 and t % tk == 0 and tc % 128 == 0
    qspec = pl.BlockSpec((None, tq * n_sub, hw), lambda i, h, j: (i, j, h))

    def kspec(n):
        return pl.BlockSpec((None, n, hw), lambda i, h, j: (i, 0, h))

    def vspec(n):
        return pl.BlockSpec((None, hw, n), lambda i, h, j: (i, h, 0))

    vmem = 4 * (t + tc) * hw * 2 + 4 * n_sub * tq * hw * 2 + n_sub * 8 * 2 * tq * tk * 4 + (8 << 20)
    return pl.pallas_call(
        functools.partial(_attn_kernel, tk=tk, tq=tq, lambda_init=lambda_init),
        grid=(b, d // hw, t // (tq * n_sub)),
        in_specs=[pl.BlockSpec(lam_params.shape, lambda i, h, j: (0, 0)), qspec,
                  kspec(tc), vspec(tc), kspec(t), vspec(t), pl.BlockSpec((1, hw), lambda i, h, j: (0, 0))],
        out_specs=qspec,
        out_shape=jax.ShapeDtypeStruct((b, t, d), BF16),
        compiler_params=_cparams(("parallel", "parallel", "parallel"), vmem),
        name="diff_attention",
    )(lam_params, q, k_c, vt_c, k_l, vt_l, sub_gain.reshape(1, hw))


def _rope_tables(n_tokens):
    n_freq = HEAD // 4
    pos = jnp.arange(n_tokens)
    freqs = ROPE_BASE ** (-jnp.arange(n_freq, dtype=F32) / n_freq)
    ang_row = (pos // GRID_W).astype(F32)[:, None] * freqs
    ang_col = (pos % GRID_W).astype(F32)[:, None] * freqs
    cos = jnp.concatenate([jnp.cos(ang_row)] * 2 + [jnp.cos(ang_col)] * 2, axis=-1)
    sin = jnp.concatenate([-jnp.sin(ang_row), jnp.sin(ang_row), -jnp.sin(ang_col), jnp.sin(ang_col)], axis=-1)
    reps = GROUP // HEAD
    return jnp.tile(cos, (1, reps)), jnp.tile(sin, (1, reps))


def _mod_rows(mod, layer, j, d, batch):
    m = mod[layer, :, j * d:(j + 1) * d]
    return m[:batch, None, :], m[batch:batch + 1, None, :]


def _rwkv_layer(xs, mods, norm_g, p):
    rkv, lora, outs = [], [], []
    for x, (sh, sc, _) in zip(xs, mods):
        xr, xw, xk, xv, xa, xg = _rwkv_mix(x, norm_g, sh, sc, p["mu"])
        rkv.append(_rwkv_rkv(xr, xk, xv, p["wr"], p["wk"], p["wv"]))
        lora.append(_rwkv_lora(xw, xa, xg, p["w0"], p["w1"], p["w2"], p["a0"], p["a1"], p["a2"], p["g1"], p["g2"]))
    b, _, d = xs[0].shape
    state = jnp.zeros((b, 2, d // GROUP, GROUP, GROUP), F32)
    for x, (_, _, gate), (r, k, v), (g, lw0, lw1, as0, as1) in zip(xs, mods, rkv, lora):
        y0, y1, state = _rwkv_scan(r, k, v, lw0, as0, lw1, as1, p["kk"], p["ka"], state)
        z = _rwkv_readout(y0, y1, r, k, v, as0, as1, g, p["ka"], p["rk"], p["lnw"], p["lnb"])
        outs.append(_matmul_res(z, p["wo"], x, gate))
    return outs


def kernel(x, c, ctx, c_ctx, mod_w, mod_b, norm1_g, norm2_g, rwkv_mu, rwkv_wr, rwkv_wk, rwkv_wv, rwkv_wo, rwkv_w0, rwkv_w1, rwkv_w2, rwkv_a0, rwkv_a1, rwkv_a2, rwkv_g1, rwkv_g2, rwkv_kk, rwkv_ka, rwkv_rk, rwkv_lnw, rwkv_lnb, diff_wqkv, diff_qn, diff_kn, diff_lq1, diff_lk1, diff_lq2, diff_lk2, diff_subln, diff_wo, ffn_wg, ffn_wu, ffn_wd):
    batch, n_lat, d = x.shape
    depth = mod_w.shape[0]
    assert depth == 2, "layer 0 is the RWKV-7 mixer, layer 1 differential attention"

    cvecs = jnp.zeros((SUBLANES, d), F32).at[:batch].set(c).at[batch].set(c_ctx)
    mod = _adaln(cvecs, mod_w, mod_b)

    def mods(layer, first):
        lat, cx = zip(*[_mod_rows(mod, layer, first + j, d, batch) for j in range(3)])
        return cx, lat

    w1, w2 = _pad_lora(rwkv_w1[0], rwkv_w2[0])
    a1, a2 = _pad_lora(rwkv_a1[0], rwkv_a2[0])
    g1, g2 = _pad_lora(rwkv_g1[0], rwkv_g2[0])
    p = dict(mu=rwkv_mu[0], wr=_weight_bf16(rwkv_wr, 0), wk=_weight_bf16(rwkv_wk, 0), wv=_weight_bf16(rwkv_wv, 0),
             wo=_weight_bf16(rwkv_wo, 0), w0=rwkv_w0[0], w1=w1, w2=w2, a0=rwkv_a0[0], a1=a1, a2=a2,
             g1=g1, g2=g2, kk=rwkv_kk[0], ka=rwkv_ka[0],
             rk=rwkv_rk[0], lnw=rwkv_lnw[0], lnb=rwkv_lnb[0])
    xc, xl = _rwkv_layer((ctx, x), mods(0, 0), norm1_g[0], p)
    wg, wu, wd = _weight_bf16(ffn_wg, 0), _weight_bf16(ffn_wu, 0), _weight_bf16(ffn_wd, 0)
    (csh, csc, cgt), (lsh, lsc, lgt) = mods(0, 3)
    xc = _ffn(xc, norm2_g[0], csh, csc, cgt, wg, wu, wd)
    xl = _ffn(xl, norm2_g[0], lsh, lsc, lgt, wg, wu, wd)

    (csh, csc, _), (lsh, lsc, lgt) = mods(1, 0)
    lambda_init = 0.8 - 0.6 * math.exp(-0.3 * 1)
    wqkv = _weight_bf16(diff_wqkv, 0)
    qn = jnp.tile(diff_qn[0], d // HEAD).reshape(1, d)
    kn = jnp.tile(diff_kn[0], d // HEAD).reshape(1, d)
    cos, sin = _rope_tables(n_lat)
    q_l, k_l, v_l = _qkv_prep(_norm_matmul(xl, norm1_g[1], lsh, lsc, wqkv), qn, kn, cos, sin, with_q=True)
    k_c, v_c = _qkv_prep(_norm_matmul(xc, norm1_g[1], csh, csc, wqkv, col0=d), None, kn, None, None, with_q=False)
    lam_params = jnp.stack([diff_lq1[0], diff_lk1[0], diff_lq2[0], diff_lk2[0], diff_qn[0], diff_kn[0]])
    o = _diff_attention(q_l, k_c, v_c, k_l, v_l, lam_params, diff_subln[0], lambda_init)
    xl = _matmul_res(o, _weight_bf16(diff_wo, 0), xl, lgt)
    (_, _, _), (lsh, lsc, lgt) = mods(1, 3)
    return _ffn(xl, norm2_g[1], lsh, lsc, lgt,
                _weight_bf16(ffn_wg, 1), _weight_bf16(ffn_wu, 1), _weight_bf16(ffn_wd, 1))
```

```python
import functools
import math

import jax
import jax.numpy as jnp
from jax import lax
from jax.experimental import pallas as pl
from jax.experimental.pallas import tpu as pltpu

F32 = jnp.float32
BF16 = jnp.bfloat16

HEAD = 64
GROUP = 256
CHUNK = 64
SCAN_UNROLL = 8
SCAN_SKEW = 4
ATTN_CHAINS = 4
FFN_ACC_COLS = 512
CAST_ROWS = 512
SUBLANES = 8
V7X_VMEM_CAP = 56 * 1024 * 1024

NORM_EPS = 1e-6
GN_EPS = 64e-5
L2_EPS = 1e-24
QK_EPS = 1e-6
SUBLN_EPS = 1e-5
ROPE_BASE = 10000.0
GRID_W = 64
DECAY_SCALE = math.exp(-0.5)
LOG2E = math.log2(math.e)
SCORE_MARGIN = 1.01
MAX_FIXED_SHIFT = 50.0


def _cparams(semantics, vmem_bytes):
    return pltpu.CompilerParams(dimension_semantics=semantics,
                                vmem_limit_bytes=int(min(max(vmem_bytes, 16 << 20), V7X_VMEM_CAP)))


def _dot(a, b):
    return jnp.dot(a, b, preferred_element_type=F32)


def _dot_nt(a, b):
    return lax.dot_general(a, b, (((1,), (1,)), ((), ())), preferred_element_type=F32)


def _dot_tn(a, b):
    return lax.dot_general(a, b, (((0,), (0,)), ((), ())), preferred_element_type=F32)


def _split_dot(x, w_bf16):
    hi = x.astype(BF16)
    lo = (x - hi.astype(F32)).astype(BF16)
    return _dot(hi, w_bf16) + _dot(lo, w_bf16)


def _block_mask(n):
    r = lax.broadcasted_iota(jnp.int32, (n, n), 0) // HEAD
    c = lax.broadcasted_iota(jnp.int32, (n, n), 1) // HEAD
    return r == c


def _cast_kernel(w_ref, o_ref):
    o_ref[...] = w_ref[...].astype(o_ref.dtype)


def _weight_bf16(w, layer):
    _, rows, cols = w.shape
    tr = CAST_ROWS
    assert rows % tr == 0
    return pl.pallas_call(
        _cast_kernel,
        grid=(rows // tr,),
        in_specs=[pl.BlockSpec((None, tr, cols), lambda i: (layer, i, 0))],
        out_specs=pl.BlockSpec((tr, cols), lambda i: (i, 0)),
        out_shape=jax.ShapeDtypeStruct((rows, cols), BF16),
        compiler_params=_cparams(("parallel",), 2 * tr * cols * 6 + tr * cols * 4),
        name="weight_bf16",
    )(w)


def _adaln_kernel(c_ref, w_ref, b_ref, o_ref):
    c = c_ref[...]
    s = (c * jax.nn.sigmoid(c)).astype(BF16)
    o_ref[...] = _dot(s, w_ref[...].astype(BF16)) + b_ref[...]


def _adaln(cvecs, mod_w, mod_b):
    depth, d, n = mod_w.shape
    rows = cvecs.shape[0]
    tn = 1024
    return pl.pallas_call(
        _adaln_kernel,
        grid=(depth, n // tn),
        in_specs=[pl.BlockSpec((rows, d), lambda l, j: (0, 0)),
                  pl.BlockSpec((None, d, tn), lambda l, j: (l, 0, j)),
                  pl.BlockSpec((None, 1, tn), lambda l, j: (l, 0, j))],
        out_specs=pl.BlockSpec((None, rows, tn), lambda l, j: (l, 0, j)),
        out_shape=jax.ShapeDtypeStruct((depth, rows, n), F32),
        compiler_params=_cparams(("parallel", "parallel"), 3 * d * tn * 4 + (8 << 20)),
        name="adaln",
    )(cvecs, mod_w, mod_b.reshape(depth, 1, n))


def _norm_mod_value(x, g, sh, sc):
    y = x * lax.rsqrt(jnp.mean(x * x, axis=-1, keepdims=True) + NORM_EPS)
    return (y * g) * (1.0 + sc) + sh


def _row_spec(arr, d):
    if arr.shape[0] == 1:
        return pl.BlockSpec((None, 1, d), lambda b, *_: (0, 0, 0))
    return pl.BlockSpec((None, 1, d), lambda b, *_: (b, 0, 0))


def _norm_specs(g, sh, sc, d):
    return [pl.BlockSpec((1, d), lambda b, *_: (0, 0)), _row_spec(sh, d), _row_spec(sc, d)]


def _mix_kernel(x_ref, xp_ref, xn_ref, g_ref, sh_ref, sc_ref, mu_ref, *o_refs, nt):
    t = pl.program_id(1)
    tm, d = x_ref.shape

    def inv_rms(ref):
        x = ref[...]
        return lax.rsqrt(jnp.mean(x * x, axis=-1, keepdims=True) + NORM_EPS)

    r, r_prev, r_next = inv_rms(x_ref), inv_rms(xp_ref), inv_rms(xn_ref)
    row = lax.broadcasted_iota(jnp.int32, (tm, GROUP), 0)
    for c0 in range(0, d, GROUP):
        sl = slice(c0, c0 + GROUP)
        g, sh, sc = g_ref[:, sl], sh_ref[:, sl], sc_ref[:, sl]

        def norm(ref, rinv):
            return ((ref[:, sl] * rinv) * g) * (1.0 + sc) + sh

        h = norm(x_ref, r)
        prev_row = jnp.where(t > 0, norm(xp_ref, r_prev)[SUBLANES - 1:SUBLANES, :], 0.0)
        next_row = jnp.where(t < nt - 1, norm(xn_ref, r_next)[0:1, :], 0.0)
        hp = jnp.where(row == 0, prev_row, pltpu.roll(h, 1, 0))
        hn = jnp.where(row == tm - 1, next_row, pltpu.roll(h, tm - 1, 0))
        xx = 0.5 * (hp + hn) - h
        for j, o_ref in enumerate(o_refs):
            o_ref[:, sl] = (h + xx * mu_ref[j:j + 1, sl]).astype(o_ref.dtype)


def _rwkv_mix(x, g, sh, sc, mu):
    b, t, d = x.shape
    n_mix = mu.shape[0]
    tm = min(t, 256)
    nt = t // tm
    per = tm // SUBLANES
    last = t // SUBLANES - 1
    spec = pl.BlockSpec((None, tm, d), lambda i, j: (i, j, 0))
    halo = [pl.BlockSpec((None, SUBLANES, d), lambda i, j: (i, jnp.maximum(j * per - 1, 0), 0)),
            pl.BlockSpec((None, SUBLANES, d), lambda i, j: (i, jnp.minimum((j + 1) * per, last), 0))]
    return pl.pallas_call(
        functools.partial(_mix_kernel, nt=nt),
        grid=(b, nt),
        in_specs=[spec] + halo + _norm_specs(g, sh, sc, d) + [pl.BlockSpec(mu.shape, lambda i, j: (0, 0))],
        out_specs=[spec] * n_mix,
        out_shape=[jax.ShapeDtypeStruct((b, t, d), BF16)] * n_mix,
        compiler_params=_cparams(("parallel", "parallel"), 2 * tm * d * (4 + 2 * n_mix) + 8 * tm * d * 4),
        name="rwkv_mix",
    )(x, x, x, g.reshape(1, d), sh, sc, mu)


def _rkv_kernel(xr_ref, xk_ref, xv_ref, wr_ref, wk_ref, wv_ref, r_ref, k_ref, v_ref):
    r_ref[...] = _dot(xr_ref[...], wr_ref[...])
    k_ref[...] = _dot(xk_ref[...], wk_ref[...])
    v_ref[...] = _dot(xv_ref[...], wv_ref[...])


def _rwkv_rkv(xr, xk, xv, wr, wk, wv):
    b, t, d = xr.shape
    tm = min(t, 1024)
    tn = 512
    out = jax.ShapeDtypeStruct((b, t, d), F32)
    xspec = pl.BlockSpec((None, tm, d), lambda i, j, n: (i, j, 0))
    wspec = pl.BlockSpec((d, tn), lambda i, j, n: (0, n))
    ospec = pl.BlockSpec((None, tm, tn), lambda i, j, n: (i, j, n))
    vmem = 6 * tm * d * 2 + 6 * d * tn * 2 + 9 * tm * tn * 4 + (4 << 20)
    return pl.pallas_call(
        _rkv_kernel,
        grid=(b, t // tm, d // tn),
        in_specs=[xspec, xspec, xspec, wspec, wspec, wspec],
        out_specs=[ospec, ospec, ospec],
        out_shape=[out, out, out],
        compiler_params=_cparams(("parallel", "parallel", "parallel"), vmem),
        name="rwkv_rkv",
    )(xr, xk, xv, wr, wk, wv)


def _lora_kernel(xw_ref, xa_ref, xg_ref, w0_ref, w1_ref, w2_ref, a0_ref, a1_ref, a2_ref, g1_ref, g2_ref,
                 g_ref, lw0_ref, lw1_ref, as0_ref, as1_ref):
    xw, xa, xg = xw_ref[...], xa_ref[...], xg_ref[...]
    g_ref[...] = _dot(jax.nn.sigmoid(_dot(xg, g1_ref[...])).astype(BF16), g2_ref[...]).astype(g_ref.dtype)
    for d, (lw_ref, as_ref) in enumerate(((lw0_ref, as0_ref), (lw1_ref, as1_ref))):
        w_pre = w0_ref[d:d + 1, :] + _dot(jnp.tanh(_dot(xw, w1_ref[d])).astype(BF16), w2_ref[d])
        lw_ref[...] = (-DECAY_SCALE * LOG2E) * jax.nn.sigmoid(w_pre)
        a_pre = a0_ref[d:d + 1, :] + _dot(_dot(xa, a1_ref[d]).astype(BF16), a2_ref[d])
        as_ref[...] = jax.nn.sigmoid(a_pre).astype(as_ref.dtype)


def _pad_lora(w_in, w_out):
    rank = w_in.shape[-1]
    pad = (-rank) % 128
    w_in = jnp.pad(w_in, [(0, 0)] * (w_in.ndim - 1) + [(0, pad)])
    w_out = jnp.pad(w_out, [(0, 0)] * (w_out.ndim - 2) + [(0, pad), (0, 0)])
    return w_in.astype(BF16), w_out.astype(BF16)


def _rwkv_lora(xw, xa, xg, w0, w1, w2, a0, a1, a2, g1, g2):
    b, t, d = xw.shape
    tm = min(t, 256)
    out = jax.ShapeDtypeStruct((b, t, d), F32)
    out16 = jax.ShapeDtypeStruct((b, t, d), BF16)
    spec = pl.BlockSpec((None, tm, d), lambda i, j: (i, j, 0))

    def full(a):
        return pl.BlockSpec(a.shape, lambda i, j, _n=a.ndim: (0,) * _n)

    params = (w0, w1, w2, a0, a1, a2, g1, g2)
    vmem = 2 * tm * d * (3 * 2 + 2 * 4 + 3 * 2) + 10 * tm * d * 4 + 2 * sum(p.size * p.dtype.itemsize for p in params)
    return pl.pallas_call(
        _lora_kernel,
        grid=(b, t // tm),
        in_specs=[spec] * 3 + [full(p) for p in params],
        out_specs=[spec] * 5,
        out_shape=[out16, out, out, out16, out16],
        compiler_params=_cparams(("parallel", "parallel"), vmem),
        name="rwkv_lora",
    )(xw, xa, xg, *params)


def _chunk_cumsum(x, reverse):
    n = x.shape[0]
    row = lax.broadcasted_iota(jnp.int32, x.shape, 0)
    shift = 1
    while shift < n:
        if reverse:
            x = x + jnp.where(row < n - shift, pltpu.roll(x, n - shift, 0), 0.0)
        else:
            x = x + jnp.where(row >= shift, pltpu.roll(x, shift, 0), 0.0)
        shift *= 2
    return x


def _scan_chain(d, refs, kks, ka, state_ref, y_ref, sl, g, consts):
    r_ref, k_ref, v_ref, lw_ref, as_ref = refs
    bmask, ones_bd, m_strict, m_incl = consts
    c = CHUNK

    def stack(x):
        return jnp.where(bmask, jnp.concatenate([x] * (GROUP // c), axis=0), 0.0).astype(BF16)

    lw = lw_ref[:, sl]
    a_sig = as_ref[:, sl].astype(F32)
    k = k_ref[:, sl]
    kk = k * kks
    ssq = _dot((kk * kk).astype(BF16), ones_bd)
    cum = _chunk_cumsum(lw, reverse=d == 1)
    yield

    kk = kk * lax.rsqrt(jnp.maximum(ssq, L2_EPS))
    b_vec = kk * a_sig
    k_d = k * (1.0 + (a_sig - 1.0) * ka)
    tot = cum[c - 1:c, :] if d == 0 else cum[0:1, :]
    a_t = -kk * jnp.exp2(cum - lw)
    r_t = r_ref[:, sl] * jnp.exp2(cum)
    p_inv = jnp.exp2(-cum)
    lhs = jnp.concatenate([a_t, r_t], axis=0).astype(BF16)
    rhs = jnp.concatenate([stack(b_vec * p_inv), stack(k_d * p_inv)], axis=0)
    aa = _dot_nt(lhs, rhs)
    yield

    n1 = jnp.where(m_strict[d], aa[:c, :GROUP], 0.0)
    a_ak = jnp.where(m_strict[d], aa[:c, GROUP:], 0.0)
    a_rb = jnp.where(m_incl[d], aa[c:, :GROUP], 0.0).astype(BF16)
    a_rk = jnp.where(m_incl[d], aa[c:, GROUP:], 0.0)
    v = v_ref[:, sl]
    z1 = _dot(jnp.concatenate([a_ak, a_rk], axis=0).astype(BF16), stack(v))
    q = n1
    npow = _dot(n1.astype(BF16), stack(n1))
    yield
    span = 2
    while 2 * span < c:
        z = _dot(jnp.concatenate([npow, q], axis=0).astype(BF16), stack(npow))
        yield
        q = q + npow + z[c:]
        npow = z[:c]
        span *= 2
    zq = _dot(q.astype(BF16), stack(npow))
    yield
    q = q + npow + zq

    v_ak = z1[:c]
    y0 = z1[c:]
    z2 = _dot(q.astype(BF16), jnp.concatenate([stack(a_t), stack(v_ak)], axis=1))
    yield
    w = a_t + z2[:, :GROUP]
    u0 = v_ak + z2[:, GROUP:]

    s = state_ref[d, g]
    z3 = _dot_nt(jnp.concatenate([w, r_t], axis=0).astype(BF16), s.astype(BF16))
    yield
    u = z3[:c] + u0
    p_end = jnp.exp2(tot - cum)
    yu = _dot(a_rb, stack(u))
    upd = _dot_tn(jnp.concatenate([u, v], axis=0).astype(BF16),
                  jnp.concatenate([b_vec * p_end, k_d * p_end], axis=0).astype(BF16))
    yield
    y_ref[:, sl] = (z3[c:] + yu + y0).astype(y_ref.dtype)
    state_ref[d, g] = s * jnp.exp2(tot) + jnp.where(bmask, upd, 0.0)


def _interleave(chains, skew=0.0):
    pending = list(enumerate(chains))
    rnd = 0
    while pending:
        alive = []
        for i, ch in pending:
            if rnd >= (i % 2) * skew:
                try:
                    next(ch)
                except StopIteration:
                    continue
            alive.append((i, ch))
        pending = alive
        rnd += 1


def _scan_kernel(r0, k0, v0, lw0, as0, r1, k1, v1, lw1, as1, kks_ref, ka_ref, sin_ref,
                 y0_ref, y1_ref, state_ref, *, groups):
    i = pl.program_id(1)
    c = CHUNK
    unroll = math.gcd(groups, SCAN_UNROLL)

    @pl.when(i == 0)
    def _():
        state_ref[...] = sin_ref[...]

    bmask = _block_mask(GROUP)
    ones_bd = jnp.where(bmask, 1.0, 0.0).astype(BF16)
    tp = lax.broadcasted_iota(jnp.int32, (c, GROUP), 0)
    sp = lax.broadcasted_iota(jnp.int32, (c, GROUP), 1) % c
    m_strict = (sp < tp, sp > tp)
    m_incl = (sp <= tp, sp >= tp)
    consts = (bmask, ones_bd, m_strict, m_incl)

    def body(it, carry):
        chains = []
        for j in range(unroll):
            g = it * unroll + j
            sl = pl.ds(pl.multiple_of(g * GROUP, GROUP), GROUP)
            kks = kks_ref[:, sl]
            ka = ka_ref[:, sl]
            chains.append(_scan_chain(0, (r0, k0, v0, lw0, as0), kks, ka, state_ref, y0_ref, sl, g, consts))
            chains.append(_scan_chain(1, (r1, k1, v1, lw1, as1), kks, ka, state_ref, y1_ref, sl, g, consts))
        _interleave(chains, SCAN_SKEW)
        return carry

    lax.fori_loop(0, groups // unroll, body, 0)


def _rwkv_scan(r, k, v, lw0, as0, lw1, as1, kk_scale, ka, s_init):
    b, t, d = r.shape
    assert CHUNK * (GROUP // HEAD) == GROUP and t % CHUNK == 0 and d % GROUP == 0
    groups = d // GROUP
    nc = t // CHUNK
    fwd = pl.BlockSpec((None, CHUNK, d), lambda i, j: (i, j, 0))
    bwd = pl.BlockSpec((None, CHUNK, d), lambda i, j: (i, nc - 1 - j, 0))
    vec = pl.BlockSpec((1, d), lambda i, j: (0, 0))
    sspec = pl.BlockSpec((None, 2, groups, GROUP, GROUP), lambda i, j: (i, 0, 0, 0, 0))
    y = jax.ShapeDtypeStruct((b, t, d), BF16)
    state_bytes = 2 * groups * GROUP * GROUP * 4
    vmem = 2 * 12 * CHUNK * d * 4 + 4 * state_bytes + 2 * SCAN_UNROLL * (3 << 19) + (4 << 20)
    return pl.pallas_call(
        functools.partial(_scan_kernel, groups=groups),
        grid=(b, nc),
        in_specs=[fwd] * 5 + [bwd] * 5 + [vec, vec, sspec],
        out_specs=[fwd, bwd, sspec],
        out_shape=[y, y, jax.ShapeDtypeStruct(s_init.shape, F32)],
        compiler_params=_cparams(("parallel", "arbitrary"), vmem),
        name="rwkv_scan",
    )(r, k, v, lw0, as0, r, k, v, lw1, as1, kk_scale.reshape(1, d), ka.reshape(1, d), s_init)


def _readout_kernel(y0_ref, y1_ref, r_ref, k_ref, v_ref, as0_ref, as1_ref, g_ref,
                    ka_ref, rk_ref, lnw_ref, lnb_ref, o_ref):
    d = o_ref.shape[-1]
    ones_bd = jnp.where(_block_mask(GROUP), 1.0, 0.0).astype(BF16)
    for j in range(d // GROUP):
        sl = slice(j * GROUP, (j + 1) * GROUP)
        y = y0_ref[:, sl].astype(F32) + y1_ref[:, sl].astype(F32)
        mean = _split_dot(y, ones_bd) * (1.0 / HEAD)
        yc = y - mean
        var = _split_dot(yc * yc, ones_bd) * (1.0 / HEAD)
        yn = (yc * lax.rsqrt(var + GN_EPS)) * lnw_ref[:, sl] + lnb_ref[:, sl]
        k = k_ref[:, sl]
        ka = ka_ref[:, sl]
        a0, a1 = as0_ref[:, sl].astype(F32), as1_ref[:, sl].astype(F32)
        k_sum = k * (1.0 + (a0 - 1.0) * ka) + k * (1.0 + (a1 - 1.0) * ka)
        bonus = _split_dot(r_ref[:, sl] * k_sum * rk_ref[:, sl], ones_bd) * v_ref[:, sl]
        o_ref[:, sl] = ((yn + bonus) * g_ref[:, sl].astype(F32)).astype(o_ref.dtype)


def _rwkv_readout(y0, y1, r, k, v, as0, as1, g, ka, rk, lnw, lnb):
    b, t, d = r.shape
    tm = min(t, 256)
    spec = pl.BlockSpec((None, tm, d), lambda i, j: (i, j, 0))
    vec = pl.BlockSpec((1, d), lambda i, j: (0, 0))
    return pl.pallas_call(
        _readout_kernel,
        grid=(b, t // tm),
        in_specs=[spec] * 8 + [vec] * 4,
        out_specs=spec,
        out_shape=jax.ShapeDtypeStruct((b, t, d), BF16),
        compiler_params=_cparams(("parallel", "parallel"), 2 * 9 * tm * d * 4 + (8 << 20)),
        name="rwkv_readout",
    )(y0, y1, r, k, v, as0, as1, g, ka.reshape(1, d), rk.reshape(1, d), lnw.reshape(1, d), lnb.reshape(1, d))


def _mm_res_kernel(z_ref, w_ref, res_ref, gate_ref, o_ref):
    o_ref[...] = res_ref[...] + gate_ref[...] * _dot(z_ref[...], w_ref[...])


def _norm_mm_kernel(x_ref, g_ref, sh_ref, sc_ref, w_ref, o_ref, h_ref):
    @pl.when(pl.program_id(2) == 0)
    def _():
        h_ref[...] = _norm_mod_value(x_ref[...], g_ref[...], sh_ref[...], sc_ref[...]).astype(BF16)

    o_ref[...] = _dot(h_ref[...], w_ref[...])


def _norm_matmul(x, g, sh, sc, w, col0=0):
    b, t, d = x.shape
    n = w.shape[1] - col0
    tm = min(t, 1024)
    tn = 512
    assert col0 % tn == 0
    cb0 = col0 // tn
    vmem = 2 * tm * d * 4 + tm * d * 2 + 2 * d * tn * 2 + 3 * tm * tn * 4 + tm * d * 4 + (4 << 20)
    return pl.pallas_call(
        _norm_mm_kernel,
        grid=(b, t // tm, n // tn),
        in_specs=[pl.BlockSpec((None, tm, d), lambda i, j, l: (i, j, 0))] + _norm_specs(g, sh, sc, d)
                 + [pl.BlockSpec((d, tn), lambda i, j, l: (0, l + cb0))],
        out_specs=pl.BlockSpec((None, tm, tn), lambda i, j, l: (i, j, l)),
        out_shape=jax.ShapeDtypeStruct((b, t, n), F32),
        scratch_shapes=[pltpu.VMEM((tm, d), BF16)],
        compiler_params=_cparams(("parallel", "parallel", "arbitrary"), vmem),
        name="norm_matmul",
    )(x, g.reshape(1, d), sh, sc, w)


def _matmul_res(z, w, res, gate):
    b, t, kdim = z.shape
    n = w.shape[1]
    tm = min(t, 1024)
    tn = 512
    zspec = pl.BlockSpec((None, tm, kdim), lambda i, j, l: (i, j, 0))
    wspec = pl.BlockSpec((kdim, tn), lambda i, j, l: (0, l))
    ospec = pl.BlockSpec((None, tm, tn), lambda i, j, l: (i, j, l))
    gspec = (pl.BlockSpec((None, 1, tn), lambda i, j, l: (0, 0, l)) if gate.shape[0] == 1
             else pl.BlockSpec((None, 1, tn), lambda i, j, l: (i, 0, l)))
    vmem = 2 * (tm * kdim * 2 + kdim * tn * 2 + 2 * tm * tn * 4) + tm * tn * 4 + (4 << 20)
    return pl.pallas_call(
        _mm_res_kernel,
        grid=(b, t // tm, n // tn),
        in_specs=[zspec, wspec, ospec, gspec],
        out_specs=ospec,
        out_shape=jax.ShapeDtypeStruct((b, t, n), F32),
        compiler_params=_cparams(("parallel", "parallel", "parallel"), vmem),
        name="matmul_res",
    )(z, w, res, gate)


def _ffn_kernel(x_ref, g_ref, sh_ref, sc_ref, gate_ref, wg_ref, wu_ref, wd_ref, o_ref, h_ref):
    f = pl.program_id(2)

    @pl.when(f == 0)
    def _():
        h_ref[...] = _norm_mod_value(x_ref[...], g_ref[...], sh_ref[...], sc_ref[...]).astype(BF16)
        o_ref[...] = jnp.zeros_like(o_ref)

    h = h_ref[...]
    gp = _dot(h, wg_ref[...])
    act = ((gp * jax.nn.sigmoid(gp)) * _dot(h, wu_ref[...])).astype(BF16)
    d = o_ref.shape[-1]
    for c0 in range(0, d, FFN_ACC_COLS):
        cols = slice(c0, c0 + FFN_ACC_COLS)
        o_ref[:, cols] += _dot(act, wd_ref[:, cols])

    @pl.when(f == pl.num_programs(2) - 1)
    def _():
        o_ref[...] = x_ref[...] + gate_ref[...] * o_ref[...]


def _ffn(x, g, sh, sc, gate, wg, wu, wd):
    b, t, d = x.shape
    ff = wg.shape[1]
    tm = min(t, 512)
    tf = 512
    xspec = pl.BlockSpec((None, tm, d), lambda i, j, f: (i, j, 0))
    vmem = 4 * tm * d * 4 + tm * d * 2 + 2 * 3 * d * tf * 2 + 6 * tm * tf * 4 + tm * d * 4 + (4 << 20)
    return pl.pallas_call(
        _ffn_kernel,
        grid=(b, t // tm, ff // tf),
        in_specs=[xspec, pl.BlockSpec((1, d), lambda i, j, f: (0, 0)),
                  _row_spec(sh, d), _row_spec(sc, d), _row_spec(gate, d),
                  pl.BlockSpec((d, tf), lambda i, j, f: (0, f)),
                  pl.BlockSpec((d, tf), lambda i, j, f: (0, f)),
                  pl.BlockSpec((tf, d), lambda i, j, f: (f, 0))],
        out_specs=xspec,
        out_shape=jax.ShapeDtypeStruct((b, t, d), F32),
        scratch_shapes=[pltpu.VMEM((tm, d), BF16)],
        compiler_params=_cparams(("parallel", "parallel", "arbitrary"), vmem),
        name="ffn",
    )(x, g.reshape(1, d), sh, sc, gate, wg, wu, wd)


def _qk_norm(x, gain, ones_bd):
    ms = _split_dot(x * x, ones_bd) * (1.0 / HEAD)
    return x * lax.rsqrt(ms + QK_EPS) * gain


def _rope(u, cos, sin):
    n = u.shape[-1]
    lane = lax.broadcasted_iota(jnp.int32, u.shape, 1)
    partner = jnp.where(lane % 32 < 16, pltpu.roll(u, n - 16, 1), pltpu.roll(u, 16, 1))
    return u * cos + partner * sin


def _qkv_prep_kernel(*refs, rope, with_q):
    refs = list(refs)
    q_ref = refs.pop(0) if with_q else None
    k_ref, v_ref = refs.pop(0), refs.pop(0)
    qn_ref = refs.pop(0) if with_q else None
    kn_ref = refs.pop(0)
    if rope:
        cos_ref, sin_ref = refs.pop(0), refs.pop(0)
    qo_ref = refs.pop(0) if with_q else None
    ko_ref, vo_ref = refs
    d = ko_ref.shape[-1]
    ones_bd = jnp.where(_block_mask(GROUP), 1.0, 0.0).astype(BF16)
    if rope:
        cos, sin = cos_ref[...], sin_ref[...]
    for j in range(d // GROUP):
        sl = slice(j * GROUP, (j + 1) * GROUP)
        kh = _qk_norm(k_ref[:, sl], kn_ref[:, sl], ones_bd)
        if rope:
            kh = _rope(kh, cos, sin)
        ko_ref[:, sl] = kh.astype(BF16)
        if with_q:
            qh = _qk_norm(q_ref[:, sl], qn_ref[:, sl], ones_bd)
            if rope:
                qh = _rope(qh, cos, sin)
            qo_ref[:, sl] = (qh * (HEAD ** -0.5 * LOG2E)).astype(BF16)
    vo_ref[...] = v_ref[...].T.astype(BF16)


def _qkv_prep(qkv, qn, kn, cos, sin, *, with_q):
    b, t, width = qkv.shape
    parts = 3 if with_q else 2
    d = width // parts
    rope = cos is not None
    tm = min(t, 256)
    col = lambda c: pl.BlockSpec((None, tm, d), lambda i, j, _c=c: (i, j, _c))
    vec = pl.BlockSpec((1, d), lambda i, j: (0, 0))
    tab = pl.BlockSpec((tm, GROUP), lambda i, j: (j, 0))
    in_specs = [col(c) for c in range(parts)] + [vec] * (2 if with_q else 1) + ([tab, tab] if rope else [])
    args = [qkv] * parts + ([qn] if with_q else []) + [kn] + ([cos, sin] if rope else [])
    out = jax.ShapeDtypeStruct((b, t, d), BF16)
    ospec = pl.BlockSpec((None, tm, d), lambda i, j: (i, j, 0))
    out_t = jax.ShapeDtypeStruct((b, d, t), BF16)
    ospec_t = pl.BlockSpec((None, d, tm), lambda i, j: (i, 0, j))
    return pl.pallas_call(
        functools.partial(_qkv_prep_kernel, rope=rope, with_q=with_q),
        grid=(b, t // tm),
        in_specs=in_specs,
        out_specs=[ospec] * (parts - 1) + [ospec_t],
        out_shape=[out] * (parts - 1) + [out_t],
        compiler_params=_cparams(("parallel", "parallel"), 2 * parts * tm * d * 6 + (16 << 20)),
        name="qkv_prep",
    )(*args)


def _attn_chain(q2, k, vt, state, shift):
    m, acc_t = state
    s_t = _dot_nt(k, q2)
    yield
    if shift is None:
        m_new = jnp.maximum(m, jnp.max(s_t, axis=0, keepdims=True))
        alpha = jnp.exp2(m - m_new)
        state[0] = m_new
        p = jnp.exp2(s_t - m_new)
    else:
        p = jnp.exp2(s_t - shift)
    pv = _dot(vt, p.astype(BF16))
    yield
    state[1] = pv + (acc_t if shift is not None else alpha * acc_t)


def _attn_kernel(lam_ref, q_ref, kc_ref, vtc_ref, kl_ref, vtl_ref, sub_ref, o_ref, *, tk, tq, lambda_init):
    n_sub = q_ref.shape[0] // tq
    lp = lam_ref[...]
    lam = (jnp.exp(jnp.sum(lp[0:1] * lp[1:2], axis=-1, keepdims=True))
           - jnp.exp(jnp.sum(lp[2:3] * lp[3:4], axis=-1, keepdims=True)) + lambda_init)
    bound = (HEAD ** 0.5 * LOG2E * SCORE_MARGIN) * (jnp.max(jnp.abs(lp[4:5]), axis=-1, keepdims=True)
                                                    * jnp.max(jnp.abs(lp[5:6]), axis=-1, keepdims=True))
    fixed_shift_ok = bound[0, 0] <= MAX_FIXED_SHIFT
    q2s = []
    for c in range(n_sub):
        q = q_ref[c * tq:(c + 1) * tq, :]
        lane = lax.broadcasted_iota(jnp.int32, q.shape, 1)
        zero = jnp.zeros_like(q)
        q2s.append(jnp.concatenate([jnp.where(lane < HEAD, q, zero), jnp.where(lane >= HEAD, q, zero)], axis=0))

    hw = 2 * HEAD
    ones_rows = 16

    def run(shift):
        def step(k, v_t, carry):
            vt = jnp.concatenate([v_t, jnp.ones((ones_rows, k.shape[0]), BF16)], axis=0)
            states = [list(st) for st in carry]
            _interleave([_attn_chain(q2, k, vt, st, shift) for q2, st in zip(q2s, states)])
            return tuple(tuple(st) for st in states)

        def latent_step(j, carry):
            ks = pl.ds(pl.multiple_of(j * tk, tk), tk)
            return step(kl_ref[ks, :], vtl_ref[:, ks], carry)

        init = (jnp.full((1, 2 * tq), -jnp.inf, F32), jnp.zeros((hw + ones_rows, 2 * tq), F32))
        carry = step(kc_ref[...], vtc_ref[...], (init,) * n_sub)
        carry = lax.fori_loop(0, kl_ref.shape[0] // tk, latent_step, carry)
        for c, (_, acc) in enumerate(carry):
            acc_t, l = acc[:hw], acc[hw:hw + 1]
            o_t = acc_t[:, :tq] / l[:, :tq] - lam * (acc_t[:, tq:] / l[:, tq:])
            o = o_t.T
            o = o * lax.rsqrt(jnp.mean(o * o, axis=-1, keepdims=True) + SUBLN_EPS) * sub_ref[...]
            o_ref[c * tq:(c + 1) * tq, :] = (o * (1.0 - lambda_init)).astype(o_ref.dtype)

    @pl.when(fixed_shift_ok)
    def _():
        run(bound)

    @pl.when(jnp.logical_not(fixed_shift_ok))
    def _():
        run(None)


def _diff_attention(q, k_c, vt_c, k_l, vt_l, lam_params, sub_gain, lambda_init):
    b, t, d = q.shape
    tc = k_c.shape[1]
    hw = 2 * HEAD
    tq = 256
    n_sub = math.gcd(t // tq, ATTN_CHAINS)
    tk = min(t, 2048)
    assert t % (tq * n_sub) == 0 and t % tk == 0 and tc % 128 == 0
    qspec = pl.BlockSpec((None, tq * n_sub, hw), lambda i, h, j: (i, j, h))

    def kspec(n):
        return pl.BlockSpec((None, n, hw), lambda i, h, j: (i, 0, h))

    def vspec(n):
        return pl.BlockSpec((None, hw, n), lambda i, h, j: (i, h, 0))

    vmem = 4 * (t + tc) * hw * 2 + 4 * n_sub * tq * hw * 2 + n_sub * 8 * 2 * tq * tk * 4 + (8 << 20)
    return pl.pallas_call(
        functools.partial(_attn_kernel, tk=tk, tq=tq, lambda_init=lambda_init),
        grid=(b, d // hw, t // (tq * n_sub)),
        in_specs=[pl.BlockSpec(lam_params.shape, lambda i, h, j: (0, 0)), qspec,
                  kspec(tc), vspec(tc), kspec(t), vspec(t), pl.BlockSpec((1, hw), lambda i, h, j: (0, 0))],
        out_specs=qspec,
        out_shape=jax.ShapeDtypeStruct((b, t, d), BF16),
        compiler_params=_cparams(("parallel", "parallel", "parallel"), vmem),
        name="diff_attention",
    )(lam_params, q, k_c, vt_c, k_l, vt_l, sub_gain.reshape(1, hw))


def _rope_tables(n_tokens):
    n_freq = HEAD // 4
    pos = jnp.arange(n_tokens)
    freqs = ROPE_BASE ** (-jnp.arange(n_freq, dtype=F32) / n_freq)
    ang_row = (pos // GRID_W).astype(F32)[:, None] * freqs
    ang_col = (pos % GRID_W).astype(F32)[:, None] * freqs
    cos = jnp.concatenate([jnp.cos(ang_row)] * 2 + [jnp.cos(ang_col)] * 2, axis=-1)
    sin = jnp.concatenate([-jnp.sin(ang_row), jnp.sin(ang_row), -jnp.sin(ang_col), jnp.sin(ang_col)], axis=-1)
    reps = GROUP // HEAD
    return jnp.tile(cos, (1, reps)), jnp.tile(sin, (1, reps))


def _mod_rows(mod, layer, j, d, batch):
    m = mod[layer, :, j * d:(j + 1) * d]
    return m[:batch, None, :], m[batch:batch + 1, None, :]


def _rwkv_layer(xs, mods, norm_g, p):
    rkv, lora, outs = [], [], []
    for x, (sh, sc, _) in zip(xs, mods):
        xr, xw, xk, xv, xa, xg = _rwkv_mix(x, norm_g, sh, sc, p["mu"])
        rkv.append(_rwkv_rkv(xr, xk, xv, p["wr"], p["wk"], p["wv"]))
        lora.append(_rwkv_lora(xw, xa, xg, p["w0"], p["w1"], p["w2"], p["a0"], p["a1"], p["a2"], p["g1"], p["g2"]))
    b, _, d = xs[0].shape
    state = jnp.zeros((b, 2, d // GROUP, GROUP, GROUP), F32)
    for x, (_, _, gate), (r, k, v), (g, lw0, lw1, as0, as1) in zip(xs, mods, rkv, lora):
        y0, y1, state = _rwkv_scan(r, k, v, lw0, as0, lw1, as1, p["kk"], p["ka"], state)
        z = _rwkv_readout(y0, y1, r, k, v, as0, as1, g, p["ka"], p["rk"], p["lnw"], p["lnb"])
        outs.append(_matmul_res(z, p["wo"], x, gate))
    return outs


def kernel(x, c, ctx, c_ctx, mod_w, mod_b, norm1_g, norm2_g, rwkv_mu, rwkv_wr, rwkv_wk, rwkv_wv, rwkv_wo, rwkv_w0, rwkv_w1, rwkv_w2, rwkv_a0, rwkv_a1, rwkv_a2, rwkv_g1, rwkv_g2, rwkv_kk, rwkv_ka, rwkv_rk, rwkv_lnw, rwkv_lnb, diff_wqkv, diff_qn, diff_kn, diff_lq1, diff_lk1, diff_lq2, diff_lk2, diff_subln, diff_wo, ffn_wg, ffn_wu, ffn_wd):
    batch, n_lat, d = x.shape
    depth = mod_w.shape[0]
    assert depth == 2, "layer 0 is the RWKV-7 mixer, layer 1 differential attention"

    cvecs = jnp.zeros((SUBLANES, d), F32).at[:batch].set(c).at[batch].set(c_ctx)
    mod = _adaln(cvecs, mod_w, mod_b)

    def mods(layer, first):
        lat, cx = zip(*[_mod_rows(mod, layer, first + j, d, batch) for j in range(3)])
        return cx, lat

    w1, w2 = _pad_lora(rwkv_w1[0], rwkv_w2[0])
    a1, a2 = _pad_lora(rwkv_a1[0], rwkv_a2[0])
    g1, g2 = _pad_lora(rwkv_g1[0], rwkv_g2[0])
    p = dict(mu=rwkv_mu[0], wr=_weight_bf16(rwkv_wr, 0), wk=_weight_bf16(rwkv_wk, 0), wv=_weight_bf16(rwkv_wv, 0),
             wo=_weight_bf16(rwkv_wo, 0), w0=rwkv_w0[0], w1=w1, w2=w2, a0=rwkv_a0[0], a1=a1, a2=a2,
             g1=g1, g2=g2, kk=rwkv_kk[0], ka=rwkv_ka[0],
             rk=rwkv_rk[0], lnw=rwkv_lnw[0], lnb=rwkv_lnb[0])
    xc, xl = _rwkv_layer((ctx, x), mods(0, 0), norm1_g[0], p)
    wg, wu, wd = _weight_bf16(ffn_wg, 0), _weight_bf16(ffn_wu, 0), _weight_bf16(ffn_wd, 0)
    (csh, csc, cgt), (lsh, lsc, lgt) = mods(0, 3)
    xc = _ffn(xc, norm2_g[0], csh, csc, cgt, wg, wu, wd)
    xl = _ffn(xl, norm2_g[0], lsh, lsc, lgt, wg, wu, wd)

    (csh, csc, _), (lsh, lsc, lgt) = mods(1, 0)
    lambda_init = 0.8 - 0.6 * math.exp(-0.3 * 1)
    wqkv = _weight_bf16(diff_wqkv, 0)
    qn = jnp.tile(diff_qn[0], d // HEAD).reshape(1, d)
    kn = jnp.tile(diff_kn[0], d // HEAD).reshape(1, d)
    cos, sin = _rope_tables(n_lat)
    q_l, k_l, v_l = _qkv_prep(_norm_matmul(xl, norm1_g[1], lsh, lsc, wqkv), qn, kn, cos, sin, with_q=True)
    k_c, v_c = _qkv_prep(_norm_matmul(xc, norm1_g[1], csh, csc, wqkv, col0=d), None, kn, None, None, with_q=False)
    lam_params = jnp.stack([diff_lq1[0], diff_lk1[0], diff_lq2[0], diff_lk2[0], diff_qn[0], diff_kn[0]])
    o = _diff_attention(q_l, k_c, v_c, k_l, v_l, lam_params, diff_subln[0], lambda_init)
    xl = _matmul_res(o, _weight_bf16(diff_wo, 0), xl, lgt)
    (_, _, _), (lsh, lsc, lgt) = mods(1, 3)
    return _ffn(xl, norm2_g[1], lsh, lsc, lgt,
                _weight_bf16(ffn_wg, 1), _weight_bf16(ffn_wu, 1), _weight_bf16(ffn_wd, 1))
```

```python
import functools
import math

import jax
import jax.numpy as jnp
from jax import lax
from jax.experimental import pallas as pl
from jax.experimental.pallas import tpu as pltpu

F32 = jnp.float32
BF16 = jnp.bfloat16

HEAD = 64
GROUP = 256
CHUNK = 64
SCAN_UNROLL = 8
SCAN_SKEW = 4
ATTN_CHAINS = 4
FFN_ACC_COLS = 512
CAST_ROWS = 512
SUBLANES = 8
V7X_VMEM_CAP = 56 * 1024 * 1024

NORM_EPS = 1e-6
GN_EPS = 64e-5
L2_EPS = 1e-24
QK_EPS = 1e-6
SUBLN_EPS = 1e-5
ROPE_BASE = 10000.0
GRID_W = 64
DECAY_SCALE = math.exp(-0.5)
LOG2E = math.log2(math.e)
SCORE_MARGIN = 1.01
MAX_FIXED_SHIFT = 50.0


def _cparams(semantics, vmem_bytes):
    return pltpu.CompilerParams(dimension_semantics=semantics,
                                vmem_limit_bytes=int(min(max(vmem_bytes, 16 << 20), V7X_VMEM_CAP)))


def _dot(a, b):
    return jnp.dot(a, b, preferred_element_type=F32)


def _dot_nt(a, b):
    return lax.dot_general(a, b, (((1,), (1,)), ((), ())), preferred_element_type=F32)


def _dot_tn(a, b):
    return lax.dot_general(a, b, (((0,), (0,)), ((), ())), preferred_element_type=F32)


def _split_dot(x, w_bf16):
    hi = x.astype(BF16)
    lo = (x - hi.astype(F32)).astype(BF16)
    return _dot(hi, w_bf16) + _dot(lo, w_bf16)


def _block_mask(n):
    r = lax.broadcasted_iota(jnp.int32, (n, n), 0) // HEAD
    c = lax.broadcasted_iota(jnp.int32, (n, n), 1) // HEAD
    return r == c


def _cast_kernel(w_ref, o_ref):
    o_ref[...] = w_ref[...].astype(o_ref.dtype)


def _weight_bf16(w, layer):
    _, rows, cols = w.shape
    tr = CAST_ROWS
    assert rows % tr == 0
    return pl.pallas_call(
        _cast_kernel,
        grid=(rows // tr,),
        in_specs=[pl.BlockSpec((None, tr, cols), lambda i: (layer, i, 0))],
        out_specs=pl.BlockSpec((tr, cols), lambda i: (i, 0)),
        out_shape=jax.ShapeDtypeStruct((rows, cols), BF16),
        compiler_params=_cparams(("parallel",), 2 * tr * cols * 6 + tr * cols * 4),
        name="weight_bf16",
    )(w)


def _adaln_kernel(c_ref, w_ref, b_ref, o_ref):
    c = c_ref[...]
    s = (c * jax.nn.sigmoid(c)).astype(BF16)
    o_ref[...] = _dot(s, w_ref[...].astype(BF16)) + b_ref[...]


def _adaln(cvecs, mod_w, mod_b):
    depth, d, n = mod_w.shape
    rows = cvecs.shape[0]
    tn = 1024
    return pl.pallas_call(
        _adaln_kernel,
        grid=(depth, n // tn),
        in_specs=[pl.BlockSpec((rows, d), lambda l, j: (0, 0)),
                  pl.BlockSpec((None, d, tn), lambda l, j: (l, 0, j)),
                  pl.BlockSpec((None, 1, tn), lambda l, j: (l, 0, j))],
        out_specs=pl.BlockSpec((None, rows, tn), lambda l, j: (l, 0, j)),
        out_shape=jax.ShapeDtypeStruct((depth, rows, n), F32),
        compiler_params=_cparams(("parallel", "parallel"), 3 * d * tn * 4 + (8 << 20)),
        name="adaln",
    )(cvecs, mod_w, mod_b.reshape(depth, 1, n))


def _norm_mod_value(x, g, sh, sc):
    y = x * lax.rsqrt(jnp.mean(x * x, axis=-1, keepdims=True) + NORM_EPS)
    return (y * g) * (1.0 + sc) + sh


def _row_spec(arr, d):
    if arr.shape[0] == 1:
        return pl.BlockSpec((None, 1, d), lambda b, *_: (0, 0, 0))
    return pl.BlockSpec((None, 1, d), lambda b, *_: (b, 0, 0))


def _norm_specs(g, sh, sc, d):
    return [pl.BlockSpec((1, d), lambda b, *_: (0, 0)), _row_spec(sh, d), _row_spec(sc, d)]


def _mix_kernel(x_ref, xp_ref, xn_ref, g_ref, sh_ref, sc_ref, mu_ref, *o_refs, nt):
    t = pl.program_id(1)
    tm, d = x_ref.shape

    def inv_rms(ref):
        x = ref[...]
        return lax.rsqrt(jnp.mean(x * x, axis=-1, keepdims=True) + NORM_EPS)

    r, r_prev, r_next = inv_rms(x_ref), inv_rms(xp_ref), inv_rms(xn_ref)
    row = lax.broadcasted_iota(jnp.int32, (tm, GROUP), 0)
    for c0 in range(0, d, GROUP):
        sl = slice(c0, c0 + GROUP)
        g, sh, sc = g_ref[:, sl], sh_ref[:, sl], sc_ref[:, sl]

        def norm(ref, rinv):
            return ((ref[:, sl] * rinv) * g) * (1.0 + sc) + sh

        h = norm(x_ref, r)
        prev_row = jnp.where(t > 0, norm(xp_ref, r_prev)[SUBLANES - 1:SUBLANES, :], 0.0)
        next_row = jnp.where(t < nt - 1, norm(xn_ref, r_next)[0:1, :], 0.0)
        hp = jnp.where(row == 0, prev_row, pltpu.roll(h, 1, 0))
        hn = jnp.where(row == tm - 1, next_row, pltpu.roll(h, tm - 1, 0))
        xx = 0.5 * (hp + hn) - h
        for j, o_ref in enumerate(o_refs):
            o_ref[:, sl] = (h + xx * mu_ref[j:j + 1, sl]).astype(o_ref.dtype)


def _rwkv_mix(x, g, sh, sc, mu):
    b, t, d = x.shape
    n_mix = mu.shape[0]
    tm = min(t, 256)
    nt = t // tm
    per = tm // SUBLANES
    last = t // SUBLANES - 1
    spec = pl.BlockSpec((None, tm, d), lambda i, j: (i, j, 0))
    halo = [pl.BlockSpec((None, SUBLANES, d), lambda i, j: (i, jnp.maximum(j * per - 1, 0), 0)),
            pl.BlockSpec((None, SUBLANES, d), lambda i, j: (i, jnp.minimum((j + 1) * per, last), 0))]
    return pl.pallas_call(
        functools.partial(_mix_kernel, nt=nt),
        grid=(b, nt),
        in_specs=[spec] + halo + _norm_specs(g, sh, sc, d) + [pl.BlockSpec(mu.shape, lambda i, j: (0, 0))],
        out_specs=[spec] * n_mix,
        out_shape=[jax.ShapeDtypeStruct((b, t, d), BF16)] * n_mix,
        compiler_params=_cparams(("parallel", "parallel"), 2 * tm * d * (4 + 2 * n_mix) + 8 * tm * d * 4),
        name="rwkv_mix",
    )(x, x, x, g.reshape(1, d), sh, sc, mu)


def _rkv_kernel(xr_ref, xk_ref, xv_ref, wr_ref, wk_ref, wv_ref, r_ref, k_ref, v_ref):
    r_ref[...] = _dot(xr_ref[...], wr_ref[...])
    k_ref[...] = _dot(xk_ref[...], wk_ref[...])
    v_ref[...] = _dot(xv_ref[...], wv_ref[...])


def _rwkv_rkv(xr, xk, xv, wr, wk, wv):
    b, t, d = xr.shape
    tm = min(t, 1024)
    tn = 512
    out = jax.ShapeDtypeStruct((b, t, d), F32)
    xspec = pl.BlockSpec((None, tm, d), lambda i, j, n: (i, j, 0))
    wspec = pl.BlockSpec((d, tn), lambda i, j, n: (0, n))
    ospec = pl.BlockSpec((None, tm, tn), lambda i, j, n: (i, j, n))
    vmem = 6 * tm * d * 2 + 6 * d * tn * 2 + 9 * tm * tn * 4 + (4 << 20)
    return pl.pallas_call(
        _rkv_kernel,
        grid=(b, t // tm, d // tn),
        in_specs=[xspec, xspec, xspec, wspec, wspec, wspec],
        out_specs=[ospec, ospec, ospec],
        out_shape=[out, out, out],
        compiler_params=_cparams(("parallel", "parallel", "parallel"), vmem),
        name="rwkv_rkv",
    )(xr, xk, xv, wr, wk, wv)


def _lora_kernel(xw_ref, xa_ref, xg_ref, w0_ref, w1_ref, w2_ref, a0_ref, a1_ref, a2_ref, g1_ref, g2_ref,
                 g_ref, lw0_ref, lw1_ref, as0_ref, as1_ref):
    xw, xa, xg = xw_ref[...], xa_ref[...], xg_ref[...]
    g_ref[...] = _dot(jax.nn.sigmoid(_dot(xg, g1_ref[...])).astype(BF16), g2_ref[...]).astype(g_ref.dtype)
    for d, (lw_ref, as_ref) in enumerate(((lw0_ref, as0_ref), (lw1_ref, as1_ref))):
        w_pre = w0_ref[d:d + 1, :] + _dot(jnp.tanh(_dot(xw, w1_ref[d])).astype(BF16), w2_ref[d])
        lw_ref[...] = (-DECAY_SCALE * LOG2E) * jax.nn.sigmoid(w_pre)
        a_pre = a0_ref[d:d + 1, :] + _dot(_dot(xa, a1_ref[d]).astype(BF16), a2_ref[d])
        as_ref[...] = jax.nn.sigmoid(a_pre).astype(as_ref.dtype)


def _pad_lora(w_in, w_out):
    rank = w_in.shape[-1]
    pad = (-rank) % 128
    w_in = jnp.pad(w_in, [(0, 0)] * (w_in.ndim - 1) + [(0, pad)])
    w_out = jnp.pad(w_out, [(0, 0)] * (w_out.ndim - 2) + [(0, pad), (0, 0)])
    return w_in.astype(BF16), w_out.astype(BF16)


def _rwkv_lora(xw, xa, xg, w0, w1, w2, a0, a1, a2, g1, g2):
    b, t, d = xw.shape
    tm = min(t, 256)
    out = jax.ShapeDtypeStruct((b, t, d), F32)
    out16 = jax.ShapeDtypeStruct((b, t, d), BF16)
    spec = pl.BlockSpec((None, tm, d), lambda i, j: (i, j, 0))

    def full(a):
        return pl.BlockSpec(a.shape, lambda i, j, _n=a.ndim: (0,) * _n)

    params = (w0, w1, w2, a0, a1, a2, g1, g2)
    vmem = 2 * tm * d * (3 * 2 + 2 * 4 + 3 * 2) + 10 * tm * d * 4 + 2 * sum(p.size * p.dtype.itemsize for p in params)
    return pl.pallas_call(
        _lora_kernel,
        grid=(b, t // tm),
        in_specs=[spec] * 3 + [full(p) for p in params],
        out_specs=[spec] * 5,
        out_shape=[out16, out, out, out16, out16],
        compiler_params=_cparams(("parallel", "parallel"), vmem),
        name="rwkv_lora",
    )(xw, xa, xg, *params)


def _chunk_cumsum(x, reverse):
    n = x.shape[0]
    row = lax.broadcasted_iota(jnp.int32, x.shape, 0)
    shift = 1
    while shift < n:
        if reverse:
            x = x + jnp.where(row < n - shift, pltpu.roll(x, n - shift, 0), 0.0)
        else:
            x = x + jnp.where(row >= shift, pltpu.roll(x, shift, 0), 0.0)
        shift *= 2
    return x


def _scan_chain(d, refs, kks, ka, state_ref, y_ref, sl, g, consts):
    r_ref, k_ref, v_ref, lw_ref, as_ref = refs
    bmask, ones_bd, m_strict, m_incl = consts
    c = CHUNK

    def stack(x):
        return jnp.where(bmask, jnp.concatenate([x] * (GROUP // c), axis=0), 0.0).astype(BF16)

    lw = lw_ref[:, sl]
    a_sig = as_ref[:, sl].astype(F32)
    k = k_ref[:, sl]
    kk = k * kks
    ssq = _dot((kk * kk).astype(BF16), ones_bd)
    cum = _chunk_cumsum(lw, reverse=d == 1)
    yield

    kk = kk * lax.rsqrt(jnp.maximum(ssq, L2_EPS))
    b_vec = kk * a_sig
    k_d = k * (1.0 + (a_sig - 1.0) * ka)
    tot = cum[c - 1:c, :] if d == 0 else cum[0:1, :]
    a_t = -kk * jnp.exp2(cum - lw)
    r_t = r_ref[:, sl] * jnp.exp2(cum)
    p_inv = jnp.exp2(-cum)
    lhs = jnp.concatenate([a_t, r_t], axis=0).astype(BF16)
    rhs = jnp.concatenate([stack(b_vec * p_inv), stack(k_d * p_inv)], axis=0)
    aa = _dot_nt(lhs, rhs)
    yield

    n1 = jnp.where(m_strict[d], aa[:c, :GROUP], 0.0)
    a_ak = jnp.where(m_strict[d], aa[:c, GROUP:], 0.0)
    a_rb = jnp.where(m_incl[d], aa[c:, :GROUP], 0.0).astype(BF16)
    a_rk = jnp.where(m_incl[d], aa[c:, GROUP:], 0.0)
    v = v_ref[:, sl]
    z1 = _dot(jnp.concatenate([a_ak, a_rk], axis=0).astype(BF16), stack(v))
    q = n1
    npow = _dot(n1.astype(BF16), stack(n1))
    yield
    span = 2
    while 2 * span < c:
        z = _dot(jnp.concatenate([npow, q], axis=0).astype(BF16), stack(npow))
        yield
        q = q + npow + z[c:]
        npow = z[:c]
        span *= 2
    zq = _dot(q.astype(BF16), stack(npow))
    yield
    q = q + npow + zq

    v_ak = z1[:c]
    y0 = z1[c:]
    z2 = _dot(q.astype(BF16), jnp.concatenate([stack(a_t), stack(v_ak)], axis=1))
    yield
    w = a_t + z2[:, :GROUP]
    u0 = v_ak + z2[:, GROUP:]

    s = state_ref[d, g]
    z3 = _dot_nt(jnp.concatenate([w, r_t], axis=0).astype(BF16), s.astype(BF16))
    yield
    u = z3[:c] + u0
    p_end = jnp.exp2(tot - cum)
    yu = _dot(a_rb, stack(u))
    upd = _dot_tn(jnp.concatenate([u, v], axis=0).astype(BF16),
                  jnp.concatenate([b_vec * p_end, k_d * p_end], axis=0).astype(BF16))
    yield
    y_ref[:, sl] = (z3[c:] + yu + y0).astype(y_ref.dtype)
    state_ref[d, g] = s * jnp.exp2(tot) + jnp.where(bmask, upd, 0.0)


def _interleave(chains, skew=0.0):
    pending = list(enumerate(chains))
    rnd = 0
    while pending:
        alive = []
        for i, ch in pending:
            if rnd >= (i % 2) * skew:
                try:
                    next(ch)
                except StopIteration:
                    continue
            alive.append((i, ch))
        pending = alive
        rnd += 1


def _scan_kernel(r0, k0, v0, lw0, as0, r1, k1, v1, lw1, as1, kks_ref, ka_ref, sin_ref,
                 y0_ref, y1_ref, state_ref, *, groups):
    i = pl.program_id(1)
    c = CHUNK
    unroll = math.gcd(groups, SCAN_UNROLL)

    @pl.when(i == 0)
    def _():
        state_ref[...] = sin_ref[...]

    bmask = _block_mask(GROUP)
    ones_bd = jnp.where(bmask, 1.0, 0.0).astype(BF16)
    tp = lax.broadcasted_iota(jnp.int32, (c, GROUP), 0)
    sp = lax.broadcasted_iota(jnp.int32, (c, GROUP), 1) % c
    m_strict = (sp < tp, sp > tp)
    m_incl = (sp <= tp, sp >= tp)
    consts = (bmask, ones_bd, m_strict, m_incl)

    def body(it, carry):
        chains = []
        for j in range(unroll):
            g = it * unroll + j
            sl = pl.ds(pl.multiple_of(g * GROUP, GROUP), GROUP)
            kks = kks_ref[:, sl]
            ka = ka_ref[:, sl]
            chains.append(_scan_chain(0, (r0, k0, v0, lw0, as0), kks, ka, state_ref, y0_ref, sl, g, consts))
            chains.append(_scan_chain(1, (r1, k1, v1, lw1, as1), kks, ka, state_ref, y1_ref, sl, g, consts))
        _interleave(chains, SCAN_SKEW)
        return carry

    lax.fori_loop(0, groups // unroll, body, 0)


def _rwkv_scan(r, k, v, lw0, as0, lw1, as1, kk_scale, ka, s_init):
    b, t, d = r.shape
    assert CHUNK * (GROUP // HEAD) == GROUP and t % CHUNK == 0 and d % GROUP == 0
    groups = d // GROUP
    nc = t // CHUNK
    fwd = pl.BlockSpec((None, CHUNK, d), lambda i, j: (i, j, 0))
    bwd = pl.BlockSpec((None, CHUNK, d), lambda i, j: (i, nc - 1 - j, 0))
    vec = pl.BlockSpec((1, d), lambda i, j: (0, 0))
    sspec = pl.BlockSpec((None, 2, groups, GROUP, GROUP), lambda i, j: (i, 0, 0, 0, 0))
    y = jax.ShapeDtypeStruct((b, t, d), BF16)
    state_bytes = 2 * groups * GROUP * GROUP * 4
    vmem = 2 * 12 * CHUNK * d * 4 + 4 * state_bytes + 2 * SCAN_UNROLL * (3 << 19) + (4 << 20)
    return pl.pallas_call(
        functools.partial(_scan_kernel, groups=groups),
        grid=(b, nc),
        in_specs=[fwd] * 5 + [bwd] * 5 + [vec, vec, sspec],
        out_specs=[fwd, bwd, sspec],
        out_shape=[y, y, jax.ShapeDtypeStruct(s_init.shape, F32)],
        compiler_params=_cparams(("parallel", "arbitrary"), vmem),
        name="rwkv_scan",
    )(r, k, v, lw0, as0, r, k, v, lw1, as1, kk_scale.reshape(1, d), ka.reshape(1, d), s_init)


def _readout_kernel(y0_ref, y1_ref, r_ref, k_ref, v_ref, as0_ref, as1_ref, g_ref,
                    ka_ref, rk_ref, lnw_ref, lnb_ref, o_ref):
    d = o_ref.shape[-1]
    ones_bd = jnp.where(_block_mask(GROUP), 1.0, 0.0).astype(BF16)
    for j in range(d // GROUP):
        sl = slice(j * GROUP, (j + 1) * GROUP)
        y = y0_ref[:, sl].astype(F32) + y1_ref[:, sl].astype(F32)
        mean = _split_dot(y, ones_bd) * (1.0 / HEAD)
        yc = y - mean
        var = _split_dot(yc * yc, ones_bd) * (1.0 / HEAD)
        yn = (yc * lax.rsqrt(var + GN_EPS)) * lnw_ref[:, sl] + lnb_ref[:, sl]
        k = k_ref[:, sl]
        ka = ka_ref[:, sl]
        a0, a1 = as0_ref[:, sl].astype(F32), as1_ref[:, sl].astype(F32)
        k_sum = k * (1.0 + (a0 - 1.0) * ka) + k * (1.0 + (a1 - 1.0) * ka)
        bonus = _split_dot(r_ref[:, sl] * k_sum * rk_ref[:, sl], ones_bd) * v_ref[:, sl]
        o_ref[:, sl] = ((yn + bonus) * g_ref[:, sl].astype(F32)).astype(o_ref.dtype)


def _rwkv_readout(y0, y1, r, k, v, as0, as1, g, ka, rk, lnw, lnb):
    b, t, d = r.shape
    tm = min(t, 256)
    spec = pl.BlockSpec((None, tm, d), lambda i, j: (i, j, 0))
    vec = pl.BlockSpec((1, d), lambda i, j: (0, 0))
    return pl.pallas_call(
        _readout_kernel,
        grid=(b, t // tm),
        in_specs=[spec] * 8 + [vec] * 4,
        out_specs=spec,
        out_shape=jax.ShapeDtypeStruct((b, t, d), BF16),
        compiler_params=_cparams(("parallel", "parallel"), 2 * 9 * tm * d * 4 + (8 << 20)),
        name="rwkv_readout",
    )(y0, y1, r, k, v, as0, as1, g, ka.reshape(1, d), rk.reshape(1, d), lnw.reshape(1, d), lnb.reshape(1, d))


def _mm_res_kernel(z_ref, w_ref, res_ref, gate_ref, o_ref):
    o_ref[...] = res_ref[...] + gate_ref[...] * _dot(z_ref[...], w_ref[...])


def _norm_mm_kernel(x_ref, g_ref, sh_ref, sc_ref, w_ref, o_ref, h_ref):
    @pl.when(pl.program_id(2) == 0)
    def _():
        h_ref[...] = _norm_mod_value(x_ref[...], g_ref[...], sh_ref[...], sc_ref[...]).astype(BF16)

    o_ref[...] = _dot(h_ref[...], w_ref[...])


def _norm_matmul(x, g, sh, sc, w, col0=0):
    b, t, d = x.shape
    n = w.shape[1] - col0
    tm = min(t, 1024)
    tn = 512
    assert col0 % tn == 0
    cb0 = col0 // tn
    vmem = 2 * tm * d * 4 + tm * d * 2 + 2 * d * tn * 2 + 3 * tm * tn * 4 + tm * d * 4 + (4 << 20)
    return pl.pallas_call(
        _norm_mm_kernel,
        grid=(b, t // tm, n // tn),
        in_specs=[pl.BlockSpec((None, tm, d), lambda i, j, l: (i, j, 0))] + _norm_specs(g, sh, sc, d)
                 + [pl.BlockSpec((d, tn), lambda i, j, l: (0, l + cb0))],
        out_specs=pl.BlockSpec((None, tm, tn), lambda i, j, l: (i, j, l)),
        out_shape=jax.ShapeDtypeStruct((b, t, n), F32),
        scratch_shapes=[pltpu.VMEM((tm, d), BF16)],
        compiler_params=_cparams(("parallel", "parallel", "arbitrary"), vmem),
        name="norm_matmul",
    )(x, g.reshape(1, d), sh, sc, w)


def _matmul_res(z, w, res, gate):
    b, t, kdim = z.shape
    n = w.shape[1]
    tm = min(t, 1024)
    tn = 512
    zspec = pl.BlockSpec((None, tm, kdim), lambda i, j, l: (i, j, 0))
    wspec = pl.BlockSpec((kdim, tn), lambda i, j, l: (0, l))
    ospec = pl.BlockSpec((None, tm, tn), lambda i, j, l: (i, j, l))
    gspec = (pl.BlockSpec((None, 1, tn), lambda i, j, l: (0, 0, l)) if gate.shape[0] == 1
             else pl.BlockSpec((None, 1, tn), lambda i, j, l: (i, 0, l)))
    vmem = 2 * (tm * kdim * 2 + kdim * tn * 2 + 2 * tm * tn * 4) + tm * tn * 4 + (4 << 20)
    return pl.pallas_call(
        _mm_res_kernel,
        grid=(b, t // tm, n // tn),
        in_specs=[zspec, wspec, ospec, gspec],
        out_specs=ospec,
        out_shape=jax.ShapeDtypeStruct((b, t, n), F32),
        compiler_params=_cparams(("parallel", "parallel", "parallel"), vmem),
        name="matmul_res",
    )(z, w, res, gate)


def _ffn_kernel(x_ref, g_ref, sh_ref, sc_ref, gate_ref, wg_ref, wu_ref, wd_ref, o_ref, h_ref):
    f = pl.program_id(2)

    @pl.when(f == 0)
    def _():
        h_ref[...] = _norm_mod_value(x_ref[...], g_ref[...], sh_ref[...], sc_ref[...]).astype(BF16)
        o_ref[...] = jnp.zeros_like(o_ref)

    h = h_ref[...]
    gp = _dot(h, wg_ref[...])
    act = ((gp * jax.nn.sigmoid(gp)) * _dot(h, wu_ref[...])).astype(BF16)
    d = o_ref.shape[-1]
    for c0 in range(0, d, FFN_ACC_COLS):
        cols = slice(c0, c0 + FFN_ACC_COLS)
        o_ref[:, cols] += _dot(act, wd_ref[:, cols])

    @pl.when(f == pl.num_programs(2) - 1)
    def _():
        o_ref[...] = x_ref[...] + gate_ref[...] * o_ref[...]


def _ffn(x, g, sh, sc, gate, wg, wu, wd):
    b, t, d = x.shape
    ff = wg.shape[1]
    tm = min(t, 1024)
    tf = 512
    xspec = pl.BlockSpec((None, tm, d), lambda i, j, f: (i, j, 0))
    xin = pl.BlockSpec((None, tm, d), lambda i, j, f: (i, j, 0), pipeline_mode=pl.Buffered(1))
    vmem = 3 * tm * d * 4 + tm * d * 2 + 2 * 3 * d * tf * 2 + 6 * tm * tf * 4 + tm * d * 4 + (4 << 20)
    return pl.pallas_call(
        _ffn_kernel,
        grid=(b, t // tm, ff // tf),
        in_specs=[xin, pl.BlockSpec((1, d), lambda i, j, f: (0, 0)),
                  _row_spec(sh, d), _row_spec(sc, d), _row_spec(gate, d),
                  pl.BlockSpec((d, tf), lambda i, j, f: (0, f)),
                  pl.BlockSpec((d, tf), lambda i, j, f: (0, f)),
                  pl.BlockSpec((tf, d), lambda i, j, f: (f, 0))],
        out_specs=xspec,
        out_shape=jax.ShapeDtypeStruct((b, t, d), F32),
        scratch_shapes=[pltpu.VMEM((tm, d), BF16)],
        compiler_params=_cparams(("parallel", "parallel", "arbitrary"), vmem),
        name="ffn",
    )(x, g.reshape(1, d), sh, sc, gate, wg, wu, wd)


def _qk_norm(x, gain, ones_bd):
    ms = _split_dot(x * x, ones_bd) * (1.0 / HEAD)
    return x * lax.rsqrt(ms + QK_EPS) * gain


def _rope(u, cos, sin):
    n = u.shape[-1]
    lane = lax.broadcasted_iota(jnp.int32, u.shape, 1)
    partner = jnp.where(lane % 32 < 16, pltpu.roll(u, n - 16, 1), pltpu.roll(u, 16, 1))
    return u * cos + partner * sin


def _qkv_prep_kernel(*refs, rope, with_q):
    refs = list(refs)
    q_ref = refs.pop(0) if with_q else None
    k_ref, v_ref = refs.pop(0), refs.pop(0)
    qn_ref = refs.pop(0) if with_q else None
    kn_ref = refs.pop(0)
    if rope:
        cos_ref, sin_ref = refs.pop(0), refs.pop(0)
    qo_ref = refs.pop(0) if with_q else None
    ko_ref, vo_ref = refs
    d = ko_ref.shape[-1]
    ones_bd = jnp.where(_block_mask(GROUP), 1.0, 0.0).astype(BF16)
    if rope:
        cos, sin = cos_ref[...], sin_ref[...]
    for j in range(d // GROUP):
        sl = slice(j * GROUP, (j + 1) * GROUP)
        kh = _qk_norm(k_ref[:, sl], kn_ref[:, sl], ones_bd)
        if rope:
            kh = _rope(kh, cos, sin)
        ko_ref[:, sl] = kh.astype(BF16)
        if with_q:
            qh = _qk_norm(q_ref[:, sl], qn_ref[:, sl], ones_bd)
            if rope:
                qh = _rope(qh, cos, sin)
            qo_ref[:, sl] = (qh * (HEAD ** -0.5 * LOG2E)).astype(BF16)
    vo_ref[...] = v_ref[...].T.astype(BF16)


def _qkv_prep(qkv, qn, kn, cos, sin, *, with_q):
    b, t, width = qkv.shape
    parts = 3 if with_q else 2
    d = width // parts
    rope = cos is not None
    tm = min(t, 256)
    col = lambda c: pl.BlockSpec((None, tm, d), lambda i, j, _c=c: (i, j, _c))
    vec = pl.BlockSpec((1, d), lambda i, j: (0, 0))
    tab = pl.BlockSpec((tm, GROUP), lambda i, j: (j, 0))
    in_specs = [col(c) for c in range(parts)] + [vec] * (2 if with_q else 1) + ([tab, tab] if rope else [])
    args = [qkv] * parts + ([qn] if with_q else []) + [kn] + ([cos, sin] if rope else [])
    out = jax.ShapeDtypeStruct((b, t, d), BF16)
    ospec = pl.BlockSpec((None, tm, d), lambda i, j: (i, j, 0))
    out_t = jax.ShapeDtypeStruct((b, d, t), BF16)
    ospec_t = pl.BlockSpec((None, d, tm), lambda i, j: (i, 0, j))
    return pl.pallas_call(
        functools.partial(_qkv_prep_kernel, rope=rope, with_q=with_q),
        grid=(b, t // tm),
        in_specs=in_specs,
        out_specs=[ospec] * (parts - 1) + [ospec_t],
        out_shape=[out] * (parts - 1) + [out_t],
        compiler_params=_cparams(("parallel", "parallel"), 2 * parts * tm * d * 6 + (16 << 20)),
        name="qkv_prep",
    )(*args)


def _attn_chain(q2, k, vt, state, shift):
    m, acc_t = state
    s_t = _dot_nt(k, q2)
    yield
    if shift is None:
        m_new = jnp.maximum(m, jnp.max(s_t, axis=0, keepdims=True))
        alpha = jnp.exp2(m - m_new)
        state[0] = m_new
        p = jnp.exp2(s_t - m_new)
    else:
        p = jnp.exp2(s_t - shift)
    pv = _dot(vt, p.astype(BF16))
    yield
    state[1] = pv + (acc_t if shift is not None else alpha * acc_t)


def _attn_kernel(lam_ref, q_ref, kc_ref, vtc_ref, kl_ref, vtl_ref, sub_ref, o_ref, *, tk, tq, lambda_init):
    n_sub = q_ref.shape[0] // tq
    lp = lam_ref[...]
    lam = (jnp.exp(jnp.sum(lp[0:1] * lp[1:2], axis=-1, keepdims=True))
           - jnp.exp(jnp.sum(lp[2:3] * lp[3:4], axis=-1, keepdims=True)) + lambda_init)
    bound = (HEAD ** 0.5 * LOG2E * SCORE_MARGIN) * (jnp.max(jnp.abs(lp[4:5]), axis=-1, keepdims=True)
                                                    * jnp.max(jnp.abs(lp[5:6]), axis=-1, keepdims=True))
    fixed_shift_ok = bound[0, 0] <= MAX_FIXED_SHIFT
    q2s = []
    for c in range(n_sub):
        q = q_ref[c * tq:(c + 1) * tq, :]
        lane = lax.broadcasted_iota(jnp.int32, q.shape, 1)
        zero = jnp.zeros_like(q)
        q2s.append(jnp.concatenate([jnp.where(lane < HEAD, q, zero), jnp.where(lane >= HEAD, q, zero)], axis=0))

    hw = 2 * HEAD
    ones_rows = 16

    def run(shift):
        def step(k, v_t, carry):
            vt = jnp.concatenate([v_t, jnp.ones((ones_rows, k.shape[0]), BF16)], axis=0)
            states = [list(st) for st in carry]
            _interleave([_attn_chain(q2, k, vt, st, shift) for q2, st in zip(q2s, states)])
            return tuple(tuple(st) for st in states)

        def latent_step(j, carry):
            ks = pl.ds(pl.multiple_of(j * tk, tk), tk)
            return step(kl_ref[ks, :], vtl_ref[:, ks], carry)

        init = (jnp.full((1, 2 * tq), -jnp.inf, F32), jnp.zeros((hw + ones_rows, 2 * tq), F32))
        carry = step(kc_ref[...], vtc_ref[...], (init,) * n_sub)
        carry = lax.fori_loop(0, kl_ref.shape[0] // tk, latent_step, carry)
        for c, (_, acc) in enumerate(carry):
            acc_t, l = acc[:hw], acc[hw:hw + 1]
            o_t = acc_t[:, :tq] / l[:, :tq] - lam * (acc_t[:, tq:] / l[:, tq:])
            o = o_t.T
            o = o * lax.rsqrt(jnp.mean(o * o, axis=-1, keepdims=True) + SUBLN_EPS) * sub_ref[...]
            o_ref[c * tq:(c + 1) * tq, :] = (o * (1.0 - lambda_init)).astype(o_ref.dtype)

    @pl.when(fixed_shift_ok)
    def _():
        run(bound)

    @pl.when(jnp.logical_not(fixed_shift_ok))
    def _():
        run(None)


def _diff_attention(q, k_c, vt_c, k_l, vt_l, lam_params, sub_gain, lambda_init):
    b, t, d = q.shape
    tc = k_c.shape[1]
    hw = 2 * HEAD
    tq = 256
    n_sub = math.gcd(t // tq, ATTN_CHAINS)
    tk = min(t, 2048)
    assert t % (tq * n_sub) == 0 and t % tk == 0 and tc % 128 == 0
    qspec = pl.BlockSpec((None, tq * n_sub, hw), lambda i, h, j: (i, j, h))

    def kspec(n):
        return pl.BlockSpec((None, n, hw), lambda i, h, j: (i, 0, h))

    def vspec(n):
        return pl.BlockSpec((None, hw, n), lambda i, h, j: (i, h, 0))

    vmem = 4 * (t + tc) * hw * 2 + 4 * n_sub * tq * hw * 2 + n_sub * 8 * 2 * tq * tk * 4 + (8 << 20)
    return pl.pallas_call(
        functools.partial(_attn_kernel, tk=tk, tq=tq, lambda_init=lambda_init),
        grid=(b, d // hw, t // (tq * n_sub)),
        in_specs=[pl.BlockSpec(lam_params.shape, lambda i, h, j: (0, 0)), qspec,
                  kspec(tc), vspec(tc), kspec(t), vspec(t), pl.BlockSpec((1, hw), lambda i, h, j: (0, 0))],
        out_specs=qspec,
        out_shape=jax.ShapeDtypeStruct((b, t, d), BF16),
        compiler_params=_cparams(("parallel", "parallel", "parallel"), vmem),
        name="diff_attention",
    )(lam_params, q, k_c, vt_c, k_l, vt_l, sub_gain.reshape(1, hw))


def _rope_tables(n_tokens):
    n_freq = HEAD // 4
    pos = jnp.arange(n_tokens)
    freqs = ROPE_BASE ** (-jnp.arange(n_freq, dtype=F32) / n_freq)
    ang_row = (pos // GRID_W).astype(F32)[:, None] * freqs
    ang_col = (pos % GRID_W).astype(F32)[:, None] * freqs
    cos = jnp.concatenate([jnp.cos(ang_row)] * 2 + [jnp.cos(ang_col)] * 2, axis=-1)
    sin = jnp.concatenate([-jnp.sin(ang_row), jnp.sin(ang_row), -jnp.sin(ang_col), jnp.sin(ang_col)], axis=-1)
    reps = GROUP // HEAD
    return jnp.tile(cos, (1, reps)), jnp.tile(sin, (1, reps))


def _mod_rows(mod, layer, j, d, batch):
    m = mod[layer, :, j * d:(j + 1) * d]
    return m[:batch, None, :], m[batch:batch + 1, None, :]


def _rwkv_layer(xs, mods, norm_g, p):
    rkv, lora, outs = [], [], []
    for x, (sh, sc, _) in zip(xs, mods):
        xr, xw, xk, xv, xa, xg = _rwkv_mix(x, norm_g, sh, sc, p["mu"])
        rkv.append(_rwkv_rkv(xr, xk, xv, p["wr"], p["wk"], p["wv"]))
        lora.append(_rwkv_lora(xw, xa, xg, p["w0"], p["w1"], p["w2"], p["a0"], p["a1"], p["a2"], p["g1"], p["g2"]))
    b, _, d = xs[0].shape
    state = jnp.zeros((b, 2, d // GROUP, GROUP, GROUP), F32)
    for x, (_, _, gate), (r, k, v), (g, lw0, lw1, as0, as1) in zip(xs, mods, rkv, lora):
        y0, y1, state = _rwkv_scan(r, k, v, lw0, as0, lw1, as1, p["kk"], p["ka"], state)
        z = _rwkv_readout(y0, y1, r, k, v, as0, as1, g, p["ka"], p["rk"], p["lnw"], p["lnb"])
        outs.append(_matmul_res(z, p["wo"], x, gate))
    return outs


def kernel(x, c, ctx, c_ctx, mod_w, mod_b, norm1_g, norm2_g, rwkv_mu, rwkv_wr, rwkv_wk, rwkv_wv, rwkv_wo, rwkv_w0, rwkv_w1, rwkv_w2, rwkv_a0, rwkv_a1, rwkv_a2, rwkv_g1, rwkv_g2, rwkv_kk, rwkv_ka, rwkv_rk, rwkv_lnw, rwkv_lnb, diff_wqkv, diff_qn, diff_kn, diff_lq1, diff_lk1, diff_lq2, diff_lk2, diff_subln, diff_wo, ffn_wg, ffn_wu, ffn_wd):
    batch, n_lat, d = x.shape
    depth = mod_w.shape[0]
    assert depth == 2, "layer 0 is the RWKV-7 mixer, layer 1 differential attention"

    cvecs = jnp.zeros((SUBLANES, d), F32).at[:batch].set(c).at[batch].set(c_ctx)
    mod = _adaln(cvecs, mod_w, mod_b)

    def mods(layer, first):
        lat, cx = zip(*[_mod_rows(mod, layer, first + j, d, batch) for j in range(3)])
        return cx, lat

    w1, w2 = _pad_lora(rwkv_w1[0], rwkv_w2[0])
    a1, a2 = _pad_lora(rwkv_a1[0], rwkv_a2[0])
    g1, g2 = _pad_lora(rwkv_g1[0], rwkv_g2[0])
    p = dict(mu=rwkv_mu[0], wr=_weight_bf16(rwkv_wr, 0), wk=_weight_bf16(rwkv_wk, 0), wv=_weight_bf16(rwkv_wv, 0),
             wo=_weight_bf16(rwkv_wo, 0), w0=rwkv_w0[0], w1=w1, w2=w2, a0=rwkv_a0[0], a1=a1, a2=a2,
             g1=g1, g2=g2, kk=rwkv_kk[0], ka=rwkv_ka[0],
             rk=rwkv_rk[0], lnw=rwkv_lnw[0], lnb=rwkv_lnb[0])
    xc, xl = _rwkv_layer((ctx, x), mods(0, 0), norm1_g[0], p)
    wg, wu, wd = _weight_bf16(ffn_wg, 0), _weight_bf16(ffn_wu, 0), _weight_bf16(ffn_wd, 0)
    (csh, csc, cgt), (lsh, lsc, lgt) = mods(0, 3)
    xc = _ffn(xc, norm2_g[0], csh, csc, cgt, wg, wu, wd)
    xl = _ffn(xl, norm2_g[0], lsh, lsc, lgt, wg, wu, wd)

    (csh, csc, _), (lsh, lsc, lgt) = mods(1, 0)
    lambda_init = 0.8 - 0.6 * math.exp(-0.3 * 1)
    wqkv = _weight_bf16(diff_wqkv, 0)
    qn = jnp.tile(diff_qn[0], d // HEAD).reshape(1, d)
    kn = jnp.tile(diff_kn[0], d // HEAD).reshape(1, d)
    cos, sin = _rope_tables(n_lat)
    q_l, k_l, v_l = _qkv_prep(_norm_matmul(xl, norm1_g[1], lsh, lsc, wqkv), qn, kn, cos, sin, with_q=True)
    k_c, v_c = _qkv_prep(_norm_matmul(xc, norm1_g[1], csh, csc, wqkv, col0=d), None, kn, None, None, with_q=False)
    lam_params = jnp.stack([diff_lq1[0], diff_lk1[0], diff_lq2[0], diff_lk2[0], diff_qn[0], diff_kn[0]])
    o = _diff_attention(q_l, k_c, v_c, k_l, v_l, lam_params, diff_subln[0], lambda_init)
    xl = _matmul_res(o, _weight_bf16(diff_wo, 0), xl, lgt)
    (_, _, _), (lsh, lsc, lgt) = mods(1, 3)
    return _ffn(xl, norm2_g[1], lsh, lsc, lgt,
                _weight_bf16(ffn_wg, 1), _weight_bf16(ffn_wu, 1), _weight_bf16(ffn_wd, 1))
```
